```python
import math
import jax, jax.numpy as jnp
from jax import lax
import numpy as np

D_MODEL = 2048
BATCH = 2
SEQ = 16384
DEPTH = 1

MIX_WIDTH = D_MODEL
ATT_WIDTH = MIX_WIDTH // 2
SSM_WIDTH = MIX_WIDTH - ATT_WIDTH

ATT_HEAD_DIM = 64
ATT_HEADS = ATT_WIDTH // (2 * ATT_HEAD_DIM)
ATT_V_DIM = 2 * ATT_HEAD_DIM
ATT_BLOCK = 128
REL_BUCKETS = 32
REL_MAX_DIST = 128

SSM_HEAD_DIM = 64
SSM_HEADS = SSM_WIDTH // SSM_HEAD_DIM
SSM_GROUPS = 2
SSM_HEADS_PER_GROUP = SSM_HEADS // SSM_GROUPS
SSM_STATE = 128
SSM_CONV = 4
SSM_CHUNK = 128
SSM_CONV_DIM = SSM_WIDTH + 2 * SSM_GROUPS * SSM_STATE

Q_COLS = ATT_HEADS * 2 * ATT_HEAD_DIM
K_COLS = Q_COLS
V_COLS = ATT_HEADS * ATT_V_DIM
Z_COLS = SSM_WIDTH
XBC_COLS = SSM_CONV_DIM
DT_COLS = SSM_HEADS
IN_COLS = Q_COLS + K_COLS + V_COLS + Z_COLS + XBC_COLS + DT_COLS
IN_SPLITS = [Q_COLS, Q_COLS + K_COLS, Q_COLS + K_COLS + V_COLS,
             Q_COLS + K_COLS + V_COLS + Z_COLS,
             Q_COLS + K_COLS + V_COLS + Z_COLS + XBC_COLS]

PEER_HEADS = 8
PEER_NKEYS = 128
PEER_EXPERTS = PEER_NKEYS ** 2
PEER_TOPK = 16
PEER_QDIM = 256
PEER_HALF = PEER_QDIM // 2
PEER_BLOCK = 128

NORM_EPS = 1e-6

kernel_name = "hybrid_diffattn_ssd_peer_layer"


def rms_norm(x, w):
    xf = x.astype(jnp.float32)
    y = xf * lax.rsqrt(jnp.mean(xf * xf, axis=-1, keepdims=True) + NORM_EPS)
    return (y * w.astype(jnp.float32)).astype(x.dtype)


def lambda_init_fn(layer_idx):
    return 0.8 - 0.6 * math.exp(-0.3 * layer_idx)


def t5_bucket(rel):
    n = jnp.maximum(rel, 0)
    max_exact = REL_BUCKETS // 2
    nf = jnp.maximum(n, 1).astype(jnp.float32)
    large = max_exact + (jnp.log(nf / max_exact) / math.log(REL_MAX_DIST / max_exact)
                         * (REL_BUCKETS - max_exact)).astype(jnp.int32)
    large = jnp.minimum(large, REL_BUCKETS - 1)
    return jnp.where(n < max_exact, n, large)


def diff_attention(q, k, v, lam, rel_bias):
    b, s, h, _, dh = q.shape
    nb = s // ATT_BLOCK
    scale = dh ** -0.5
    kpos = jnp.arange(s, dtype=jnp.int32)

    def block(i):
        q_blk = lax.dynamic_slice_in_dim(q, i * ATT_BLOCK, ATT_BLOCK, axis=1)
        logits = jnp.einsum('bqhmd,bkhmd->bhmqk', q_blk, k).astype(jnp.float32) * scale
        qpos = i * ATT_BLOCK + jnp.arange(ATT_BLOCK, dtype=jnp.int32)
        rel = qpos[:, None] - kpos[None, :]
        bias = jnp.transpose(rel_bias[t5_bucket(rel)], (2, 3, 0, 1))
        logits = jnp.where(rel >= 0, logits + bias.astype(jnp.float32), -jnp.inf)
        p = jax.nn.softmax(logits, axis=-1)
        p = p[:, :, 0] - lam * p[:, :, 1]
        return jnp.einsum('bhqk,bkhd->bqhd', p.astype(v.dtype), v)

    out = lax.map(block, jnp.arange(nb, dtype=jnp.int32))
    return jnp.moveaxis(out, 0, 1).reshape(b, s, h, -1)


def causal_depthwise_conv(u, w, bias):
    out = lax.conv_general_dilated(
        u, w[:, None, :].astype(u.dtype), window_strides=(1,), padding=[(SSM_CONV - 1, 0)],
        dimension_numbers=('NWC', 'WIO', 'NWC'), feature_group_count=u.shape[-1])
    return out + bias


def ssd_mixer(z, xbc, dt, conv_w, conv_b, dt_bias, a_log, d_skip, norm_w):
    f32 = jnp.float32
    b, s, _ = z.shape
    G, J, P, N, L = SSM_GROUPS, SSM_HEADS_PER_GROUP, SSM_HEAD_DIM, SSM_STATE, SSM_CHUNK
    nc = s // L
    xbc = jax.nn.silu(causal_depthwise_conv(xbc, conv_w, conv_b)).astype(f32)
    xs, bm, cm = jnp.split(xbc, [SSM_WIDTH, SSM_WIDTH + G * N], axis=-1)
    xs = xs.reshape(b, s, G, J, P)
    dt = jax.nn.softplus(dt.astype(f32) + dt_bias.astype(f32)).reshape(b, s, G, J)
    a = -jnp.exp(a_log.astype(f32)).reshape(G, J)

    xc = (xs * dt[..., None]).reshape(b, nc, L, G, J, P)
    bc = bm.reshape(b, nc, L, G, N)
    cc = cm.reshape(b, nc, L, G, N)
    a_cs = jnp.cumsum((dt * a).reshape(b, nc, L, G, J), axis=2)

    tril = jnp.tril(jnp.ones((L, L), dtype=bool))[:, :, None, None]
    seg = a_cs[:, :, :, None] - a_cs[:, :, None, :]
    decay = jnp.exp(jnp.where(tril, seg, -jnp.inf))
    cb = jnp.einsum('bclgn,bcsgn->bclsg', cc, bc)
    y_diag = jnp.einsum('bclsgj,bcsgjp->bclgjp', cb[..., None] * decay, xc)

    decay_states = jnp.exp(a_cs[:, :, -1:] - a_cs)
    states = jnp.einsum('bclgn,bclgjp->bcgjpn', bc, xc * decay_states[..., None])
    chunk_decay = jnp.exp(a_cs[:, :, -1])

    def step(state, inp):
        s_c, d_c = inp
        return state * d_c[..., None, None] + s_c, state

    init = jnp.zeros((b, G, J, P, N), f32)
    _, prev = lax.scan(step, init, (jnp.moveaxis(states, 1, 0), jnp.moveaxis(chunk_decay, 1, 0)))
    prev = jnp.moveaxis(prev, 0, 1)
    y_off = jnp.einsum('bclgn,bcgjpn->bclgjp', cc, prev) * jnp.exp(a_cs)[..., None]

    y = (y_diag + y_off).reshape(b, s, G, J, P) + d_skip.astype(f32).reshape(G, J)[:, :, None] * xs
    y = y.reshape(b, s, G, J * P) * jax.nn.silu(z.astype(f32)).reshape(b, s, G, J * P)
    y = y * lax.rsqrt(jnp.mean(y * y, axis=-1, keepdims=True) + NORM_EPS)
    y = y.reshape(b, s, SSM_WIDTH) * norm_w.astype(f32)
    return y.astype(z.dtype)


def peer_ffn(u, wq, sub_keys, down, up):
    b, s, d = u.shape
    tokens = u.reshape((b * s) // PEER_BLOCK, PEER_BLOCK, d)

    def block(xb):
        q = (xb @ wq).reshape(PEER_BLOCK, PEER_HEADS, 2, PEER_HALF)
        scores = jnp.einsum('thcd,hckd->thck', q, sub_keys).astype(jnp.float32)
        s1, i1 = lax.top_k(scores[:, :, 0], PEER_TOPK)
        s2, i2 = lax.top_k(scores[:, :, 1], PEER_TOPK)
        cand = (s1[..., :, None] + s2[..., None, :]).reshape(PEER_BLOCK, PEER_HEADS, PEER_TOPK ** 2)
        cand_idx = (i1[..., :, None] * PEER_NKEYS + i2[..., None, :]).reshape(
            PEER_BLOCK, PEER_HEADS, PEER_TOPK ** 2)
        top, pos = lax.top_k(cand, PEER_TOPK)
        idx = jnp.take_along_axis(cand_idx, pos, axis=-1)
        gate = jax.nn.softmax(top, axis=-1)
        pre = jnp.einsum('thkd,td->thk', down[idx], xb).astype(jnp.float32)
        act = jax.nn.gelu(pre, approximate=False) * gate
        return jnp.einsum('thk,thkd->td', act.astype(up.dtype), up[idx])

    return lax.map(block, tokens).reshape(b, s, d)


def setup_inputs(seed: int = 0) -> dict:
    key = jax.random.key(seed)
    ks = jax.random.split(key, 32)
    f32 = jnp.float32
    D = D_MODEL
    nrm = lambda k, shp, sc: jax.random.normal(k, shp, f32) * sc
    dt0 = jnp.exp(jax.random.uniform(ks[15], (DEPTH, SSM_HEADS), f32,
                                     math.log(1e-3), math.log(1e-1)))
    return {
        'x': nrm(ks[0], (BATCH, SEQ, D), 1.0),
        'c': nrm(ks[1], (BATCH, D), 1.0),
        'ada_w': nrm(ks[2], (DEPTH, D, 6 * D), D ** -0.5),
        'ada_b': nrm(ks[3], (DEPTH, 6 * D), 0.01),
        'norm1_w': 1.0 + nrm(ks[4], (DEPTH, D), 0.02),
        'w_in': nrm(ks[5], (DEPTH, D, IN_COLS), D ** -0.5),
        'q_norm_w': 1.0 + nrm(ks[6], (DEPTH, ATT_HEAD_DIM), 0.02),
        'k_norm_w': 1.0 + nrm(ks[7], (DEPTH, ATT_HEAD_DIM), 0.02),
        'rel_bias': nrm(ks[8], (REL_BUCKETS, ATT_HEADS, 2), 0.5),
        'lambda_q1': nrm(ks[9], (DEPTH, ATT_HEAD_DIM), 0.1),
        'lambda_k1': nrm(ks[10], (DEPTH, ATT_HEAD_DIM), 0.1),
        'lambda_q2': nrm(ks[11], (DEPTH, ATT_HEAD_DIM), 0.1),
        'lambda_k2': nrm(ks[12], (DEPTH, ATT_HEAD_DIM), 0.1),
        'subln_w': 1.0 + nrm(ks[13], (DEPTH, ATT_V_DIM), 0.02),
        'conv_w': nrm(ks[14], (DEPTH, SSM_CONV, SSM_CONV_DIM), SSM_CONV ** -0.5),
        'conv_b': nrm(ks[16], (DEPTH, SSM_CONV_DIM), 0.01),
        'dt_bias': dt0 + jnp.log(-jnp.expm1(-dt0)),
        'a_log': jnp.log(jax.random.uniform(ks[17], (DEPTH, SSM_HEADS), f32, 1.0, 16.0)),
        'd_skip': 1.0 + nrm(ks[18], (DEPTH, SSM_HEADS), 0.02),
        'ssm_norm_w': 1.0 + nrm(ks[19], (DEPTH, SSM_WIDTH), 0.02),
        'w_out': nrm(ks[20], (DEPTH, MIX_WIDTH, D), MIX_WIDTH ** -0.5),
        'norm2_w': 1.0 + nrm(ks[21], (DEPTH, D), 0.02),
        'peer_wq': nrm(ks[22], (DEPTH, D, PEER_HEADS * PEER_QDIM), D ** -0.5),
        'peer_keys': nrm(ks[23], (DEPTH, PEER_HEADS, 2, PEER_NKEYS, PEER_HALF), PEER_HALF ** -0.5),
        'expert_down': nrm(ks[24], (DEPTH, PEER_EXPERTS, D), D ** -0.5),
        'expert_up': nrm(ks[25], (DEPTH, PEER_EXPERTS, D), PEER_HEADS ** -0.5),
    }


def reference(x, c, ada_w, ada_b, norm1_w, w_in, q_norm_w, k_norm_w, rel_bias,
              lambda_q1, lambda_k1, lambda_q2, lambda_k2, subln_w, conv_w, conv_b,
              dt_bias, a_log, d_skip, ssm_norm_w, w_out, norm2_w, peer_wq, peer_keys,
              expert_down, expert_up):
    b, s, _ = x.shape
    f32 = jnp.float32
    h = x
    for l in range(DEPTH):
        lam_init = lambda_init_fn(l)
        mod = jax.nn.silu(c) @ ada_w[l] + ada_b[l]
        sh1, sc1, g1, sh2, sc2, g2 = jnp.split(mod[:, None, :], 6, axis=-1)

        hn = rms_norm(h, norm1_w[l]) * (1.0 + sc1) + sh1
        proj = hn @ w_in[l]
        q, k, v, z, xbc, dt = jnp.split(proj, IN_SPLITS, axis=-1)

        q = rms_norm(q.reshape(b, s, ATT_HEADS, 2, ATT_HEAD_DIM), q_norm_w[l])
        k = rms_norm(k.reshape(b, s, ATT_HEADS, 2, ATT_HEAD_DIM), k_norm_w[l])
        v = v.reshape(b, s, ATT_HEADS, ATT_V_DIM)
        lam = (jnp.exp(jnp.sum(lambda_q1[l].astype(f32) * lambda_k1[l].astype(f32)))
               - jnp.exp(jnp.sum(lambda_q2[l].astype(f32) * lambda_k2[l].astype(f32)))
               + lam_init)
        att = diff_attention(q, k, v, lam, rel_bias)
        att = (rms_norm(att, subln_w[l]) * (1.0 - lam_init)).reshape(b, s, ATT_WIDTH)

        ssm = ssd_mixer(z, xbc, dt, conv_w[l], conv_b[l], dt_bias[l], a_log[l],
                        d_skip[l], ssm_norm_w[l])

        mix = jnp.concatenate([att, ssm.astype(att.dtype)], axis=-1) @ w_out[l]
        h = h + g1 * mix

        hn2 = rms_norm(h, norm2_w[l]) * (1.0 + sc2) + sh2
        h = h + g2 * peer_ffn(hn2, peer_wq[l], peer_keys[l], expert_down[l], expert_up[l])
    return h
```

```python
import functools
import math

import jax
import jax.numpy as jnp
from jax import lax
from jax.experimental import pallas as pl
from jax.experimental.pallas import tpu as pltpu

F32 = jnp.float32
BF16 = jnp.bfloat16
I32 = jnp.int32

LANES = 128
VMEM_LIMIT = 56 * 1024 * 1024

NORM_EPS = 1e-6
HEAD_DIM = 64
ATT_HEADS = 8
ATT_WIDTH = 1024
SSM_WIDTH = 1024
SSM_HEADS = 16
SSM_GROUPS = 2
SSM_STATE = 128
SSM_CONV = 4
SSM_CHUNK = 128
SSM_BC = 2 * SSM_GROUPS * SSM_STATE
REL_BUCKETS = 32
REL_MAX_DIST = 128
PEER_HEADS = 8
PEER_NKEYS = 128
PEER_TOPK = 16
PEER_SLOTS = PEER_HEADS * PEER_TOPK
MAIN_COLS = 3 * ATT_WIDTH + SSM_WIDTH + SSM_WIDTH + SSM_BC
NEG = -1e30

NT_DIMS = (((1,), (1,)), ((), ()))


def _cparams(*sem):
    return pltpu.CompilerParams(dimension_semantics=sem, vmem_limit_bytes=VMEM_LIMIT)


def _sigmoid(x):
    return 1.0 / (1.0 + jnp.exp(-x))


def _ada_kernel(c_ref, w_ref, b_ref, o_ref):
    c = c_ref[...]
    sc = (c * _sigmoid(c)).astype(BF16)
    o_ref[...] = jnp.dot(sc, w_ref[...].astype(BF16), preferred_element_type=F32) + b_ref[...]


def _ada(c_pad, ada_w, ada_b):
    rows, d = c_pad.shape
    n = ada_w.shape[1]
    tn = 1536
    return pl.pallas_call(
        _ada_kernel,
        grid=(n // tn,),
        in_specs=[pl.BlockSpec((rows, d), lambda j: (0, 0)),
                  pl.BlockSpec((d, tn), lambda j: (0, j)),
                  pl.BlockSpec((1, tn), lambda j: (0, j))],
        out_specs=pl.BlockSpec((rows, tn), lambda j: (0, j)),
        out_shape=jax.ShapeDtypeStruct((rows, n), F32),
        compiler_params=_cparams("arbitrary"),
    )(c_pad, ada_w, ada_b)


def _group_rms(blk, w_row, lo):
    sq = blk * blk
    s_all = jnp.sum(sq, axis=1, keepdims=True)
    s_lo = jnp.sum(jnp.where(lo, sq, 0.0), axis=1, keepdims=True)
    s = jnp.where(lo, s_lo, s_all - s_lo)
    return blk * lax.rsqrt(s * (1.0 / HEAD_DIM) + NORM_EPS) * w_row


def _in_proj_kernel(x_ref, nw_ref, mod_ref, w_ref, wdt_ref, qn_ref, kn_ref,
                    o_ref, dt_ref, hn_scr, *, tn):
    j = pl.program_id(1)

    @pl.when(j == 0)
    def _():
        x = x_ref[...]
        y = x * lax.rsqrt(jnp.mean(x * x, axis=1, keepdims=True) + NORM_EPS) * nw_ref[...]
        mod = mod_ref[0]
        hn = (y * (1.0 + mod[1:2, :]) + mod[0:1, :]).astype(BF16)
        hn_scr[...] = hn
        dt_ref[...] = jnp.dot(hn, wdt_ref[...], preferred_element_type=F32)

    acc = jnp.dot(hn_scr[...], w_ref[...], preferred_element_type=F32)
    qk_tiles = 2 * ATT_WIDTH // tn

    @pl.when(j < qk_tiles)
    def _():
        lo = lax.broadcasted_iota(I32, (1, LANES), 1) < HEAD_DIM
        w_row = jnp.where(j < qk_tiles // 2, qn_ref[...], kn_ref[...])
        for cb in range(tn // LANES):
            blk = acc[:, cb * LANES:(cb + 1) * LANES]
            o_ref[:, cb * LANES:(cb + 1) * LANES] = _group_rms(blk, w_row, lo).astype(BF16)

    @pl.when(j >= qk_tiles)
    def _():
        o_ref[...] = acc.astype(BF16)


def _in_proj(x2, norm_w, mod3, w_main, w_dt, qn, kn, seq):
    n, d = x2.shape
    tm = min(1024, seq)
    tn = 512
    return pl.pallas_call(
        functools.partial(_in_proj_kernel, tn=tn),
        grid=(n // tm, MAIN_COLS // tn),
        in_specs=[pl.BlockSpec((tm, d), lambda i, j: (i, 0)),
                  pl.BlockSpec((1, d), lambda i, j: (0, 0)),
                  pl.BlockSpec((1, 6, d), lambda i, j: (i * tm // seq, 0, 0)),
                  pl.BlockSpec((d, tn), lambda i, j: (0, j)),
                  pl.BlockSpec((d, LANES), lambda i, j: (0, 0)),
                  pl.BlockSpec((1, LANES), lambda i, j: (0, 0)),
                  pl.BlockSpec((1, LANES), lambda i, j: (0, 0))],
        out_specs=[pl.BlockSpec((tm, tn), lambda i, j: (i, j)),
                   pl.BlockSpec((tm, LANES), lambda i, j: (i, 0))],
        out_shape=[jax.ShapeDtypeStruct((n, MAIN_COLS), BF16),
                   jax.ShapeDtypeStruct((n, LANES), F32)],
        scratch_shapes=[pltpu.VMEM((tm, d), BF16)],
        compiler_params=_cparams("arbitrary", "arbitrary"),
    )(x2, norm_w, mod3, w_main, w_dt, qn, kn)


def _attn_kernel(relb_ref, q_ref, k_ref, v_ref, lamv_ref, subw_ref, o_ref,
                 q2_scr, m_scr, l_scr, acc_scr, bias_scr, *, t, lam_init):
    h = pl.program_id(1)
    qi = pl.program_id(2)
    reps = t // LANES

    @pl.when(qi == 0)
    def _():
        r = lax.broadcasted_iota(I32, (t, t), 0)
        c = lax.broadcasted_iota(I32, (t, t), 1)
        max_exact = REL_BUCKETS // 2
        for tile, off in ((0, 0), (1, t)):
            rel = r - c + off
            nn = jnp.maximum(rel, 0)
            nf = jnp.maximum(nn, 1).astype(F32)
            large = max_exact + (jnp.log(nf / max_exact) / math.log(REL_MAX_DIST / max_exact)
                                 * (REL_BUCKETS - max_exact)).astype(I32)
            large = jnp.minimum(large, REL_BUCKETS - 1)
            bucket = jnp.where(nn < max_exact, nn, large)
            for m in range(2):
                far = relb_ref[(REL_BUCKETS - 1) * 2 * ATT_HEADS + h * 2 + m]
                val = jnp.zeros((t, t), F32)
                for b in range(REL_BUCKETS - 1):
                    val = jnp.where(bucket == b, relb_ref[b * 2 * ATT_HEADS + h * 2 + m] - far, val)
                bias_scr[tile, m * t:(m + 1) * t, :] = jnp.where(rel >= 0, val, NEG)

    q = q_ref[...]
    lo = lax.broadcasted_iota(I32, (t, LANES), 1) < HEAD_DIM
    zero = jnp.zeros_like(q)
    q2_scr[0:t, :] = jnp.where(lo, q, zero)
    q2_scr[t:2 * t, :] = jnp.where(lo, zero, q)
    m_scr[...] = jnp.full((2 * t, LANES), NEG, F32)
    l_scr[...] = jnp.zeros((2 * t, LANES), F32)
    acc_scr[...] = jnp.zeros((2 * t, LANES), F32)

    def chunk(start, bias):
        k_c = k_ref[pl.ds(start, t), :]
        v_c = v_ref[pl.ds(start, t), :]
        s = lax.dot_general(q2_scr[...], k_c, NT_DIMS, preferred_element_type=F32)
        if bias is not None:
            s = s + bias
        m_prev = m_scr[...]
        m_new = jnp.maximum(m_prev, jnp.max(s, axis=1, keepdims=True))
        alpha = jnp.exp(m_prev - m_new)
        p = jnp.exp(s - jnp.concatenate([m_new] * reps, axis=1))
        l_scr[...] = alpha * l_scr[...] + jnp.sum(p, axis=1, keepdims=True)
        acc_scr[...] = alpha * acc_scr[...] + jnp.dot(p.astype(BF16), v_c, preferred_element_type=F32)
        m_scr[...] = m_new

    def far_body(j, carry):
        chunk(pl.multiple_of(j * t, t), None)
        return carry

    lax.fori_loop(0, jnp.maximum(qi - 1, 0), far_body, 0)

    @pl.when(qi > 0)
    def _():
        chunk(pl.multiple_of((qi - 1) * t, t), bias_scr[1])

    chunk(pl.multiple_of(qi * t, t), bias_scr[0])

    lv = lamv_ref[...]
    lam = (jnp.exp(jnp.sum(lv[0:1, :] * lv[1:2, :], axis=1, keepdims=True))
           - jnp.exp(jnp.sum(lv[2:3, :] * lv[3:4, :], axis=1, keepdims=True)) + lam_init)
    o = acc_scr[0:t, :] / l_scr[0:t, :] - lam * (acc_scr[t:2 * t, :] / l_scr[t:2 * t, :])
    o = o * lax.rsqrt(jnp.mean(o * o, axis=1, keepdims=True) + NORM_EPS) * subw_ref[...]
    o_ref[...] = (o * (1.0 - lam_init)).astype(BF16)


def _attention(relb, proj, lamv, subw, batch, seq, lam_init):
    n = batch * seq
    t = min(256, seq)
    nq = seq // t
    return pl.pallas_call(
        functools.partial(_attn_kernel, t=t, lam_init=lam_init),
        grid=(batch, ATT_HEADS, nq),
        in_specs=[pl.BlockSpec(memory_space=pltpu.SMEM),
                  pl.BlockSpec((t, LANES), lambda b, h, i: (b * nq + i, h)),
                  pl.BlockSpec((seq, LANES), lambda b, h, i: (b, ATT_HEADS + h)),
                  pl.BlockSpec((seq, LANES), lambda b, h, i: (b, 2 * ATT_HEADS + h)),
                  pl.BlockSpec((8, LANES), lambda b, h, i: (0, 0)),
                  pl.BlockSpec((1, LANES), lambda b, h, i: (0, 0))],
        out_specs=pl.BlockSpec((t, LANES), lambda b, h, i: (b * nq + i, h)),
        out_shape=jax.ShapeDtypeStruct((n, ATT_WIDTH), BF16),
        scratch_shapes=[pltpu.VMEM((2 * t, LANES), BF16),
                        pltpu.VMEM((2 * t, LANES), F32),
                        pltpu.VMEM((2 * t, LANES), F32),
                        pltpu.VMEM((2 * t, LANES), F32),
                        pltpu.VMEM((2, 2 * t, t), F32)],
        compiler_params=_cparams("arbitrary", "arbitrary", "arbitrary"),
    )(relb, proj, proj, proj, lamv, subw)


def _split3(x):
    hi = x.astype(BF16)
    r1 = x - hi.astype(F32)
    mid = r1.astype(BF16)
    lo = (r1 - mid.astype(F32)).astype(BF16)
    return hi, mid, lo


def _ssd_kernel(z_ref, xs_ref, bc_ref, dt_ref, cw_ref, cb_ref, dtb_ref, alog_ref, dskip_ref, nw_ref,
                o_ref, xpad_scr, state_scr):
    L = SSM_CHUNK
    W = SSM_WIDTH
    P2 = LANES
    nblk = W // P2
    gw = W // SSM_GROUPS

    @pl.when(pl.program_id(1) == 0)
    def _():
        xpad_scr[0:8, :] = jnp.zeros((8, W + SSM_BC), F32)
        state_scr[...] = jnp.zeros((SSM_STATE, W), F32)

    xpad_scr[8:8 + L, 0:W] = xs_ref[...].astype(F32)
    xpad_scr[8:8 + L, W:W + SSM_BC] = bc_ref[...].astype(F32)
    conv = cb_ref[...] + cw_ref[0:1, :] * xpad_scr[5:5 + L, :]
    for kk in range(1, SSM_CONV):
        conv = conv + cw_ref[kk:kk + 1, :] * xpad_scr[5 + kk:5 + kk + L, :]
    xpad_scr[0:8, :] = xpad_scr[L:L + 8, :]
    u = conv * _sigmoid(conv)

    dtr = dt_ref[...] + dtb_ref[...]
    dt = jnp.maximum(dtr, 0.0) + jnp.log1p(jnp.exp(-jnp.abs(dtr)))
    a = -jnp.exp(alog_ref[...])
    da = dt * a

    ri = lax.broadcasted_iota(I32, (L, L), 0)
    ci = lax.broadcasted_iota(I32, (L, L), 1)
    tril = ri >= ci
    tri = jnp.where(tril, 1.0, 0.0).astype(BF16)
    hi, mid, lo3 = _split3(da)
    a_cs = (jnp.dot(tri, hi, preferred_element_type=F32) + jnp.dot(tri, mid, preferred_element_type=F32)
            + jnp.dot(tri, lo3, preferred_element_type=F32))
    a_cs_t = a_cs.T
    a_last = a_cs[L - 1:L, :]
    e_cs = jnp.exp(a_cs)
    dt_ds = dt * jnp.exp(a_last - a_cs)

    lane_lo = lax.broadcasted_iota(I32, (1, P2), 1) < HEAD_DIM

    def expand(mat, i):
        return jnp.where(lane_lo, mat[:, 2 * i:2 * i + 1], mat[:, 2 * i + 1:2 * i + 2])

    y_blocks = []
    for g in range(SSM_GROUPS):
        bm = u[:, W + g * SSM_STATE:W + (g + 1) * SSM_STATE]
        cm = u[:, W + (SSM_GROUPS + g) * SSM_STATE:W + (SSM_GROUPS + g + 1) * SSM_STATE]
        bm16 = bm.astype(BF16)
        cm16 = cm.astype(BF16)
        cb = lax.dot_general(cm16, bm16, NT_DIMS, preferred_element_type=F32)
        st_g = state_scr[:, g * gw:(g + 1) * gw]
        y_off = jnp.dot(cm16, st_g.astype(BF16), preferred_element_type=F32)
        xd_blocks = []
        for ib in range(nblk // SSM_GROUPS):
            i = g * (nblk // SSM_GROUPS) + ib
            xs_blk = u[:, i * P2:(i + 1) * P2]
            xc = xs_blk * expand(dt, i)
            yd = jnp.zeros((L, P2), F32)
            for hh in range(2):
                head = 2 * i + hh
                seg = a_cs[:, head:head + 1] - a_cs_t[head:head + 1, :]
                wmat = (cb * jnp.where(tril, jnp.exp(seg), 0.0)).astype(BF16)
                keep = lane_lo if hh == 0 else jnp.logical_not(lane_lo)
                yd = yd + jnp.dot(wmat, jnp.where(keep, xc, 0.0).astype(BF16), preferred_element_type=F32)
            y = yd + y_off[:, ib * P2:(ib + 1) * P2] * expand(e_cs, i) + expand(dskip_ref[...], i) * xs_blk
            zf = z_ref[:, i * P2:(i + 1) * P2].astype(F32)
            y_blocks.append(y * (zf * _sigmoid(zf)))
            xd_blocks.append((xs_blk * expand(dt_ds, i)).astype(BF16))
        xd = jnp.concatenate(xd_blocks, axis=1)
        st_new = jnp.dot(bm.T.astype(BF16), xd, preferred_element_type=F32)
        decay = jnp.concatenate([expand(jnp.exp(a_last), g * (nblk // SSM_GROUPS) + ib)
                                 for ib in range(nblk // SSM_GROUPS)], axis=1)
        state_scr[:, g * gw:(g + 1) * gw] = st_g * decay + st_new

    per_g = nblk // SSM_GROUPS
    for g in range(SSM_GROUPS):
        blks = y_blocks[g * per_g:(g + 1) * per_g]
        ss = sum(jnp.sum(b * b, axis=1, keepdims=True) for b in blks) * (1.0 / gw)
        inv = lax.rsqrt(ss + NORM_EPS)
        for ib, b in enumerate(blks):
            i = g * per_g + ib
            o_ref[:, i * P2:(i + 1) * P2] = (b * inv * nw_ref[:, i * P2:(i + 1) * P2]).astype(BF16)


def _ssd(proj, dt_raw, conv_w, conv_b, dt_bias, a_log, d_skip, norm_w, batch, seq):
    n = batch * seq
    L = SSM_CHUNK
    nc = seq // L
    cd = SSM_WIDTH + SSM_BC
    row = lambda b, c: (b * nc + c)
    return pl.pallas_call(
        _ssd_kernel,
        grid=(batch, nc),
        in_specs=[pl.BlockSpec((L, SSM_WIDTH), lambda b, c: (row(b, c), 3)),
                  pl.BlockSpec((L, SSM_WIDTH), lambda b, c: (row(b, c), 4)),
                  pl.BlockSpec((L, SSM_BC), lambda b, c: (row(b, c), 10)),
                  pl.BlockSpec((L, LANES), lambda b, c: (row(b, c), 0)),
                  pl.BlockSpec((SSM_CONV, cd), lambda b, c: (0, 0)),
                  pl.BlockSpec((1, cd), lambda b, c: (0, 0)),
                  pl.BlockSpec((1, LANES), lambda b, c: (0, 0)),
                  pl.BlockSpec((1, LANES), lambda b, c: (0, 0)),
                  pl.BlockSpec((1, LANES), lambda b, c: (0, 0)),
                  pl.BlockSpec((1, SSM_WIDTH), lambda b, c: (0, 0))],
        out_specs=pl.BlockSpec((L, SSM_WIDTH), lambda b, c: (row(b, c), 0)),
        out_shape=jax.ShapeDtypeStruct((n, SSM_WIDTH), BF16),
        scratch_shapes=[pltpu.VMEM((L + 8, cd), F32),
                        pltpu.VMEM((SSM_STATE, SSM_WIDTH), F32)],
        compiler_params=_cparams("arbitrary", "arbitrary"),
    )(proj, proj, proj, dt_raw, conv_w, conv_b, dt_bias, a_log, d_skip, norm_w)


def _out_proj_kernel(x_ref, att_ref, ssm_ref, w_ref, mod_ref, nw_ref, h_ref, hn_ref):
    mix = (jnp.dot(att_ref[...], w_ref[0:ATT_WIDTH, :], preferred_element_type=F32)
           + jnp.dot(ssm_ref[...], w_ref[ATT_WIDTH:, :], preferred_element_type=F32))
    mod = mod_ref[0]
    h1 = x_ref[...] + mod[2:3, :] * mix
    h_ref[...] = h1
    y = h1 * lax.rsqrt(jnp.mean(h1 * h1, axis=1, keepdims=True) + NORM_EPS) * nw_ref[...]
    hn_ref[...] = (y * (1.0 + mod[4:5, :]) + mod[3:4, :]).astype(BF16)


def _out_proj(x2, att, ssm, w_out, mod3, norm2_w, seq):
    n, d = x2.shape
    tm = min(256, seq)
    return pl.pallas_call(
        _out_proj_kernel,
        grid=(n // tm,),
        in_specs=[pl.BlockSpec((tm, d), lambda i: (i, 0)),
                  pl.BlockSpec((tm, ATT_WIDTH), lambda i: (i, 0)),
                  pl.BlockSpec((tm, SSM_WIDTH), lambda i: (i, 0)),
                  pl.BlockSpec((ATT_WIDTH + SSM_WIDTH, d), lambda i: (0, 0)),
                  pl.BlockSpec((1, 6, d), lambda i: (i * tm // seq, 0, 0)),
                  pl.BlockSpec((1, d), lambda i: (0, 0))],
        out_specs=[pl.BlockSpec((tm, d), lambda i: (i, 0)),
                   pl.BlockSpec((tm, d), lambda i: (i, 0))],
        out_shape=[jax.ShapeDtypeStruct((n, d), F32),
                   jax.ShapeDtypeStruct((n, d), BF16)],
        compiler_params=_cparams("arbitrary"),
    )(x2, att, ssm, w_out, mod3, norm2_w)


def _topk_rows(s, k, rows):
    row = lax.broadcasted_iota(I32, s.shape, 0)
    vals, idxs = [], []
    for _ in range(k):
        m = jnp.max(s, axis=0, keepdims=True)
        idx = jnp.min(jnp.where(s == m, row, rows), axis=0, keepdims=True)
        s = jnp.where(row == idx, -jnp.inf, s)
        vals.append(m)
        idxs.append(idx)
    return vals, idxs


_CAND = [(i, j) for i in range(PEER_TOPK) for j in range(PEER_TOPK) if (i + 1) * (j + 1) <= PEER_TOPK]


def _route_kernel(hn_ref, wq_ref, keys_ref, a_ref, b_ref, g_ref, top_scr, aa_scr, bb_scr):
    t = hn_ref.shape[0]
    qp = jnp.dot(hn_ref[...], wq_ref[...], preferred_element_type=F32).astype(BF16)
    ncand = len(_CAND)
    pad = (-ncand) % 8
    for h in range(PEER_HEADS):
        sub = []
        for c in range(2):
            hc = 2 * h + c
            sc = lax.dot_general(keys_ref[hc], qp[:, hc * LANES:(hc + 1) * LANES], NT_DIMS,
                                 preferred_element_type=F32)
            sub.append(_topk_rows(sc, PEER_TOPK, PEER_NKEYS))
        (s1, i1), (s2, i2) = sub
        cand = jnp.concatenate([s1[i] + s2[j] for i, j in _CAND]
                               + [jnp.full((pad, t), -jnp.inf, F32)], axis=0)
        cand_a = jnp.concatenate([i1[i] for i, _ in _CAND] + [jnp.zeros((pad, t), I32)], axis=0)
        cand_b = jnp.concatenate([i2[j] for _, j in _CAND] + [jnp.zeros((pad, t), I32)], axis=0)
        row = lax.broadcasted_iota(I32, cand.shape, 0)
        for kk in range(PEER_TOPK):
            m = jnp.max(cand, axis=0, keepdims=True)
            idx = jnp.min(jnp.where(cand == m, row, ncand + pad), axis=0, keepdims=True)
            sel = row == idx
            slot = h * PEER_TOPK + kk
            top_scr[slot:slot + 1, :] = m
            aa_scr[slot:slot + 1, :] = jnp.max(jnp.where(sel, cand_a, -1), axis=0, keepdims=True)
            bb_scr[slot:slot + 1, :] = jnp.max(jnp.where(sel, cand_b, -1), axis=0, keepdims=True)
            cand = jnp.where(sel, -jnp.inf, cand)
        top = top_scr[h * PEER_TOPK:(h + 1) * PEER_TOPK, :]
        e = jnp.exp(top - jnp.max(top, axis=0, keepdims=True))
        top_scr[h * PEER_TOPK:(h + 1) * PEER_TOPK, :] = e / jnp.sum(e, axis=0, keepdims=True)
    a_ref[...] = aa_scr[...].T
    b_ref[...] = bb_scr[...].T
    g_ref[...] = top_scr[...].T


def _route(hn2, wq, keys):
    n, d = hn2.shape
    t = min(256, n)
    qd = wq.shape[1]
    out = jax.ShapeDtypeStruct((n, PEER_SLOTS), I32)
    return pl.pallas_call(
        _route_kernel,
        grid=(n // t,),
        in_specs=[pl.BlockSpec((t, d), lambda i: (i, 0)),
                  pl.BlockSpec((d, qd), lambda i: (0, 0)),
                  pl.BlockSpec(keys.shape, lambda i: (0, 0, 0))],
        out_specs=[pl.BlockSpec((t, PEER_SLOTS), lambda i: (i, 0))] * 3,
        out_shape=[out, out, jax.ShapeDtypeStruct((n, PEER_SLOTS), F32)],
        scratch_shapes=[pltpu.VMEM((PEER_SLOTS, t), F32),
                        pltpu.VMEM((PEER_SLOTS, t), I32),
                        pltpu.VMEM((PEER_SLOTS, t), I32)],
        compiler_params=_cparams("arbitrary"),
    )(hn2, wq, keys)


PAIR = 2 * PEER_NKEYS


def _peer_down_kernel(x_ref, dn_ref, a_ref, b_ref, pre_ref):
    j = pl.program_id(1)

    @pl.when(j == 0)
    def _():
        pre_ref[...] = jnp.zeros(pre_ref.shape, F32)

    p = lax.dot_general(x_ref[...], dn_ref[...], NT_DIMS, preferred_element_type=F32)
    a = a_ref[...]
    b = b_ref[...]
    pre = pre_ref[...]
    for half in range(2):
        g = jnp.take_along_axis(p[:, half * LANES:(half + 1) * LANES], b, axis=1)
        pre = jnp.where(a == 2 * j + half, g, pre)
    pre_ref[...] = pre


def _peer_down(hn2, down16, aidx, bidx):
    n, d = hn2.shape
    t = min(1024, n)
    slot_spec = pl.BlockSpec((t, PEER_SLOTS), lambda i, j: (i, 0))
    return pl.pallas_call(
        _peer_down_kernel,
        grid=(n // t, down16.shape[0] // PAIR),
        in_specs=[pl.BlockSpec((t, d), lambda i, j: (i, 0)),
                  pl.BlockSpec((PAIR, d), lambda i, j: (j, 0)),
                  slot_spec, slot_spec],
        out_specs=slot_spec,
        out_shape=jax.ShapeDtypeStruct((n, PEER_SLOTS), F32),
        compiler_params=_cparams("arbitrary", "arbitrary"),
    )(hn2, down16, aidx, bidx)


def _peer_up_kernel(pre_ref, g_ref, a_ref, b_ref, up_ref, h_ref, mod_ref, o_ref,
                    act_scr, dense_scr, acc_scr):
    j = pl.program_id(1)
    t = pre_ref.shape[0]
    nk = PEER_NKEYS

    @pl.when(j == 0)
    def _():
        pre = pre_ref[...]
        act_scr[...] = 0.5 * pre * (1.0 + lax.erf(pre * (1.0 / math.sqrt(2.0)))) * g_ref[...]
        acc_scr[...] = jnp.zeros(acc_scr.shape, F32)
        row = lax.broadcasted_iota(I32, (nk, PEER_SLOTS), 0)

        def body(tok, carry):
            a_row = a_ref[pl.ds(tok, 1), :]
            b_row = b_ref[pl.ds(tok, 1), :]
            c_row = act_scr[pl.ds(tok, 1), :]
            xa = jnp.where(row == a_row, c_row, 0.0).astype(BF16)
            yb = jnp.where(row == b_row, 1.0, 0.0).astype(BF16)
            dense_scr[pl.ds(pl.multiple_of(tok * nk, nk), nk), :] = lax.dot_general(
                xa, yb, NT_DIMS, preferred_element_type=F32)
            return carry

        lax.fori_loop(0, t, body, 0)

    lhs = jnp.concatenate([dense_scr[pl.ds(2 * j, t, stride=nk), :],
                           dense_scr[pl.ds(2 * j + 1, t, stride=nk), :]], axis=1).astype(BF16)
    acc_scr[...] += jnp.dot(lhs, up_ref[...], preferred_element_type=F32)

    @pl.when(j == pl.num_programs(1) - 1)
    def _():
        o_ref[...] = h_ref[...] + mod_ref[0][5:6, :] * acc_scr[...]


def _peer_up(pre, gate, aidx, bidx, up16, h1, mod3, seq):
    n, d = h1.shape
    t = min(256, seq)
    slot_spec = pl.BlockSpec((t, PEER_SLOTS), lambda i, j: (i, 0))
    return pl.pallas_call(
        _peer_up_kernel,
        grid=(n // t, up16.shape[0] // PAIR),
        in_specs=[slot_spec, slot_spec, slot_spec, slot_spec,
                  pl.BlockSpec((PAIR, d), lambda i, j: (j, 0)),
                  pl.BlockSpec((t, d), lambda i, j: (i, 0)),
                  pl.BlockSpec((1, 6, d), lambda i, j: (i * t // seq, 0, 0))],
        out_specs=pl.BlockSpec((t, d), lambda i, j: (i, 0)),
        out_shape=jax.ShapeDtypeStruct((n, d), F32),
        scratch_shapes=[pltpu.VMEM((t, PEER_SLOTS), F32),
                        pltpu.VMEM((t * PEER_NKEYS, PEER_NKEYS), F32),
                        pltpu.VMEM((t, d), F32)],
        compiler_params=_cparams("arbitrary", "arbitrary"),
    )(pre, gate, aidx, bidx, up16, h1, mod3)


def _pad_lanes(v):
    return jnp.pad(v.astype(F32), (0, LANES - v.shape[0])).reshape(1, LANES)


def _layer(h2, mod3, l, batch, seq, norm1_w, w_in, q_norm_w, k_norm_w, rel_bias, lambda_q1, lambda_k1,
           lambda_q2, lambda_k2, subln_w, conv_w, conv_b, dt_bias, a_log, d_skip, ssm_norm_w, w_out,
           norm2_w, peer_wq, peer_keys, expert_down, expert_up):
    d = h2.shape[1]
    lam_init = 0.8 - 0.6 * math.exp(-0.3 * l)
    w16 = w_in.astype(BF16)
    w_main = w16[:, :MAIN_COLS]
    w_dt = jnp.pad(w16[:, MAIN_COLS:], ((0, 0), (0, LANES - SSM_HEADS)))
    qn = jnp.tile(q_norm_w.astype(F32) * HEAD_DIM ** -0.5, 2).reshape(1, LANES)
    kn = jnp.tile(k_norm_w.astype(F32), 2).reshape(1, LANES)
    proj, dt_raw = _in_proj(h2, norm1_w.reshape(1, d), mod3, w_main, w_dt, qn, kn, seq)

    lamv = jnp.pad(jnp.stack([lambda_q1, lambda_k1, lambda_q2, lambda_k2]).astype(F32),
                   ((0, 4), (0, LANES - HEAD_DIM)))
    att = _attention(rel_bias.astype(F32).reshape(-1), proj, lamv, subln_w.reshape(1, LANES),
                     batch, seq, lam_init)
    ssm = _ssd(proj, dt_raw, conv_w, conv_b.reshape(1, -1), _pad_lanes(dt_bias), _pad_lanes(a_log),
               _pad_lanes(d_skip), ssm_norm_w.reshape(1, -1), batch, seq)
    h1, hn2 = _out_proj(h2, att, ssm, w_out.astype(BF16), mod3, norm2_w.reshape(1, d), seq)

    keys = peer_keys.astype(BF16).reshape(2 * PEER_HEADS, PEER_NKEYS, -1)
    aidx, bidx, gate = _route(hn2, peer_wq.astype(BF16), keys)
    pre = _peer_down(hn2, expert_down.astype(BF16), aidx, bidx)
    return _peer_up(pre, gate, aidx, bidx, expert_up.astype(BF16), h1, mod3, seq)


def kernel(x, c, ada_w, ada_b, norm1_w, w_in, q_norm_w, k_norm_w, rel_bias, lambda_q1, lambda_k1, lambda_q2, lambda_k2, subln_w, conv_w, conv_b, dt_bias, a_log, d_skip, ssm_norm_w, w_out, norm2_w, peer_wq, peer_keys, expert_down, expert_up):
    batch, seq, d = x.shape
    depth = ada_w.shape[0]
    h2 = x.reshape(batch * seq, d)
    c_pad = jnp.pad(c, ((0, 8 - batch), (0, 0)))
    for l in range(depth):
        mod = _ada(c_pad, ada_w[l], ada_b[l].reshape(1, -1))
        mod3 = mod[:batch].reshape(batch, 6, d)
        h2 = _layer(h2, mod3, l, batch, seq, norm1_w[l], w_in[l], q_norm_w[l], k_norm_w[l], rel_bias,
                    lambda_q1[l], lambda_k1[l], lambda_q2[l], lambda_k2[l], subln_w[l], conv_w[l],
                    conv_b[l], dt_bias[l], a_log[l], d_skip[l], ssm_norm_w[l], w_out[l], norm2_w[l],
                    peer_wq[l], peer_keys[l], expert_down[l], expert_up[l])
    return h2.reshape(batch, seq, d)
```

```python
import functools
import math

import jax
import jax.numpy as jnp
from jax import lax
from jax.experimental import pallas as pl
from jax.experimental.pallas import tpu as pltpu

F32 = jnp.float32
BF16 = jnp.bfloat16
I32 = jnp.int32

LANES = 128
VMEM_LIMIT = 56 * 1024 * 1024

NORM_EPS = 1e-6
HEAD_DIM = 64
ATT_HEADS = 8
ATT_WIDTH = 1024
SSM_WIDTH = 1024
SSM_HEADS = 16
SSM_GROUPS = 2
SSM_STATE = 128
SSM_CONV = 4
SSM_CHUNK = 128
SSM_BC = 2 * SSM_GROUPS * SSM_STATE
REL_BUCKETS = 32
REL_MAX_DIST = 128
PEER_HEADS = 8
PEER_NKEYS = 128
PEER_TOPK = 16
PEER_SLOTS = PEER_HEADS * PEER_TOPK
MAIN_COLS = 3 * ATT_WIDTH + SSM_WIDTH + SSM_WIDTH + SSM_BC
NEG = -1e30

NT_DIMS = (((1,), (1,)), ((), ()))


def _cparams(*sem):
    return pltpu.CompilerParams(dimension_semantics=sem, vmem_limit_bytes=VMEM_LIMIT)


def _sigmoid(x):
    return 1.0 / (1.0 + jnp.exp(-x))


def _ada_kernel(c_ref, w_ref, b_ref, o_ref):
    c = c_ref[...]
    sc = (c * _sigmoid(c)).astype(BF16)
    o_ref[...] = jnp.dot(sc, w_ref[...].astype(BF16), preferred_element_type=F32) + b_ref[...]


def _ada(c_pad, ada_w, ada_b):
    rows, d = c_pad.shape
    n = ada_w.shape[1]
    tn = 1536
    return pl.pallas_call(
        _ada_kernel,
        grid=(n // tn,),
        in_specs=[pl.BlockSpec((rows, d), lambda j: (0, 0)),
                  pl.BlockSpec((d, tn), lambda j: (0, j)),
                  pl.BlockSpec((1, tn), lambda j: (0, j))],
        out_specs=pl.BlockSpec((rows, tn), lambda j: (0, j)),
        out_shape=jax.ShapeDtypeStruct((rows, n), F32),
        compiler_params=_cparams("arbitrary"),
    )(c_pad, ada_w, ada_b)


def _group_rms(blk, w_row, lo):
    sq = blk * blk
    s_all = jnp.sum(sq, axis=1, keepdims=True)
    s_lo = jnp.sum(jnp.where(lo, sq, 0.0), axis=1, keepdims=True)
    s = jnp.where(lo, s_lo, s_all - s_lo)
    return blk * lax.rsqrt(s * (1.0 / HEAD_DIM) + NORM_EPS) * w_row


def _in_proj_kernel(x_ref, nw_ref, mod_ref, w_ref, wdt_ref, qn_ref, kn_ref,
                    o_ref, dt_ref, hn_scr, *, tn):
    j = pl.program_id(1)

    @pl.when(j == 0)
    def _():
        x = x_ref[...]
        y = x * lax.rsqrt(jnp.mean(x * x, axis=1, keepdims=True) + NORM_EPS) * nw_ref[...]
        mod = mod_ref[0]
        hn = (y * (1.0 + mod[1:2, :]) + mod[0:1, :]).astype(BF16)
        hn_scr[...] = hn
        dt_ref[...] = jnp.dot(hn, wdt_ref[...], preferred_element_type=F32)

    acc = jnp.dot(hn_scr[...], w_ref[...], preferred_element_type=F32)
    qk_tiles = 2 * ATT_WIDTH // tn

    @pl.when(j < qk_tiles)
    def _():
        lo = lax.broadcasted_iota(I32, (1, LANES), 1) < HEAD_DIM
        w_row = jnp.where(j < qk_tiles // 2, qn_ref[...], kn_ref[...])
        for cb in range(tn // LANES):
            blk = acc[:, cb * LANES:(cb + 1) * LANES]
            o_ref[:, cb * LANES:(cb + 1) * LANES] = _group_rms(blk, w_row, lo).astype(BF16)

    @pl.when(j >= qk_tiles)
    def _():
        o_ref[...] = acc.astype(BF16)


def _in_proj(x2, norm_w, mod3, w_main, w_dt, qn, kn, seq):
    n, d = x2.shape
    tm = min(1024, seq)
    tn = 512
    return pl.pallas_call(
        functools.partial(_in_proj_kernel, tn=tn),
        grid=(n // tm, MAIN_COLS // tn),
        in_specs=[pl.BlockSpec((tm, d), lambda i, j: (i, 0)),
                  pl.BlockSpec((1, d), lambda i, j: (0, 0)),
                  pl.BlockSpec((1, 6, d), lambda i, j: (i * tm // seq, 0, 0)),
                  pl.BlockSpec((d, tn), lambda i, j: (0, j)),
                  pl.BlockSpec((d, LANES), lambda i, j: (0, 0)),
                  pl.BlockSpec((1, LANES), lambda i, j: (0, 0)),
                  pl.BlockSpec((1, LANES), lambda i, j: (0, 0))],
        out_specs=[pl.BlockSpec((tm, tn), lambda i, j: (i, j)),
                   pl.BlockSpec((tm, LANES), lambda i, j: (i, 0))],
        out_shape=[jax.ShapeDtypeStruct((n, MAIN_COLS), BF16),
                   jax.ShapeDtypeStruct((n, LANES), F32)],
        scratch_shapes=[pltpu.VMEM((tm, d), BF16)],
        compiler_params=_cparams("arbitrary", "arbitrary"),
    )(x2, norm_w, mod3, w_main, w_dt, qn, kn)


ATT_GROUP = 2


def _attn_kernel(relb_ref, q_ref, k_ref, vt_ref, lamv_ref, subw_ref, o_ref,
                 q2t_scr, acc_scr, bias_scr, *, t, lam_init):
    hp = pl.program_id(1)
    qi = pl.program_id(2)
    nchunk = vt_ref.shape[0] // ATT_GROUP

    @pl.when(qi == 0)
    def _():
        kk = lax.broadcasted_iota(I32, (t, t), 0)
        qq = lax.broadcasted_iota(I32, (t, t), 1)
        max_exact = REL_BUCKETS // 2
        for tile, off in ((0, 0), (1, t)):
            rel = qq - kk + off
            nn = jnp.maximum(rel, 0)
            nf = jnp.maximum(nn, 1).astype(F32)
            large = max_exact + (jnp.log(nf / max_exact) / math.log(REL_MAX_DIST / max_exact)
                                 * (REL_BUCKETS - max_exact)).astype(I32)
            large = jnp.minimum(large, REL_BUCKETS - 1)
            bucket = jnp.where(nn < max_exact, nn, large)
            for hh in range(ATT_GROUP):
                head = hp * ATT_GROUP + hh
                for m in range(2):
                    far = relb_ref[(REL_BUCKETS - 1) * 2 * ATT_HEADS + head * 2 + m]
                    val = jnp.zeros((t, t), F32)
                    for b in range(REL_BUCKETS - 1):
                        val = jnp.where(bucket == b, relb_ref[b * 2 * ATT_HEADS + head * 2 + m] - far, val)
                    bias_scr[hh, tile, :, m * t:(m + 1) * t] = jnp.where(rel >= 0, val, NEG)

    d_lo = lax.broadcasted_iota(I32, (LANES, t), 0) < HEAD_DIM
    for hh in range(ATT_GROUP):
        qt = q_ref[:, hh * LANES:(hh + 1) * LANES].astype(F32).T
        q2t_scr[hh, :, 0:t] = jnp.where(d_lo, qt, 0.0).astype(BF16)
        q2t_scr[hh, :, t:2 * t] = jnp.where(d_lo, 0.0, qt).astype(BF16)
        acc_scr[hh] = jnp.zeros((LANES, 2 * t), F32)

    def step(c, carry, tile):
        start = pl.multiple_of(c * t, t)
        out = []
        for hh in range(ATT_GROUP):
            m_prev, l_prev = carry[hh]
            k_c = k_ref[pl.ds(start, t), hh * LANES:(hh + 1) * LANES]
            s = jnp.dot(k_c, q2t_scr[hh], preferred_element_type=F32)
            if tile is not None:
                s = s + bias_scr[hh, tile]
            m_new = jnp.maximum(m_prev, jnp.max(s, axis=0, keepdims=True))
            alpha = jnp.exp(m_prev - m_new)
            p = jnp.exp(s - m_new)
            l_new = alpha * l_prev + jnp.sum(p, axis=0, keepdims=True)
            pv = jnp.dot(vt_ref[hh * nchunk + c], p.astype(BF16), preferred_element_type=F32)
            acc_scr[hh] = alpha * acc_scr[hh] + pv
            out.append((m_new, l_new))
        return tuple(out)

    init = tuple((jnp.full((1, 2 * t), NEG, F32), jnp.zeros((1, 2 * t), F32)) for _ in range(ATT_GROUP))
    carry = lax.fori_loop(0, jnp.maximum(qi - 1, 0), lambda c, cr: step(c, cr, None), init)
    carry = lax.cond(qi > 0, lambda cr: step(qi - 1, cr, 1), lambda cr: cr, carry)
    carry = step(qi, carry, 0)

    lv = lamv_ref[...]
    lam = (jnp.exp(jnp.sum(lv[0:1, :] * lv[1:2, :], axis=1, keepdims=True))
           - jnp.exp(jnp.sum(lv[2:3, :] * lv[3:4, :], axis=1, keepdims=True)) + lam_init)
    for hh in range(ATT_GROUP):
        inv_l = 1.0 / carry[hh][1]
        acc = acc_scr[hh] * inv_l
        o = (acc[:, 0:t] - lam * acc[:, t:2 * t]).T
        o = o * lax.rsqrt(jnp.mean(o * o, axis=1, keepdims=True) + NORM_EPS) * subw_ref[...]
        o_ref[:, hh * LANES:(hh + 1) * LANES] = (o * (1.0 - lam_init)).astype(BF16)


def _attention(relb, proj, lamv, subw, batch, seq, lam_init):
    n = batch * seq
    t = min(256, seq)
    nq = seq // t
    gw = ATT_GROUP * LANES
    ngroups = ATT_HEADS // ATT_GROUP
    vt = proj[:, 2 * ATT_WIDTH:3 * ATT_WIDTH].reshape(batch, nq, t, ATT_HEADS, LANES)
    vt = vt.transpose(0, 3, 1, 4, 2).reshape(batch * ATT_HEADS * nq, LANES, t)
    return pl.pallas_call(
        functools.partial(_attn_kernel, t=t, lam_init=lam_init),
        grid=(batch, ngroups, nq),
        in_specs=[pl.BlockSpec(memory_space=pltpu.SMEM),
                  pl.BlockSpec((t, gw), lambda b, g, i: (b * nq + i, g)),
                  pl.BlockSpec((seq, gw), lambda b, g, i: (b, ngroups + g)),
                  pl.BlockSpec((ATT_GROUP * nq, LANES, t), lambda b, g, i: (b * ngroups + g, 0, 0)),
                  pl.BlockSpec((8, LANES), lambda b, g, i: (0, 0)),
                  pl.BlockSpec((1, LANES), lambda b, g, i: (0, 0))],
        out_specs=pl.BlockSpec((t, gw), lambda b, g, i: (b * nq + i, g)),
        out_shape=jax.ShapeDtypeStruct((n, ATT_WIDTH), BF16),
        scratch_shapes=[pltpu.VMEM((ATT_GROUP, LANES, 2 * t), BF16),
                        pltpu.VMEM((ATT_GROUP, LANES, 2 * t), F32),
                        pltpu.VMEM((ATT_GROUP, 2, t, 2 * t), F32)],
        compiler_params=_cparams("arbitrary", "arbitrary", "arbitrary"),
    )(relb, proj, proj, vt, lamv, subw)


def _split3(x):
    hi = x.astype(BF16)
    r1 = x - hi.astype(F32)
    mid = r1.astype(BF16)
    lo = (r1 - mid.astype(F32)).astype(BF16)
    return hi, mid, lo


def _ssd_kernel(z_ref, xs_ref, bc_ref, dt_ref, cw_ref, cb_ref, dtb_ref, alog_ref, dskip_ref, nw_ref,
                o_ref, xpad_scr, state_scr):
    L = SSM_CHUNK
    W = SSM_WIDTH
    P2 = LANES
    nblk = W // P2
    gw = W // SSM_GROUPS

    @pl.when(pl.program_id(1) == 0)
    def _():
        xpad_scr[0:8, :] = jnp.zeros((8, W + SSM_BC), F32)
        state_scr[...] = jnp.zeros((SSM_STATE, W), F32)

    xpad_scr[8:8 + L, 0:W] = xs_ref[...].astype(F32)
    xpad_scr[8:8 + L, W:W + SSM_BC] = bc_ref[...].astype(F32)
    conv = cb_ref[...] + cw_ref[0:1, :] * xpad_scr[5:5 + L, :]
    for kk in range(1, SSM_CONV):
        conv = conv + cw_ref[kk:kk + 1, :] * xpad_scr[5 + kk:5 + kk + L, :]
    xpad_scr[0:8, :] = xpad_scr[L:L + 8, :]
    u = conv * _sigmoid(conv)

    dtr = dt_ref[...] + dtb_ref[...]
    dt = jnp.maximum(dtr, 0.0) + jnp.log1p(jnp.exp(-jnp.abs(dtr)))
    a = -jnp.exp(alog_ref[...])
    da = dt * a

    ri = lax.broadcasted_iota(I32, (L, L), 0)
    ci = lax.broadcasted_iota(I32, (L, L), 1)
    tril = ri >= ci
    tri = jnp.where(tril, 1.0, 0.0).astype(BF16)
    hi, mid, lo3 = _split3(da)
    a_cs = (jnp.dot(tri, hi, preferred_element_type=F32) + jnp.dot(tri, mid, preferred_element_type=F32)
            + jnp.dot(tri, lo3, preferred_element_type=F32))
    a_cs_t = a_cs.T
    a_last = a_cs[L - 1:L, :]
    e_cs = jnp.exp(a_cs)
    dt_ds = dt * jnp.exp(a_last - a_cs)

    lane_lo = lax.broadcasted_iota(I32, (1, P2), 1) < HEAD_DIM

    def expand(mat, i):
        return jnp.where(lane_lo, mat[:, 2 * i:2 * i + 1], mat[:, 2 * i + 1:2 * i + 2])

    y_blocks = []
    for g in range(SSM_GROUPS):
        bm = u[:, W + g * SSM_STATE:W + (g + 1) * SSM_STATE]
        cm = u[:, W + (SSM_GROUPS + g) * SSM_STATE:W + (SSM_GROUPS + g + 1) * SSM_STATE]
        bm16 = bm.astype(BF16)
        cm16 = cm.astype(BF16)
        cb = lax.dot_general(cm16, bm16, NT_DIMS, preferred_element_type=F32)
        st_g = state_scr[:, g * gw:(g + 1) * gw]
        y_off = jnp.dot(cm16, st_g.astype(BF16), preferred_element_type=F32)
        xd_blocks = []
        for ib in range(nblk // SSM_GROUPS):
            i = g * (nblk // SSM_GROUPS) + ib
            xs_blk = u[:, i * P2:(i + 1) * P2]
            xc = xs_blk * expand(dt, i)
            yd = jnp.zeros((L, P2), F32)
            for hh in range(2):
                head = 2 * i + hh
                seg = a_cs[:, head:head + 1] - a_cs_t[head:head + 1, :]
                wmat = (cb * jnp.where(tril, jnp.exp(seg), 0.0)).astype(BF16)
                keep = lane_lo if hh == 0 else jnp.logical_not(lane_lo)
                yd = yd + jnp.dot(wmat, jnp.where(keep, xc, 0.0).astype(BF16), preferred_element_type=F32)
            y = yd + y_off[:, ib * P2:(ib + 1) * P2] * expand(e_cs, i) + expand(dskip_ref[...], i) * xs_blk
            zf = z_ref[:, i * P2:(i + 1) * P2].astype(F32)
            y_blocks.append(y * (zf * _sigmoid(zf)))
            xd_blocks.append((xs_blk * expand(dt_ds, i)).astype(BF16))
        xd = jnp.concatenate(xd_blocks, axis=1)
        st_new = jnp.dot(bm.T.astype(BF16), xd, preferred_element_type=F32)
        decay = jnp.concatenate([expand(jnp.exp(a_last), g * (nblk // SSM_GROUPS) + ib)
                                 for ib in range(nblk // SSM_GROUPS)], axis=1)
        state_scr[:, g * gw:(g + 1) * gw] = st_g * decay + st_new

    per_g = nblk // SSM_GROUPS
    for g in range(SSM_GROUPS):
        blks = y_blocks[g * per_g:(g + 1) * per_g]
        ss = sum(jnp.sum(b * b, axis=1, keepdims=True) for b in blks) * (1.0 / gw)
        inv = lax.rsqrt(ss + NORM_EPS)
        for ib, b in enumerate(blks):
            i = g * per_g + ib
            o_ref[:, i * P2:(i + 1) * P2] = (b * inv * nw_ref[:, i * P2:(i + 1) * P2]).astype(BF16)


def _ssd(proj, dt_raw, conv_w, conv_b, dt_bias, a_log, d_skip, norm_w, batch, seq):
    n = batch * seq
    L = SSM_CHUNK
    nc = seq // L
    cd = SSM_WIDTH + SSM_BC
    row = lambda b, c: (b * nc + c)
    return pl.pallas_call(
        _ssd_kernel,
        grid=(batch, nc),
        in_specs=[pl.BlockSpec((L, SSM_WIDTH), lambda b, c: (row(b, c), 3)),
                  pl.BlockSpec((L, SSM_WIDTH), lambda b, c: (row(b, c), 4)),
                  pl.BlockSpec((L, SSM_BC), lambda b, c: (row(b, c), 10)),
                  pl.BlockSpec((L, LANES), lambda b, c: (row(b, c), 0)),
                  pl.BlockSpec((SSM_CONV, cd), lambda b, c: (0, 0)),
                  pl.BlockSpec((1, cd), lambda b, c: (0, 0)),
                  pl.BlockSpec((1, LANES), lambda b, c: (0, 0)),
                  pl.BlockSpec((1, LANES), lambda b, c: (0, 0)),
                  pl.BlockSpec((1, LANES), lambda b, c: (0, 0)),
                  pl.BlockSpec((1, SSM_WIDTH), lambda b, c: (0, 0))],
        out_specs=pl.BlockSpec((L, SSM_WIDTH), lambda b, c: (row(b, c), 0)),
        out_shape=jax.ShapeDtypeStruct((n, SSM_WIDTH), BF16),
        scratch_shapes=[pltpu.VMEM((L + 8, cd), F32),
                        pltpu.VMEM((SSM_STATE, SSM_WIDTH), F32)],
        compiler_params=_cparams("arbitrary", "arbitrary"),
    )(proj, proj, proj, dt_raw, conv_w, conv_b, dt_bias, a_log, d_skip, norm_w)


def _out_proj_kernel(x_ref, att_ref, ssm_ref, w_ref, mod_ref, nw_ref, h_ref, hn_ref):
    mix = (jnp.dot(att_ref[...], w_ref[0:ATT_WIDTH, :], preferred_element_type=F32)
           + jnp.dot(ssm_ref[...], w_ref[ATT_WIDTH:, :], preferred_element_type=F32))
    mod = mod_ref[0]
    h1 = x_ref[...] + mod[2:3, :] * mix
    h_ref[...] = h1
    y = h1 * lax.rsqrt(jnp.mean(h1 * h1, axis=1, keepdims=True) + NORM_EPS) * nw_ref[...]
    hn_ref[...] = (y * (1.0 + mod[4:5, :]) + mod[3:4, :]).astype(BF16)


def _out_proj(x2, att, ssm, w_out, mod3, norm2_w, seq):
    n, d = x2.shape
    tm = min(256, seq)
    return pl.pallas_call(
        _out_proj_kernel,
        grid=(n // tm,),
        in_specs=[pl.BlockSpec((tm, d), lambda i: (i, 0)),
                  pl.BlockSpec((tm, ATT_WIDTH), lambda i: (i, 0)),
                  pl.BlockSpec((tm, SSM_WIDTH), lambda i: (i, 0)),
                  pl.BlockSpec((ATT_WIDTH + SSM_WIDTH, d), lambda i: (0, 0)),
                  pl.BlockSpec((1, 6, d), lambda i: (i * tm // seq, 0, 0)),
                  pl.BlockSpec((1, d), lambda i: (0, 0))],
        out_specs=[pl.BlockSpec((tm, d), lambda i: (i, 0)),
                   pl.BlockSpec((tm, d), lambda i: (i, 0))],
        out_shape=[jax.ShapeDtypeStruct((n, d), F32),
                   jax.ShapeDtypeStruct((n, d), BF16)],
        compiler_params=_cparams("arbitrary"),
    )(x2, att, ssm, w_out, mod3, norm2_w)


def _topk_rows(s, k, rows):
    row = lax.broadcasted_iota(I32, s.shape, 0)
    vals, idxs = [], []
    for _ in range(k):
        m = jnp.max(s, axis=0, keepdims=True)
        idx = jnp.min(jnp.where(s == m, row, rows), axis=0, keepdims=True)
        s = jnp.where(row == idx, -jnp.inf, s)
        vals.append(m)
        idxs.append(idx)
    return vals, idxs


_CAND = [(i, j) for i in range(PEER_TOPK) for j in range(PEER_TOPK) if (i + 1) * (j + 1) <= PEER_TOPK]


def _route_kernel(hn_ref, wq_ref, keys_ref, a_ref, b_ref, g_ref, top_scr, aa_scr, bb_scr):
    t = hn_ref.shape[0]
    qp = jnp.dot(hn_ref[...], wq_ref[...], preferred_element_type=F32).astype(BF16)
    ncand = len(_CAND)
    pad = (-ncand) % 8
    for h in range(PEER_HEADS):
        sub = []
        for c in range(2):
            hc = 2 * h + c
            sc = lax.dot_general(keys_ref[hc], qp[:, hc * LANES:(hc + 1) * LANES], NT_DIMS,
                                 preferred_element_type=F32)
            sub.append(_topk_rows(sc, PEER_TOPK, PEER_NKEYS))
        (s1, i1), (s2, i2) = sub
        cand = jnp.concatenate([s1[i] + s2[j] for i, j in _CAND]
                               + [jnp.full((pad, t), -jnp.inf, F32)], axis=0)
        cand_a = jnp.concatenate([i1[i] for i, _ in _CAND] + [jnp.zeros((pad, t), I32)], axis=0)
        cand_b = jnp.concatenate([i2[j] for _, j in _CAND] + [jnp.zeros((pad, t), I32)], axis=0)
        row = lax.broadcasted_iota(I32, cand.shape, 0)
        for kk in range(PEER_TOPK):
            m = jnp.max(cand, axis=0, keepdims=True)
            idx = jnp.min(jnp.where(cand == m, row, ncand + pad), axis=0, keepdims=True)
            sel = row == idx
            slot = h * PEER_TOPK + kk
            top_scr[slot:slot + 1, :] = m
            aa_scr[slot:slot + 1, :] = jnp.max(jnp.where(sel, cand_a, -1), axis=0, keepdims=True)
            bb_scr[slot:slot + 1, :] = jnp.max(jnp.where(sel, cand_b, -1), axis=0, keepdims=True)
            cand = jnp.where(sel, -jnp.inf, cand)
        top = top_scr[h * PEER_TOPK:(h + 1) * PEER_TOPK, :]
        e = jnp.exp(top - jnp.max(top, axis=0, keepdims=True))
        top_scr[h * PEER_TOPK:(h + 1) * PEER_TOPK, :] = e / jnp.sum(e, axis=0, keepdims=True)
    a_ref[...] = aa_scr[...].T
    b_ref[...] = bb_scr[...].T
    g_ref[...] = top_scr[...].T


def _route(hn2, wq, keys):
    n, d = hn2.shape
    t = min(256, n)
    qd = wq.shape[1]
    out = jax.ShapeDtypeStruct((n, PEER_SLOTS), I32)
    return pl.pallas_call(
        _route_kernel,
        grid=(n // t,),
        in_specs=[pl.BlockSpec((t, d), lambda i: (i, 0)),
                  pl.BlockSpec((d, qd), lambda i: (0, 0)),
                  pl.BlockSpec(keys.shape, lambda i: (0, 0, 0))],
        out_specs=[pl.BlockSpec((t, PEER_SLOTS), lambda i: (i, 0))] * 3,
        out_shape=[out, out, jax.ShapeDtypeStruct((n, PEER_SLOTS), F32)],
        scratch_shapes=[pltpu.VMEM((PEER_SLOTS, t), F32),
                        pltpu.VMEM((PEER_SLOTS, t), I32),
                        pltpu.VMEM((PEER_SLOTS, t), I32)],
        compiler_params=_cparams("arbitrary"),
    )(hn2, wq, keys)


PAIR = 2 * PEER_NKEYS
UP_KEYS = 8


def _peer_down_kernel(x_ref, dn_ref, a_ref, b_ref, pre_ref):
    j = pl.program_id(1)

    @pl.when(j == 0)
    def _():
        pre_ref[...] = jnp.zeros(pre_ref.shape, F32)

    p = lax.dot_general(x_ref[...], dn_ref[...], NT_DIMS, preferred_element_type=F32)
    a = a_ref[...]
    b = b_ref[...]
    pre = pre_ref[...]
    for half in range(2):
        g = jnp.take_along_axis(p[:, half * LANES:(half + 1) * LANES], b, axis=1)
        pre = jnp.where(a == 2 * j + half, g, pre)
    pre_ref[...] = pre


def _peer_down(hn2, down16, aidx, bidx):
    n, d = hn2.shape
    t = min(1024, n)
    slot_spec = pl.BlockSpec((t, PEER_SLOTS), lambda i, j: (i, 0))
    return pl.pallas_call(
        _peer_down_kernel,
        grid=(n // t, down16.shape[0] // PAIR),
        in_specs=[pl.BlockSpec((t, d), lambda i, j: (i, 0)),
                  pl.BlockSpec((PAIR, d), lambda i, j: (j, 0)),
                  slot_spec, slot_spec],
        out_specs=slot_spec,
        out_shape=jax.ShapeDtypeStruct((n, PEER_SLOTS), F32),
        compiler_params=_cparams("arbitrary", "arbitrary"),
    )(hn2, down16, aidx, bidx)


def _peer_up_kernel(pre_ref, g_ref, a_ref, b_ref, up_ref, h_ref, mod_ref, o_ref,
                    act_scr, dense_scr, acc_scr):
    j = pl.program_id(1)
    t = pre_ref.shape[0]
    nk = PEER_NKEYS

    @pl.when(j == 0)
    def _():
        pre = pre_ref[...]
        act_scr[...] = 0.5 * pre * (1.0 + lax.erf(pre * (1.0 / math.sqrt(2.0)))) * g_ref[...]
        acc_scr[...] = jnp.zeros(acc_scr.shape, F32)
        row = lax.broadcasted_iota(I32, (nk, PEER_SLOTS), 0)

        def body(i, carry):
            base = pl.multiple_of(i * 8, 8)
            a8 = a_ref[pl.ds(base, 8), :]
            b8 = b_ref[pl.ds(base, 8), :]
            c8 = act_scr[pl.ds(base, 8), :]
            for u in range(8):
                xa = jnp.where(row == a8[u:u + 1, :], c8[u:u + 1, :], 0.0).astype(BF16)
                yb = jnp.where(row == b8[u:u + 1, :], 1.0, 0.0).astype(BF16)
                dense_scr[pl.ds(pl.multiple_of((base + u) * nk, nk), nk), :] = lax.dot_general(
                    xa, yb, NT_DIMS, preferred_element_type=F32)
            return carry

        lax.fori_loop(0, t // 8, body, 0)

    lhs = jnp.concatenate([dense_scr[pl.ds(UP_KEYS * j + u, t, stride=nk), :] for u in range(UP_KEYS)],
                          axis=1).astype(BF16)
    acc_scr[...] += jnp.dot(lhs, up_ref[...], preferred_element_type=F32)

    @pl.when(j == pl.num_programs(1) - 1)
    def _():
        o_ref[...] = h_ref[...] + mod_ref[0][5:6, :] * acc_scr[...]


def _peer_up(pre, gate, aidx, bidx, up16, h1, mod3, seq):
    n, d = h1.shape
    t = min(256, seq)
    slot_spec = pl.BlockSpec((t, PEER_SLOTS), lambda i, j: (i, 0))
    return pl.pallas_call(
        _peer_up_kernel,
        grid=(n // t, up16.shape[0] // (UP_KEYS * PEER_NKEYS)),
        in_specs=[slot_spec, slot_spec, slot_spec, slot_spec,
                  pl.BlockSpec((UP_KEYS * PEER_NKEYS, d), lambda i, j: (j, 0)),
                  pl.BlockSpec((t, d), lambda i, j: (i, 0)),
                  pl.BlockSpec((1, 6, d), lambda i, j: (i * t // seq, 0, 0))],
        out_specs=pl.BlockSpec((t, d), lambda i, j: (i, 0)),
        out_shape=jax.ShapeDtypeStruct((n, d), F32),
        scratch_shapes=[pltpu.VMEM((t, PEER_SLOTS), F32),
                        pltpu.VMEM((t * PEER_NKEYS, PEER_NKEYS), F32),
                        pltpu.VMEM((t, d), F32)],
        compiler_params=_cparams("arbitrary", "arbitrary"),
    )(pre, gate, aidx, bidx, up16, h1, mod3)


def _pad_lanes(v):
    return jnp.pad(v.astype(F32), (0, LANES - v.shape[0])).reshape(1, LANES)


def _layer(h2, mod3, l, batch, seq, norm1_w, w_in, q_norm_w, k_norm_w, rel_bias, lambda_q1, lambda_k1,
           lambda_q2, lambda_k2, subln_w, conv_w, conv_b, dt_bias, a_log, d_skip, ssm_norm_w, w_out,
           norm2_w, peer_wq, peer_keys, expert_down, expert_up):
    d = h2.shape[1]
    lam_init = 0.8 - 0.6 * math.exp(-0.3 * l)
    w16 = w_in.astype(BF16)
    w_main = w16[:, :MAIN_COLS]
    w_dt = jnp.pad(w16[:, MAIN_COLS:], ((0, 0), (0, LANES - SSM_HEADS)))
    qn = jnp.tile(q_norm_w.astype(F32) * HEAD_DIM ** -0.5, 2).reshape(1, LANES)
    kn = jnp.tile(k_norm_w.astype(F32), 2).reshape(1, LANES)
    proj, dt_raw = _in_proj(h2, norm1_w.reshape(1, d), mod3, w_main, w_dt, qn, kn, seq)

    lamv = jnp.pad(jnp.stack([lambda_q1, lambda_k1, lambda_q2, lambda_k2]).astype(F32),
                   ((0, 4), (0, LANES - HEAD_DIM)))
    att = _attention(rel_bias.astype(F32).reshape(-1), proj, lamv, subln_w.reshape(1, LANES),
                     batch, seq, lam_init)
    ssm = _ssd(proj, dt_raw, conv_w, conv_b.reshape(1, -1), _pad_lanes(dt_bias), _pad_lanes(a_log),
               _pad_lanes(d_skip), ssm_norm_w.reshape(1, -1), batch, seq)
    h1, hn2 = _out_proj(h2, att, ssm, w_out.astype(BF16), mod3, norm2_w.reshape(1, d), seq)

    keys = peer_keys.astype(BF16).reshape(2 * PEER_HEADS, PEER_NKEYS, -1)
    aidx, bidx, gate = _route(hn2, peer_wq.astype(BF16), keys)
    pre = _peer_down(hn2, expert_down.astype(BF16), aidx, bidx)
    return _peer_up(pre, gate, aidx, bidx, expert_up.astype(BF16), h1, mod3, seq)


def kernel(x, c, ada_w, ada_b, norm1_w, w_in, q_norm_w, k_norm_w, rel_bias, lambda_q1, lambda_k1, lambda_q2, lambda_k2, subln_w, conv_w, conv_b, dt_bias, a_log, d_skip, ssm_norm_w, w_out, norm2_w, peer_wq, peer_keys, expert_down, expert_up):
    batch, seq, d = x.shape
    depth = ada_w.shape[0]
    h2 = x.reshape(batch * seq, d)
    c_pad = jnp.pad(c, ((0, 8 - batch), (0, 0)))
    for l in range(depth):
        mod = _ada(c_pad, ada_w[l], ada_b[l].reshape(1, -1))
        mod3 = mod[:batch].reshape(batch, 6, d)
        h2 = _layer(h2, mod3, l, batch, seq, norm1_w[l], w_in[l], q_norm_w[l], k_norm_w[l], rel_bias,
                    lambda_q1[l], lambda_k1[l], lambda_q2[l], lambda_k2[l], subln_w[l], conv_w[l],
                    conv_b[l], dt_bias[l], a_log[l], d_skip[l], ssm_norm_w[l], w_out[l], norm2_w[l],
                    peer_wq[l], peer_keys[l], expert_down[l], expert_up[l])
    return h2.reshape(batch, seq, d)
```

```python
import functools
import math

import jax
import jax.numpy as jnp
from jax import lax
from jax.experimental import pallas as pl
from jax.experimental.pallas import tpu as pltpu

F32 = jnp.float32
BF16 = jnp.bfloat16
I32 = jnp.int32

LANES = 128
VMEM_LIMIT = 56 * 1024 * 1024

NORM_EPS = 1e-6
HEAD_DIM = 64
ATT_HEADS = 8
ATT_WIDTH = 1024
SSM_WIDTH = 1024
SSM_HEADS = 16
SSM_GROUPS = 2
SSM_STATE = 128
SSM_CONV = 4
SSM_CHUNK = 128
SSM_BC = 2 * SSM_GROUPS * SSM_STATE
REL_BUCKETS = 32
REL_MAX_DIST = 128
PEER_HEADS = 8
PEER_NKEYS = 128
PEER_TOPK = 16
PEER_SLOTS = PEER_HEADS * PEER_TOPK
MAIN_COLS = 3 * ATT_WIDTH + SSM_WIDTH + SSM_WIDTH + SSM_BC
NEG = -1e30
LOG2E = math.log2(math.e)

NT_DIMS = (((1,), (1,)), ((), ()))


def _cparams(*sem):
    return pltpu.CompilerParams(dimension_semantics=sem, vmem_limit_bytes=VMEM_LIMIT)


def _sigmoid(x):
    return 1.0 / (1.0 + jnp.exp(-x))


def _ada_kernel(c_ref, w_ref, b_ref, o_ref):
    c = c_ref[...]
    sc = (c * _sigmoid(c)).astype(BF16)
    o_ref[...] = jnp.dot(sc, w_ref[...].astype(BF16), preferred_element_type=F32) + b_ref[...]


def _ada(c_pad, ada_w, ada_b):
    rows, d = c_pad.shape
    n = ada_w.shape[1]
    tn = 1536
    return pl.pallas_call(
        _ada_kernel,
        grid=(n // tn,),
        in_specs=[pl.BlockSpec((rows, d), lambda j: (0, 0)),
                  pl.BlockSpec((d, tn), lambda j: (0, j)),
                  pl.BlockSpec((1, tn), lambda j: (0, j))],
        out_specs=pl.BlockSpec((rows, tn), lambda j: (0, j)),
        out_shape=jax.ShapeDtypeStruct((rows, n), F32),
        compiler_params=_cparams("arbitrary"),
    )(c_pad, ada_w, ada_b)


def _group_rms(blk, w_row, lo):
    sq = blk * blk
    s_all = jnp.sum(sq, axis=1, keepdims=True)
    s_lo = jnp.sum(jnp.where(lo, sq, 0.0), axis=1, keepdims=True)
    s = jnp.where(lo, s_lo, s_all - s_lo)
    return blk * lax.rsqrt(s * (1.0 / HEAD_DIM) + NORM_EPS) * w_row


def _in_proj_kernel(x_ref, nw_ref, mod_ref, w_ref, wdt_ref, qn_ref, kn_ref,
                    o_ref, dt_ref, hn_scr, *, tn):
    j = pl.program_id(1)

    @pl.when(j == 0)
    def _():
        x = x_ref[...]
        y = x * lax.rsqrt(jnp.mean(x * x, axis=1, keepdims=True) + NORM_EPS) * nw_ref[...]
        mod = mod_ref[0]
        hn = (y * (1.0 + mod[1:2, :]) + mod[0:1, :]).astype(BF16)
        hn_scr[...] = hn
        dt_ref[...] = jnp.dot(hn, wdt_ref[...], preferred_element_type=F32)

    acc = jnp.dot(hn_scr[...], w_ref[...], preferred_element_type=F32)
    qk_tiles = 2 * ATT_WIDTH // tn

    @pl.when(j < qk_tiles)
    def _():
        lo = lax.broadcasted_iota(I32, (1, LANES), 1) < HEAD_DIM
        w_row = jnp.where(j < qk_tiles // 2, qn_ref[...], kn_ref[...])
        for cb in range(tn // LANES):
            blk = acc[:, cb * LANES:(cb + 1) * LANES]
            o_ref[:, cb * LANES:(cb + 1) * LANES] = _group_rms(blk, w_row, lo).astype(BF16)

    @pl.when(j >= qk_tiles)
    def _():
        o_ref[...] = acc.astype(BF16)


def _in_proj(x2, norm_w, mod3, w_main, w_dt, qn, kn, seq):
    n, d = x2.shape
    tm = min(1024, seq)
    tn = 512
    return pl.pallas_call(
        functools.partial(_in_proj_kernel, tn=tn),
        grid=(n // tm, MAIN_COLS // tn),
        in_specs=[pl.BlockSpec((tm, d), lambda i, j: (i, 0)),
                  pl.BlockSpec((1, d), lambda i, j: (0, 0)),
                  pl.BlockSpec((1, 6, d), lambda i, j: (i * tm // seq, 0, 0)),
                  pl.BlockSpec((d, tn), lambda i, j: (0, j)),
                  pl.BlockSpec((d, LANES), lambda i, j: (0, 0)),
                  pl.BlockSpec((1, LANES), lambda i, j: (0, 0)),
                  pl.BlockSpec((1, LANES), lambda i, j: (0, 0))],
        out_specs=[pl.BlockSpec((tm, tn), lambda i, j: (i, j)),
                   pl.BlockSpec((tm, LANES), lambda i, j: (i, 0))],
        out_shape=[jax.ShapeDtypeStruct((n, MAIN_COLS), BF16),
                   jax.ShapeDtypeStruct((n, LANES), F32)],
        scratch_shapes=[pltpu.VMEM((tm, d), BF16)],
        compiler_params=_cparams("arbitrary", "arbitrary"),
    )(x2, norm_w, mod3, w_main, w_dt, qn, kn)


ATT_GROUP = 2


def _attn_kernel(relb_ref, q_ref, k_ref, vt_ref, lamv_ref, subw_ref, o_ref,
                 q2t_scr, acc_scr, bias_scr, s_scr, *, t, lam_init):
    hp = pl.program_id(1)
    qi = pl.program_id(2)
    nchunk = vt_ref.shape[0] // ATT_GROUP

    @pl.when(qi == 0)
    def _():
        kk = lax.broadcasted_iota(I32, (t, t), 0)
        qq = lax.broadcasted_iota(I32, (t, t), 1)
        max_exact = REL_BUCKETS // 2
        for tile, off in ((0, 0), (1, t)):
            rel = qq - kk + off
            nn = jnp.maximum(rel, 0)
            nf = jnp.maximum(nn, 1).astype(F32)
            large = max_exact + (jnp.log(nf / max_exact) / math.log(REL_MAX_DIST / max_exact)
                                 * (REL_BUCKETS - max_exact)).astype(I32)
            large = jnp.minimum(large, REL_BUCKETS - 1)
            bucket = jnp.where(nn < max_exact, nn, large)
            for hh in range(ATT_GROUP):
                head = hp * ATT_GROUP + hh
                for m in range(2):
                    far = relb_ref[(REL_BUCKETS - 1) * 2 * ATT_HEADS + head * 2 + m]
                    val = jnp.zeros((t, t), F32)
                    for b in range(REL_BUCKETS - 1):
                        delta = (relb_ref[b * 2 * ATT_HEADS + head * 2 + m] - far) * LOG2E
                        val = jnp.where(bucket == b, delta, val)
                    bias_scr[hh, tile, :, m * t:(m + 1) * t] = jnp.where(rel >= 0, val, NEG)

    d_lo = lax.broadcasted_iota(I32, (LANES, t), 0) < HEAD_DIM
    for hh in range(ATT_GROUP):
        qt = q_ref[:, hh * LANES:(hh + 1) * LANES].astype(F32).T
        q2t_scr[hh, :, 0:t] = jnp.where(d_lo, qt, 0.0).astype(BF16)
        q2t_scr[hh, :, t:2 * t] = jnp.where(d_lo, 0.0, qt).astype(BF16)
        acc_scr[hh] = jnp.zeros((LANES, 2 * t), F32)

    def scores(c, hh):
        k_c = k_ref[pl.ds(pl.multiple_of(c * t, t), t), hh * LANES:(hh + 1) * LANES]
        return jnp.dot(k_c, q2t_scr[hh], preferred_element_type=F32)

    def issue(c, slot):
        for hh in range(ATT_GROUP):
            s_scr[slot, hh] = scores(c, hh)

    def consume(c, carry, s_of):
        out = []
        for hh in range(ATT_GROUP):
            m_prev, l_prev = carry[hh]
            s = s_of(hh)
            m_new = jnp.maximum(m_prev, jnp.max(s, axis=0, keepdims=True))
            alpha = jnp.exp2(m_prev - m_new)
            p = jnp.exp2(s - m_new)
            l_new = alpha * l_prev + jnp.sum(p, axis=0, keepdims=True)
            pv = jnp.dot(vt_ref[hh * nchunk + c], p.astype(BF16), preferred_element_type=F32)
            acc_scr[hh] = alpha * acc_scr[hh] + pv
            out.append((m_new, l_new))
        return tuple(out)

    nfar = jnp.maximum(qi - 1, 0)

    @pl.when(nfar > 0)
    def _():
        issue(0, 0)

    def pair(i, carry):
        issue(2 * i + 1, 1)
        carry = consume(2 * i, carry, lambda hh: s_scr[0, hh])
        issue(jnp.minimum(2 * i + 2, nfar - 1), 0)
        return consume(2 * i + 1, carry, lambda hh: s_scr[1, hh])

    init = tuple((jnp.full((1, 2 * t), NEG, F32), jnp.zeros((1, 2 * t), F32)) for _ in range(ATT_GROUP))
    carry = lax.fori_loop(0, nfar // 2, pair, init)
    carry = lax.cond(nfar % 2 == 1, lambda cr: consume(nfar - 1, cr, lambda hh: s_scr[0, hh]),
                     lambda cr: cr, carry)
    carry = lax.cond(qi > 0, lambda cr: consume(qi - 1, cr, lambda hh: scores(qi - 1, hh) + bias_scr[hh, 1]),
                     lambda cr: cr, carry)
    carry = consume(qi, carry, lambda hh: scores(qi, hh) + bias_scr[hh, 0])

    lv = lamv_ref[...]
    lam = (jnp.exp(jnp.sum(lv[0:1, :] * lv[1:2, :], axis=1, keepdims=True))
           - jnp.exp(jnp.sum(lv[2:3, :] * lv[3:4, :], axis=1, keepdims=True)) + lam_init)
    for hh in range(ATT_GROUP):
        inv_l = 1.0 / carry[hh][1]
        acc = acc_scr[hh] * inv_l
        o = (acc[:, 0:t] - lam * acc[:, t:2 * t]).T
        o = o * lax.rsqrt(jnp.mean(o * o, axis=1, keepdims=True) + NORM_EPS) * subw_ref[...]
        o_ref[:, hh * LANES:(hh + 1) * LANES] = (o * (1.0 - lam_init)).astype(BF16)


def _attention(relb, proj, lamv, subw, batch, seq, lam_init):
    n = batch * seq
    t = min(256, seq)
    nq = seq // t
    gw = ATT_GROUP * LANES
    ngroups = ATT_HEADS // ATT_GROUP
    vt = proj[:, 2 * ATT_WIDTH:3 * ATT_WIDTH].reshape(batch, nq, t, ATT_HEADS, LANES)
    vt = vt.transpose(0, 3, 1, 4, 2).reshape(batch * ATT_HEADS * nq, LANES, t)
    return pl.pallas_call(
        functools.partial(_attn_kernel, t=t, lam_init=lam_init),
        grid=(batch, ngroups, nq),
        in_specs=[pl.BlockSpec(memory_space=pltpu.SMEM),
                  pl.BlockSpec((t, gw), lambda b, g, i: (b * nq + i, g)),
                  pl.BlockSpec((seq, gw), lambda b, g, i: (b, ngroups + g)),
                  pl.BlockSpec((ATT_GROUP * nq, LANES, t), lambda b, g, i: (b * ngroups + g, 0, 0)),
                  pl.BlockSpec((8, LANES), lambda b, g, i: (0, 0)),
                  pl.BlockSpec((1, LANES), lambda b, g, i: (0, 0))],
        out_specs=pl.BlockSpec((t, gw), lambda b, g, i: (b * nq + i, g)),
        out_shape=jax.ShapeDtypeStruct((n, ATT_WIDTH), BF16),
        scratch_shapes=[pltpu.VMEM((ATT_GROUP, LANES, 2 * t), BF16),
                        pltpu.VMEM((ATT_GROUP, LANES, 2 * t), F32),
                        pltpu.VMEM((ATT_GROUP, 2, t, 2 * t), F32),
                        pltpu.VMEM((2, ATT_GROUP, t, 2 * t), F32)],
        compiler_params=_cparams("arbitrary", "arbitrary", "arbitrary"),
    )(relb, proj, proj, vt, lamv, subw)


def _split3(x):
    hi = x.astype(BF16)
    r1 = x - hi.astype(F32)
    mid = r1.astype(BF16)
    lo = (r1 - mid.astype(F32)).astype(BF16)
    return hi, mid, lo


def _ssd_kernel(z_ref, xs_ref, bc_ref, dt_ref, cw_ref, cb_ref, dtb_ref, alog_ref, dskip_ref, nw_ref,
                o_ref, xpad_scr, state_scr):
    L = SSM_CHUNK
    W = SSM_WIDTH
    P2 = LANES
    nblk = W // P2
    gw = W // SSM_GROUPS

    @pl.when(pl.program_id(1) == 0)
    def _():
        xpad_scr[0:8, :] = jnp.zeros((8, W + SSM_BC), F32)
        state_scr[...] = jnp.zeros((SSM_STATE, W), F32)

    xpad_scr[8:8 + L, 0:W] = xs_ref[...].astype(F32)
    xpad_scr[8:8 + L, W:W + SSM_BC] = bc_ref[...].astype(F32)
    conv = cb_ref[...] + cw_ref[0:1, :] * xpad_scr[5:5 + L, :]
    for kk in range(1, SSM_CONV):
        conv = conv + cw_ref[kk:kk + 1, :] * xpad_scr[5 + kk:5 + kk + L, :]
    xpad_scr[0:8, :] = xpad_scr[L:L + 8, :]
    u = conv * _sigmoid(conv)

    dtr = dt_ref[...] + dtb_ref[...]
    dt = jnp.maximum(dtr, 0.0) + jnp.log1p(jnp.exp(-jnp.abs(dtr)))
    a = -jnp.exp(alog_ref[...])
    da = dt * a

    ri = lax.broadcasted_iota(I32, (L, L), 0)
    ci = lax.broadcasted_iota(I32, (L, L), 1)
    tril = ri >= ci
    tri = jnp.where(tril, 1.0, 0.0).astype(BF16)
    hi, mid, lo3 = _split3(da)
    a_cs = (jnp.dot(tri, hi, preferred_element_type=F32) + jnp.dot(tri, mid, preferred_element_type=F32)
            + jnp.dot(tri, lo3, preferred_element_type=F32))
    a_cs_t = a_cs.T
    a_last = a_cs[L - 1:L, :]
    e_cs = jnp.exp(a_cs)
    dt_ds = dt * jnp.exp(a_last - a_cs)

    lane_lo = lax.broadcasted_iota(I32, (1, P2), 1) < HEAD_DIM

    def expand(mat, i):
        return jnp.where(lane_lo, mat[:, 2 * i:2 * i + 1], mat[:, 2 * i + 1:2 * i + 2])

    y_blocks = []
    for g in range(SSM_GROUPS):
        bm = u[:, W + g * SSM_STATE:W + (g + 1) * SSM_STATE]
        cm = u[:, W + (SSM_GROUPS + g) * SSM_STATE:W + (SSM_GROUPS + g + 1) * SSM_STATE]
        bm16 = bm.astype(BF16)
        cm16 = cm.astype(BF16)
        cb = lax.dot_general(cm16, bm16, NT_DIMS, preferred_element_type=F32)
        st_g = state_scr[:, g * gw:(g + 1) * gw]
        y_off = jnp.dot(cm16, st_g.astype(BF16), preferred_element_type=F32)
        xd_blocks = []
        for ib in range(nblk // SSM_GROUPS):
            i = g * (nblk // SSM_GROUPS) + ib
            xs_blk = u[:, i * P2:(i + 1) * P2]
            xc = xs_blk * expand(dt, i)
            yd = jnp.zeros((L, P2), F32)
            for hh in range(2):
                head = 2 * i + hh
                seg = a_cs[:, head:head + 1] - a_cs_t[head:head + 1, :]
                wmat = (cb * jnp.where(tril, jnp.exp(seg), 0.0)).astype(BF16)
                keep = lane_lo if hh == 0 else jnp.logical_not(lane_lo)
                yd = yd + jnp.dot(wmat, jnp.where(keep, xc, 0.0).astype(BF16), preferred_element_type=F32)
            y = yd + y_off[:, ib * P2:(ib + 1) * P2] * expand(e_cs, i) + expand(dskip_ref[...], i) * xs_blk
            zf = z_ref[:, i * P2:(i + 1) * P2].astype(F32)
            y_blocks.append(y * (zf * _sigmoid(zf)))
            xd_blocks.append((xs_blk * expand(dt_ds, i)).astype(BF16))
        xd = jnp.concatenate(xd_blocks, axis=1)
        st_new = jnp.dot(bm.T.astype(BF16), xd, preferred_element_type=F32)
        decay = jnp.concatenate([expand(jnp.exp(a_last), g * (nblk // SSM_GROUPS) + ib)
                                 for ib in range(nblk // SSM_GROUPS)], axis=1)
        state_scr[:, g * gw:(g + 1) * gw] = st_g * decay + st_new

    per_g = nblk // SSM_GROUPS
    for g in range(SSM_GROUPS):
        blks = y_blocks[g * per_g:(g + 1) * per_g]
        ss = sum(jnp.sum(b * b, axis=1, keepdims=True) for b in blks) * (1.0 / gw)
        inv = lax.rsqrt(ss + NORM_EPS)
        for ib, b in enumerate(blks):
            i = g * per_g + ib
            o_ref[:, i * P2:(i + 1) * P2] = (b * inv * nw_ref[:, i * P2:(i + 1) * P2]).astype(BF16)


def _ssd(proj, dt_raw, conv_w, conv_b, dt_bias, a_log, d_skip, norm_w, batch, seq):
    n = batch * seq
    L = SSM_CHUNK
    nc = seq // L
    cd = SSM_WIDTH + SSM_BC
    row = lambda b, c: (b * nc + c)
    return pl.pallas_call(
        _ssd_kernel,
        grid=(batch, nc),
        in_specs=[pl.BlockSpec((L, SSM_WIDTH), lambda b, c: (row(b, c), 3)),
                  pl.BlockSpec((L, SSM_WIDTH), lambda b, c: (row(b, c), 4)),
                  pl.BlockSpec((L, SSM_BC), lambda b, c: (row(b, c), 10)),
                  pl.BlockSpec((L, LANES), lambda b, c: (row(b, c), 0)),
                  pl.BlockSpec((SSM_CONV, cd), lambda b, c: (0, 0)),
                  pl.BlockSpec((1, cd), lambda b, c: (0, 0)),
                  pl.BlockSpec((1, LANES), lambda b, c: (0, 0)),
                  pl.BlockSpec((1, LANES), lambda b, c: (0, 0)),
                  pl.BlockSpec((1, LANES), lambda b, c: (0, 0)),
                  pl.BlockSpec((1, SSM_WIDTH), lambda b, c: (0, 0))],
        out_specs=pl.BlockSpec((L, SSM_WIDTH), lambda b, c: (row(b, c), 0)),
        out_shape=jax.ShapeDtypeStruct((n, SSM_WIDTH), BF16),
        scratch_shapes=[pltpu.VMEM((L + 8, cd), F32),
                        pltpu.VMEM((SSM_STATE, SSM_WIDTH), F32)],
        compiler_params=_cparams("arbitrary", "arbitrary"),
    )(proj, proj, proj, dt_raw, conv_w, conv_b, dt_bias, a_log, d_skip, norm_w)


def _out_proj_kernel(x_ref, att_ref, ssm_ref, w_ref, mod_ref, nw_ref, h_ref, hn_ref):
    mix = (jnp.dot(att_ref[...], w_ref[0:ATT_WIDTH, :], preferred_element_type=F32)
           + jnp.dot(ssm_ref[...], w_ref[ATT_WIDTH:, :], preferred_element_type=F32))
    mod = mod_ref[0]
    h1 = x_ref[...] + mod[2:3, :] * mix
    h_ref[...] = h1
    y = h1 * lax.rsqrt(jnp.mean(h1 * h1, axis=1, keepdims=True) + NORM_EPS) * nw_ref[...]
    hn_ref[...] = (y * (1.0 + mod[4:5, :]) + mod[3:4, :]).astype(BF16)


def _out_proj(x2, att, ssm, w_out, mod3, norm2_w, seq):
    n, d = x2.shape
    tm = min(256, seq)
    return pl.pallas_call(
        _out_proj_kernel,
        grid=(n // tm,),
        in_specs=[pl.BlockSpec((tm, d), lambda i: (i, 0)),
                  pl.BlockSpec((tm, ATT_WIDTH), lambda i: (i, 0)),
                  pl.BlockSpec((tm, SSM_WIDTH), lambda i: (i, 0)),
                  pl.BlockSpec((ATT_WIDTH + SSM_WIDTH, d), lambda i: (0, 0)),
                  pl.BlockSpec((1, 6, d), lambda i: (i * tm // seq, 0, 0)),
                  pl.BlockSpec((1, d), lambda i: (0, 0))],
        out_specs=[pl.BlockSpec((tm, d), lambda i: (i, 0)),
                   pl.BlockSpec((tm, d), lambda i: (i, 0))],
        out_shape=[jax.ShapeDtypeStruct((n, d), F32),
                   jax.ShapeDtypeStruct((n, d), BF16)],
        compiler_params=_cparams("arbitrary"),
    )(x2, att, ssm, w_out, mod3, norm2_w)


def _topk_rows(s, k, rows):
    row = lax.broadcasted_iota(I32, s.shape, 0)
    vals, idxs = [], []
    for _ in range(k):
        m = jnp.max(s, axis=0, keepdims=True)
        idx = jnp.min(jnp.where(s == m, row, rows), axis=0, keepdims=True)
        s = jnp.where(row == idx, -jnp.inf, s)
        vals.append(m)
        idxs.append(idx)
    return vals, idxs


_CAND = [(i, j) for i in range(PEER_TOPK) for j in range(PEER_TOPK) if (i + 1) * (j + 1) <= PEER_TOPK]


def _route_kernel(hn_ref, wq_ref, keys_ref, a_ref, b_ref, g_ref, top_scr, aa_scr, bb_scr):
    t = hn_ref.shape[0]
    qp = jnp.dot(hn_ref[...], wq_ref[...], preferred_element_type=F32).astype(BF16)
    ncand = len(_CAND)
    pad = (-ncand) % 8
    for h in range(PEER_HEADS):
        sub = []
        for c in range(2):
            hc = 2 * h + c
            sc = lax.dot_general(keys_ref[hc], qp[:, hc * LANES:(hc + 1) * LANES], NT_DIMS,
                                 preferred_element_type=F32)
            sub.append(_topk_rows(sc, PEER_TOPK, PEER_NKEYS))
        (s1, i1), (s2, i2) = sub
        cand = jnp.concatenate([s1[i] + s2[j] for i, j in _CAND]
                               + [jnp.full((pad, t), -jnp.inf, F32)], axis=0)
        cand_a = jnp.concatenate([i1[i] for i, _ in _CAND] + [jnp.zeros((pad, t), I32)], axis=0)
        cand_b = jnp.concatenate([i2[j] for _, j in _CAND] + [jnp.zeros((pad, t), I32)], axis=0)
        row = lax.broadcasted_iota(I32, cand.shape, 0)
        for kk in range(PEER_TOPK):
            m = jnp.max(cand, axis=0, keepdims=True)
            idx = jnp.min(jnp.where(cand == m, row, ncand + pad), axis=0, keepdims=True)
            sel = row == idx
            slot = h * PEER_TOPK + kk
            top_scr[slot:slot + 1, :] = m
            aa_scr[slot:slot + 1, :] = jnp.max(jnp.where(sel, cand_a, -1), axis=0, keepdims=True)
            bb_scr[slot:slot + 1, :] = jnp.max(jnp.where(sel, cand_b, -1), axis=0, keepdims=True)
            cand = jnp.where(sel, -jnp.inf, cand)
        top = top_scr[h * PEER_TOPK:(h + 1) * PEER_TOPK, :]
        e = jnp.exp(top - jnp.max(top, axis=0, keepdims=True))
        top_scr[h * PEER_TOPK:(h + 1) * PEER_TOPK, :] = e / jnp.sum(e, axis=0, keepdims=True)
    a_ref[...] = aa_scr[...].T
    b_ref[...] = bb_scr[...].T
    g_ref[...] = top_scr[...].T


def _route(hn2, wq, keys):
    n, d = hn2.shape
    t = min(256, n)
    qd = wq.shape[1]
    out = jax.ShapeDtypeStruct((n, PEER_SLOTS), I32)
    return pl.pallas_call(
        _route_kernel,
        grid=(n // t,),
        in_specs=[pl.BlockSpec((t, d), lambda i: (i, 0)),
                  pl.BlockSpec((d, qd), lambda i: (0, 0)),
                  pl.BlockSpec(keys.shape, lambda i: (0, 0, 0))],
        out_specs=[pl.BlockSpec((t, PEER_SLOTS), lambda i: (i, 0))] * 3,
        out_shape=[out, out, jax.ShapeDtypeStruct((n, PEER_SLOTS), F32)],
        scratch_shapes=[pltpu.VMEM((PEER_SLOTS, t), F32),
                        pltpu.VMEM((PEER_SLOTS, t), I32),
                        pltpu.VMEM((PEER_SLOTS, t), I32)],
        compiler_params=_cparams("arbitrary"),
    )(hn2, wq, keys)


PAIR = 2 * PEER_NKEYS
DOWN_PAIRS = 4
UP_KEYS = 16
DENSE_PITCH = PEER_NKEYS + 8


def _peer_down_kernel(x_ref, dn_ref, a_ref, b_ref, pre_ref):
    j = pl.program_id(1)

    @pl.when(j == 0)
    def _():
        pre_ref[...] = jnp.zeros(pre_ref.shape, F32)

    x = x_ref[...]
    a = a_ref[...]
    b = b_ref[...]
    pre = pre_ref[...]
    for q in range(DOWN_PAIRS):
        p = lax.dot_general(x, dn_ref[q * PAIR:(q + 1) * PAIR, :], NT_DIMS,
                            preferred_element_type=F32)
        for half in range(2):
            g = jnp.take_along_axis(p[:, half * LANES:(half + 1) * LANES], b, axis=1)
            pre = jnp.where(a == 2 * (DOWN_PAIRS * j + q) + half, g, pre)
    pre_ref[...] = pre


def _peer_down(hn2, down16, aidx, bidx):
    n, d = hn2.shape
    t = min(1024, n)
    slot_spec = pl.BlockSpec((t, PEER_SLOTS), lambda i, j: (i, 0))
    return pl.pallas_call(
        _peer_down_kernel,
        grid=(n // t, down16.shape[0] // (DOWN_PAIRS * PAIR)),
        in_specs=[pl.BlockSpec((t, d), lambda i, j: (i, 0)),
                  pl.BlockSpec((DOWN_PAIRS * PAIR, d), lambda i, j: (j, 0)),
                  slot_spec, slot_spec],
        out_specs=slot_spec,
        out_shape=jax.ShapeDtypeStruct((n, PEER_SLOTS), F32),
        compiler_params=_cparams("arbitrary", "arbitrary"),
    )(hn2, down16, aidx, bidx)


def _peer_up_kernel(pre_ref, g_ref, a_ref, b_ref, up_ref, h_ref, mod_ref, o_ref,
                    act_scr, dense_scr, acc_scr):
    j = pl.program_id(1)
    t = pre_ref.shape[0]
    nk = PEER_NKEYS

    @pl.when(j == 0)
    def _():
        pre = pre_ref[...]
        act_scr[...] = 0.5 * pre * (1.0 + lax.erf(pre * (1.0 / math.sqrt(2.0)))) * g_ref[...]
        acc_scr[...] = jnp.zeros(acc_scr.shape, F32)
        row = lax.broadcasted_iota(I32, (nk, PEER_SLOTS), 0)

        def body(i, carry):
            base = pl.multiple_of(i * 8, 8)
            a8 = a_ref[pl.ds(base, 8), :]
            b8 = b_ref[pl.ds(base, 8), :]
            c8 = act_scr[pl.ds(base, 8), :]
            for u in range(8):
                xa = jnp.where(row == a8[u:u + 1, :], c8[u:u + 1, :], 0.0).astype(BF16)
                yb = jnp.where(row == b8[u:u + 1, :], 1.0, 0.0).astype(BF16)
                dense_scr[pl.ds(pl.multiple_of((base + u) * DENSE_PITCH, 8), nk), :] = lax.dot_general(
                    xa, yb, NT_DIMS, preferred_element_type=F32)
            return carry

        lax.fori_loop(0, t // 8, body, 0)

    lhs = jnp.concatenate([dense_scr[pl.ds(UP_KEYS * j + u, t, stride=DENSE_PITCH), :]
                           for u in range(UP_KEYS)], axis=1).astype(BF16)
    acc_scr[...] += jnp.dot(lhs, up_ref[...], preferred_element_type=F32)

    @pl.when(j == pl.num_programs(1) - 1)
    def _():
        o_ref[...] = h_ref[...] + mod_ref[0][5:6, :] * acc_scr[...]


def _peer_up(pre, gate, aidx, bidx, up16, h1, mod3, seq):
    n, d = h1.shape
    t = min(256, seq)
    slot_spec = pl.BlockSpec((t, PEER_SLOTS), lambda i, j: (i, 0))
    return pl.pallas_call(
        _peer_up_kernel,
        grid=(n // t, up16.shape[0] // (UP_KEYS * PEER_NKEYS)),
        in_specs=[slot_spec, slot_spec, slot_spec, slot_spec,
                  pl.BlockSpec((UP_KEYS * PEER_NKEYS, d), lambda i, j: (j, 0)),
                  pl.BlockSpec((t, d), lambda i, j: (i, 0)),
                  pl.BlockSpec((1, 6, d), lambda i, j: (i * t // seq, 0, 0))],
        out_specs=pl.BlockSpec((t, d), lambda i, j: (i, 0)),
        out_shape=jax.ShapeDtypeStruct((n, d), F32),
        scratch_shapes=[pltpu.VMEM((t, PEER_SLOTS), F32),
                        pltpu.VMEM((t * DENSE_PITCH, PEER_NKEYS), F32),
                        pltpu.VMEM((t, d), F32)],
        compiler_params=_cparams("arbitrary", "arbitrary"),
    )(pre, gate, aidx, bidx, up16, h1, mod3)


def _pad_lanes(v):
    return jnp.pad(v.astype(F32), (0, LANES - v.shape[0])).reshape(1, LANES)


def _layer(h2, mod3, l, batch, seq, norm1_w, w_in, q_norm_w, k_norm_w, rel_bias, lambda_q1, lambda_k1,
           lambda_q2, lambda_k2, subln_w, conv_w, conv_b, dt_bias, a_log, d_skip, ssm_norm_w, w_out,
           norm2_w, peer_wq, peer_keys, expert_down, expert_up):
    d = h2.shape[1]
    lam_init = 0.8 - 0.6 * math.exp(-0.3 * l)
    w16 = w_in.astype(BF16)
    w_main = w16[:, :MAIN_COLS]
    w_dt = jnp.pad(w16[:, MAIN_COLS:], ((0, 0), (0, LANES - SSM_HEADS)))
    qn = jnp.tile(q_norm_w.astype(F32) * (HEAD_DIM ** -0.5 * LOG2E), 2).reshape(1, LANES)
    kn = jnp.tile(k_norm_w.astype(F32), 2).reshape(1, LANES)
    proj, dt_raw = _in_proj(h2, norm1_w.reshape(1, d), mod3, w_main, w_dt, qn, kn, seq)

    lamv = jnp.pad(jnp.stack([lambda_q1, lambda_k1, lambda_q2, lambda_k2]).astype(F32),
                   ((0, 4), (0, LANES - HEAD_DIM)))
    att = _attention(rel_bias.astype(F32).reshape(-1), proj, lamv, subln_w.reshape(1, LANES),
                     batch, seq, lam_init)
    ssm = _ssd(proj, dt_raw, conv_w, conv_b.reshape(1, -1), _pad_lanes(dt_bias), _pad_lanes(a_log),
               _pad_lanes(d_skip), ssm_norm_w.reshape(1, -1), batch, seq)
    h1, hn2 = _out_proj(h2, att, ssm, w_out.astype(BF16), mod3, norm2_w.reshape(1, d), seq)

    keys = peer_keys.astype(BF16).reshape(2 * PEER_HEADS, PEER_NKEYS, -1)
    aidx, bidx, gate = _route(hn2, peer_wq.astype(BF16), keys)
    pre = _peer_down(hn2, expert_down.astype(BF16), aidx, bidx)
    return _peer_up(pre, gate, aidx, bidx, expert_up.astype(BF16), h1, mod3, seq)


def kernel(x, c, ada_w, ada_b, norm1_w, w_in, q_norm_w, k_norm_w, rel_bias, lambda_q1, lambda_k1, lambda_q2, lambda_k2, subln_w, conv_w, conv_b, dt_bias, a_log, d_skip, ssm_norm_w, w_out, norm2_w, peer_wq, peer_keys, expert_down, expert_up):
    batch, seq, d = x.shape
    depth = ada_w.shape[0]
    h2 = x.reshape(batch * seq, d)
    c_pad = jnp.pad(c, ((0, 8 - batch), (0, 0)))
    for l in range(depth):
        mod = _ada(c_pad, ada_w[l], ada_b[l].reshape(1, -1))
        mod3 = mod[:batch].reshape(batch, 6, d)
        h2 = _layer(h2, mod3, l, batch, seq, norm1_w[l], w_in[l], q_norm_w[l], k_norm_w[l], rel_bias,
                    lambda_q1[l], lambda_k1[l], lambda_q2[l], lambda_k2[l], subln_w[l], conv_w[l],
                    conv_b[l], dt_bias[l], a_log[l], d_skip[l], ssm_norm_w[l], w_out[l], norm2_w[l],
                    peer_wq[l], peer_keys[l], expert_down[l], expert_up[l])
    return h2.reshape(batch, seq, d)
```

```python
import functools
import math

import jax
import jax.numpy as jnp
from jax import lax
from jax.experimental import pallas as pl
from jax.experimental.pallas import tpu as pltpu

F32 = jnp.float32
BF16 = jnp.bfloat16
I32 = jnp.int32

LANES = 128
VMEM_LIMIT = 56 * 1024 * 1024

NORM_EPS = 1e-6
HEAD_DIM = 64
ATT_HEADS = 8
ATT_WIDTH = 1024
SSM_WIDTH = 1024
SSM_HEADS = 16
SSM_GROUPS = 2
SSM_STATE = 128
SSM_CONV = 4
SSM_CHUNK = 128
SSM_BC = 2 * SSM_GROUPS * SSM_STATE
REL_BUCKETS = 32
REL_MAX_DIST = 128
PEER_HEADS = 8
PEER_NKEYS = 128
PEER_TOPK = 16
PEER_SLOTS = PEER_HEADS * PEER_TOPK
MAIN_COLS = 3 * ATT_WIDTH + SSM_WIDTH + SSM_WIDTH + SSM_BC
NEG = -1e30
LOG2E = math.log2(math.e)

NT_DIMS = (((1,), (1,)), ((), ()))


def _cparams(*sem):
    return pltpu.CompilerParams(dimension_semantics=sem, vmem_limit_bytes=VMEM_LIMIT)


def _sigmoid(x):
    return 1.0 / (1.0 + jnp.exp(-x))


def _ada_kernel(c_ref, w_ref, b_ref, o_ref):
    c = c_ref[...]
    sc = (c * _sigmoid(c)).astype(BF16)
    o_ref[...] = jnp.dot(sc, w_ref[...].astype(BF16), preferred_element_type=F32) + b_ref[...]


def _ada(c_pad, ada_w, ada_b):
    rows, d = c_pad.shape
    n = ada_w.shape[1]
    tn = 1536
    return pl.pallas_call(
        _ada_kernel,
        grid=(n // tn,),
        in_specs=[pl.BlockSpec((rows, d), lambda j: (0, 0)),
                  pl.BlockSpec((d, tn), lambda j: (0, j)),
                  pl.BlockSpec((1, tn), lambda j: (0, j))],
        out_specs=pl.BlockSpec((rows, tn), lambda j: (0, j)),
        out_shape=jax.ShapeDtypeStruct((rows, n), F32),
        compiler_params=_cparams("arbitrary"),
    )(c_pad, ada_w, ada_b)


def _group_rms(blk, w_row, lo):
    sq = blk * blk
    s_all = jnp.sum(sq, axis=1, keepdims=True)
    s_lo = jnp.sum(jnp.where(lo, sq, 0.0), axis=1, keepdims=True)
    s = jnp.where(lo, s_lo, s_all - s_lo)
    return blk * lax.rsqrt(s * (1.0 / HEAD_DIM) + NORM_EPS) * w_row


def _in_proj_kernel(x_ref, nw_ref, mod_ref, w_ref, wdt_ref, qn_ref, kn_ref,
                    o_ref, dt_ref, hn_scr, *, tn):
    j = pl.program_id(1)

    @pl.when(j == 0)
    def _():
        x = x_ref[...]
        y = x * lax.rsqrt(jnp.mean(x * x, axis=1, keepdims=True) + NORM_EPS) * nw_ref[...]
        mod = mod_ref[0]
        hn = (y * (1.0 + mod[1:2, :]) + mod[0:1, :]).astype(BF16)
        hn_scr[...] = hn
        dt_ref[...] = jnp.dot(hn, wdt_ref[...], preferred_element_type=F32)

    acc = jnp.dot(hn_scr[...], w_ref[...], preferred_element_type=F32)
    qk_tiles = 2 * ATT_WIDTH // tn

    @pl.when(j < qk_tiles)
    def _():
        lo = lax.broadcasted_iota(I32, (1, LANES), 1) < HEAD_DIM
        w_row = jnp.where(j < qk_tiles // 2, qn_ref[...], kn_ref[...])
        for cb in range(tn // LANES):
            blk = acc[:, cb * LANES:(cb + 1) * LANES]
            o_ref[:, cb * LANES:(cb + 1) * LANES] = _group_rms(blk, w_row, lo).astype(BF16)

    @pl.when(j >= qk_tiles)
    def _():
        o_ref[...] = acc.astype(BF16)


def _in_proj(x2, norm_w, mod3, w_main, w_dt, qn, kn, seq):
    n, d = x2.shape
    tm = min(1024, seq)
    tn = 512
    return pl.pallas_call(
        functools.partial(_in_proj_kernel, tn=tn),
        grid=(n // tm, MAIN_COLS // tn),
        in_specs=[pl.BlockSpec((tm, d), lambda i, j: (i, 0)),
                  pl.BlockSpec((1, d), lambda i, j: (0, 0)),
                  pl.BlockSpec((1, 6, d), lambda i, j: (i * tm // seq, 0, 0)),
                  pl.BlockSpec((d, tn), lambda i, j: (0, j)),
                  pl.BlockSpec((d, LANES), lambda i, j: (0, 0)),
                  pl.BlockSpec((1, LANES), lambda i, j: (0, 0)),
                  pl.BlockSpec((1, LANES), lambda i, j: (0, 0))],
        out_specs=[pl.BlockSpec((tm, tn), lambda i, j: (i, j)),
                   pl.BlockSpec((tm, LANES), lambda i, j: (i, 0))],
        out_shape=[jax.ShapeDtypeStruct((n, MAIN_COLS), BF16),
                   jax.ShapeDtypeStruct((n, LANES), F32)],
        scratch_shapes=[pltpu.VMEM((tm, d), BF16)],
        compiler_params=_cparams("arbitrary", "arbitrary"),
    )(x2, norm_w, mod3, w_main, w_dt, qn, kn)


ATT_GROUP = 2
ATT_UNROLL = 4


def _attn_kernel(relb_ref, q_ref, k_ref, vt_ref, lamv_ref, subw_ref, o_ref,
                 q2t_scr, acc_scr, bias_scr, s_scr, *, t, lam_init):
    hp = pl.program_id(1)
    qi = pl.program_id(2)
    nchunk = vt_ref.shape[0] // ATT_GROUP

    @pl.when(qi == 0)
    def _():
        kk = lax.broadcasted_iota(I32, (t, t), 0)
        qq = lax.broadcasted_iota(I32, (t, t), 1)
        max_exact = REL_BUCKETS // 2
        for tile, off in ((0, 0), (1, t)):
            rel = qq - kk + off
            nn = jnp.maximum(rel, 0)
            nf = jnp.maximum(nn, 1).astype(F32)
            large = max_exact + (jnp.log(nf / max_exact) / math.log(REL_MAX_DIST / max_exact)
                                 * (REL_BUCKETS - max_exact)).astype(I32)
            large = jnp.minimum(large, REL_BUCKETS - 1)
            bucket = jnp.where(nn < max_exact, nn, large)
            for hh in range(ATT_GROUP):
                head = hp * ATT_GROUP + hh
                for m in range(2):
                    far = relb_ref[(REL_BUCKETS - 1) * 2 * ATT_HEADS + head * 2 + m]
                    val = jnp.zeros((t, t), F32)
                    for b in range(REL_BUCKETS - 1):
                        delta = (relb_ref[b * 2 * ATT_HEADS + head * 2 + m] - far) * LOG2E
                        val = jnp.where(bucket == b, delta, val)
                    bias_scr[hh, tile, :, m * t:(m + 1) * t] = jnp.where(rel >= 0, val, NEG)

    d_lo = lax.broadcasted_iota(I32, (LANES, t), 0) < HEAD_DIM
    for hh in range(ATT_GROUP):
        qt = q_ref[:, hh * LANES:(hh + 1) * LANES].astype(F32).T
        q2t_scr[hh, :, 0:t] = jnp.where(d_lo, qt, 0.0).astype(BF16)
        q2t_scr[hh, :, t:2 * t] = jnp.where(d_lo, 0.0, qt).astype(BF16)
        acc_scr[hh] = jnp.zeros((LANES, 2 * t), F32)

    def scores(c, hh):
        k_c = k_ref[pl.ds(pl.multiple_of(c * t, t), t), hh * LANES:(hh + 1) * LANES]
        return jnp.dot(k_c, q2t_scr[hh], preferred_element_type=F32)

    def issue(c, slot):
        for hh in range(ATT_GROUP):
            s_scr[slot, hh] = scores(c, hh)

    def consume(c, carry, slot, tile):
        out = []
        for hh in range(ATT_GROUP):
            m_prev, l_prev = carry[hh]
            s = s_scr[slot, hh]
            if tile is not None:
                s = s + bias_scr[hh, tile]
            m_new = jnp.maximum(m_prev, jnp.max(s, axis=0, keepdims=True))
            alpha = jnp.exp2(m_prev - m_new)
            p = jnp.exp2(s - m_new)
            l_new = alpha * l_prev + jnp.sum(p, axis=0, keepdims=True)
            pv = jnp.dot(vt_ref[hh * nchunk + c], p.astype(BF16), preferred_element_type=F32)
            acc_scr[hh] = alpha * acc_scr[hh] + pv
            out.append((m_new, l_new))
        return tuple(out)

    def run(first, tiles, more, carry):
        for k, tile in enumerate(tiles):
            if k + 1 < len(tiles) or more:
                issue(first + k + 1, (k + 1) % 2)
            carry = consume(first + k, carry, k % 2, tile)
        return carry

    nfar = jnp.maximum(qi - 1, 0)
    ntrip = nfar // ATT_UNROLL
    issue(0, 0)

    def trip(i, carry):
        return run(i * ATT_UNROLL, [None] * ATT_UNROLL, True, carry)

    init = tuple((jnp.full((1, 2 * t), NEG, F32), jnp.zeros((1, 2 * t), F32)) for _ in range(ATT_GROUP))
    carry = lax.fori_loop(0, ntrip, trip, init)
    rest = ntrip * ATT_UNROLL
    tails = [functools.partial(run, rest, [None] * r + [1, 0], False) for r in range(ATT_UNROLL)]
    tails.append(functools.partial(run, rest, [0], False))
    carry = lax.switch(jnp.where(qi == 0, ATT_UNROLL, nfar - rest), tails, carry)

    lv = lamv_ref[...]
    lam = (jnp.exp(jnp.sum(lv[0:1, :] * lv[1:2, :], axis=1, keepdims=True))
           - jnp.exp(jnp.sum(lv[2:3, :] * lv[3:4, :], axis=1, keepdims=True)) + lam_init)
    for hh in range(ATT_GROUP):
        inv_l = 1.0 / carry[hh][1]
        acc = acc_scr[hh] * inv_l
        o = (acc[:, 0:t] - lam * acc[:, t:2 * t]).T
        o = o * lax.rsqrt(jnp.mean(o * o, axis=1, keepdims=True) + NORM_EPS) * subw_ref[...]
        o_ref[:, hh * LANES:(hh + 1) * LANES] = (o * (1.0 - lam_init)).astype(BF16)


def _attention(relb, proj, lamv, subw, batch, seq, lam_init):
    n = batch * seq
    t = min(256, seq)
    nq = seq // t
    gw = ATT_GROUP * LANES
    ngroups = ATT_HEADS // ATT_GROUP
    vt = proj[:, 2 * ATT_WIDTH:3 * ATT_WIDTH].reshape(batch, nq, t, ATT_HEADS, LANES)
    vt = vt.transpose(0, 3, 1, 4, 2).reshape(batch * ATT_HEADS * nq, LANES, t)
    return pl.pallas_call(
        functools.partial(_attn_kernel, t=t, lam_init=lam_init),
        grid=(batch, ngroups, nq),
        in_specs=[pl.BlockSpec(memory_space=pltpu.SMEM),
                  pl.BlockSpec((t, gw), lambda b, g, i: (b * nq + i, g)),
                  pl.BlockSpec((seq, gw), lambda b, g, i: (b, ngroups + g)),
                  pl.BlockSpec((ATT_GROUP * nq, LANES, t), lambda b, g, i: (b * ngroups + g, 0, 0)),
                  pl.BlockSpec((8, LANES), lambda b, g, i: (0, 0)),
                  pl.BlockSpec((1, LANES), lambda b, g, i: (0, 0))],
        out_specs=pl.BlockSpec((t, gw), lambda b, g, i: (b * nq + i, g)),
        out_shape=jax.ShapeDtypeStruct((n, ATT_WIDTH), BF16),
        scratch_shapes=[pltpu.VMEM((ATT_GROUP, LANES, 2 * t), BF16),
                        pltpu.VMEM((ATT_GROUP, LANES, 2 * t), F32),
                        pltpu.VMEM((ATT_GROUP, 2, t, 2 * t), F32),
                        pltpu.VMEM((2, ATT_GROUP, t, 2 * t), F32)],
        compiler_params=_cparams("arbitrary", "arbitrary", "arbitrary"),
    )(relb, proj, proj, vt, lamv, subw)


def _split3(x):
    hi = x.astype(BF16)
    r1 = x - hi.astype(F32)
    mid = r1.astype(BF16)
    lo = (r1 - mid.astype(F32)).astype(BF16)
    return hi, mid, lo


def _ssd_kernel(z_ref, xs_ref, bc_ref, dt_ref, cw_ref, cb_ref, dtb_ref, alog_ref, dskip_ref, nw_ref,
                o_ref, xpad_scr, state_scr):
    L = SSM_CHUNK
    W = SSM_WIDTH
    P2 = LANES
    nblk = W // P2
    gw = W // SSM_GROUPS

    @pl.when(pl.program_id(1) == 0)
    def _():
        xpad_scr[0:8, :] = jnp.zeros((8, W + SSM_BC), F32)
        state_scr[...] = jnp.zeros((SSM_STATE, W), F32)

    xpad_scr[8:8 + L, 0:W] = xs_ref[...].astype(F32)
    xpad_scr[8:8 + L, W:W + SSM_BC] = bc_ref[...].astype(F32)
    conv = cb_ref[...] + cw_ref[0:1, :] * xpad_scr[5:5 + L, :]
    for kk in range(1, SSM_CONV):
        conv = conv + cw_ref[kk:kk + 1, :] * xpad_scr[5 + kk:5 + kk + L, :]
    xpad_scr[0:8, :] = xpad_scr[L:L + 8, :]
    u = conv * _sigmoid(conv)

    dtr = dt_ref[...] + dtb_ref[...]
    dt = jnp.maximum(dtr, 0.0) + jnp.log1p(jnp.exp(-jnp.abs(dtr)))
    a = -jnp.exp(alog_ref[...])
    da = dt * a

    ri = lax.broadcasted_iota(I32, (L, L), 0)
    ci = lax.broadcasted_iota(I32, (L, L), 1)
    tril = ri >= ci
    tri = jnp.where(tril, 1.0, 0.0).astype(BF16)
    hi, mid, lo3 = _split3(da)
    a_cs = (jnp.dot(tri, hi, preferred_element_type=F32) + jnp.dot(tri, mid, preferred_element_type=F32)
            + jnp.dot(tri, lo3, preferred_element_type=F32))
    a_cs_t = a_cs.T
    a_last = a_cs[L - 1:L, :]
    e_cs = jnp.exp(a_cs)
    dt_ds = dt * jnp.exp(a_last - a_cs)

    lane_lo = lax.broadcasted_iota(I32, (1, P2), 1) < HEAD_DIM

    def expand(mat, i):
        return jnp.where(lane_lo, mat[:, 2 * i:2 * i + 1], mat[:, 2 * i + 1:2 * i + 2])

    y_blocks = []
    for g in range(SSM_GROUPS):
        bm = u[:, W + g * SSM_STATE:W + (g + 1) * SSM_STATE]
        cm = u[:, W + (SSM_GROUPS + g) * SSM_STATE:W + (SSM_GROUPS + g + 1) * SSM_STATE]
        bm16 = bm.astype(BF16)
        cm16 = cm.astype(BF16)
        cb = lax.dot_general(cm16, bm16, NT_DIMS, preferred_element_type=F32)
        st_g = state_scr[:, g * gw:(g + 1) * gw]
        y_off = jnp.dot(cm16, st_g.astype(BF16), preferred_element_type=F32)
        xd_blocks = []
        for ib in range(nblk // SSM_GROUPS):
            i = g * (nblk // SSM_GROUPS) + ib
            xs_blk = u[:, i * P2:(i + 1) * P2]
            xc = xs_blk * expand(dt, i)
            yd = jnp.zeros((L, P2), F32)
            for hh in range(2):
                head = 2 * i + hh
                seg = a_cs[:, head:head + 1] - a_cs_t[head:head + 1, :]
                wmat = (cb * jnp.where(tril, jnp.exp(seg), 0.0)).astype(BF16)
                keep = lane_lo if hh == 0 else jnp.logical_not(lane_lo)
                yd = yd + jnp.dot(wmat, jnp.where(keep, xc, 0.0).astype(BF16), preferred_element_type=F32)
            y = yd + y_off[:, ib * P2:(ib + 1) * P2] * expand(e_cs, i) + expand(dskip_ref[...], i) * xs_blk
            zf = z_ref[:, i * P2:(i + 1) * P2].astype(F32)
            y_blocks.append(y * (zf * _sigmoid(zf)))
            xd_blocks.append((xs_blk * expand(dt_ds, i)).astype(BF16))
        xd = jnp.concatenate(xd_blocks, axis=1)
        st_new = jnp.dot(bm.T.astype(BF16), xd, preferred_element_type=F32)
        decay = jnp.concatenate([expand(jnp.exp(a_last), g * (nblk // SSM_GROUPS) + ib)
                                 for ib in range(nblk // SSM_GROUPS)], axis=1)
        state_scr[:, g * gw:(g + 1) * gw] = st_g * decay + st_new

    per_g = nblk // SSM_GROUPS
    for g in range(SSM_GROUPS):
        blks = y_blocks[g * per_g:(g + 1) * per_g]
        ss = sum(jnp.sum(b * b, axis=1, keepdims=True) for b in blks) * (1.0 / gw)
        inv = lax.rsqrt(ss + NORM_EPS)
        for ib, b in enumerate(blks):
            i = g * per_g + ib
            o_ref[:, i * P2:(i + 1) * P2] = (b * inv * nw_ref[:, i * P2:(i + 1) * P2]).astype(BF16)


def _ssd(proj, dt_raw, conv_w, conv_b, dt_bias, a_log, d_skip, norm_w, batch, seq):
    n = batch * seq
    L = SSM_CHUNK
    nc = seq // L
    cd = SSM_WIDTH + SSM_BC
    row = lambda b, c: (b * nc + c)
    return pl.pallas_call(
        _ssd_kernel,
        grid=(batch, nc),
        in_specs=[pl.BlockSpec((L, SSM_WIDTH), lambda b, c: (row(b, c), 3)),
                  pl.BlockSpec((L, SSM_WIDTH), lambda b, c: (row(b, c), 4)),
                  pl.BlockSpec((L, SSM_BC), lambda b, c: (row(b, c), 10)),
                  pl.BlockSpec((L, LANES), lambda b, c: (row(b, c), 0)),
                  pl.BlockSpec((SSM_CONV, cd), lambda b, c: (0, 0)),
                  pl.BlockSpec((1, cd), lambda b, c: (0, 0)),
                  pl.BlockSpec((1, LANES), lambda b, c: (0, 0)),
                  pl.BlockSpec((1, LANES), lambda b, c: (0, 0)),
                  pl.BlockSpec((1, LANES), lambda b, c: (0, 0)),
                  pl.BlockSpec((1, SSM_WIDTH), lambda b, c: (0, 0))],
        out_specs=pl.BlockSpec((L, SSM_WIDTH), lambda b, c: (row(b, c), 0)),
        out_shape=jax.ShapeDtypeStruct((n, SSM_WIDTH), BF16),
        scratch_shapes=[pltpu.VMEM((L + 8, cd), F32),
                        pltpu.VMEM((SSM_STATE, SSM_WIDTH), F32)],
        compiler_params=_cparams("arbitrary", "arbitrary"),
    )(proj, proj, proj, dt_raw, conv_w, conv_b, dt_bias, a_log, d_skip, norm_w)


def _out_proj_kernel(x_ref, att_ref, ssm_ref, w_ref, mod_ref, nw_ref, h_ref, hn_ref):
    mix = (jnp.dot(att_ref[...], w_ref[0:ATT_WIDTH, :], preferred_element_type=F32)
           + jnp.dot(ssm_ref[...], w_ref[ATT_WIDTH:, :], preferred_element_type=F32))
    mod = mod_ref[0]
    h1 = x_ref[...] + mod[2:3, :] * mix
    h_ref[...] = h1
    y = h1 * lax.rsqrt(jnp.mean(h1 * h1, axis=1, keepdims=True) + NORM_EPS) * nw_ref[...]
    hn_ref[...] = (y * (1.0 + mod[4:5, :]) + mod[3:4, :]).astype(BF16)


def _out_proj(x2, att, ssm, w_out, mod3, norm2_w, seq):
    n, d = x2.shape
    tm = min(256, seq)
    return pl.pallas_call(
        _out_proj_kernel,
        grid=(n // tm,),
        in_specs=[pl.BlockSpec((tm, d), lambda i: (i, 0)),
                  pl.BlockSpec((tm, ATT_WIDTH), lambda i: (i, 0)),
                  pl.BlockSpec((tm, SSM_WIDTH), lambda i: (i, 0)),
                  pl.BlockSpec((ATT_WIDTH + SSM_WIDTH, d), lambda i: (0, 0)),
                  pl.BlockSpec((1, 6, d), lambda i: (i * tm // seq, 0, 0)),
                  pl.BlockSpec((1, d), lambda i: (0, 0))],
        out_specs=[pl.BlockSpec((tm, d), lambda i: (i, 0)),
                   pl.BlockSpec((tm, d), lambda i: (i, 0))],
        out_shape=[jax.ShapeDtypeStruct((n, d), F32),
                   jax.ShapeDtypeStruct((n, d), BF16)],
        compiler_params=_cparams("arbitrary"),
    )(x2, att, ssm, w_out, mod3, norm2_w)


def _topk_rows(s, k, rows):
    row = lax.broadcasted_iota(I32, s.shape, 0).astype(F32)
    vals, idxs = [], []
    for _ in range(k):
        m = jnp.max(s, axis=0, keepdims=True)
        idx = jnp.min(jnp.where(s == m, row, float(rows)), axis=0, keepdims=True)
        s = jnp.where(row == idx, -jnp.inf, s)
        vals.append(m)
        idxs.append(idx)
    return vals, idxs


_CAND = [(i, j) for i in range(PEER_TOPK) for j in range(PEER_TOPK) if (i + 1) * (j + 1) <= PEER_TOPK]


def _route_kernel(hn_ref, wq_ref, keys_ref, a_ref, b_ref, g_ref, top_scr, code_scr):
    t = hn_ref.shape[0]
    qp = jnp.dot(hn_ref[...], wq_ref[...], preferred_element_type=F32).astype(BF16)
    ncand = len(_CAND)
    pad = (-ncand) % 8
    for h in range(PEER_HEADS):
        sub = []
        for c in range(2):
            hc = 2 * h + c
            sc = lax.dot_general(keys_ref[hc], qp[:, hc * LANES:(hc + 1) * LANES], NT_DIMS,
                                 preferred_element_type=F32)
            sub.append(_topk_rows(sc, PEER_TOPK, PEER_NKEYS))
        (s1, i1), (s2, i2) = sub
        cand = jnp.concatenate([s1[i] + s2[j] for i, j in _CAND]
                               + [jnp.full((pad, t), -jnp.inf, F32)], axis=0)
        a_hi = [v * float(PEER_NKEYS) for v in i1]
        code = jnp.concatenate([a_hi[i] + i2[j] for i, j in _CAND] + [jnp.zeros((pad, t), F32)], axis=0)
        row = lax.broadcasted_iota(I32, cand.shape, 0).astype(F32)
        for kk in range(PEER_TOPK):
            m = jnp.max(cand, axis=0, keepdims=True)
            idx = jnp.min(jnp.where(cand == m, row, float(ncand + pad)), axis=0, keepdims=True)
            sel = row == idx
            slot = h * PEER_TOPK + kk
            top_scr[slot:slot + 1, :] = m
            code_scr[slot:slot + 1, :] = jnp.max(jnp.where(sel, code, -1.0), axis=0, keepdims=True)
            cand = jnp.where(sel, -jnp.inf, cand)
        top = top_scr[h * PEER_TOPK:(h + 1) * PEER_TOPK, :]
        e = jnp.exp(top - jnp.max(top, axis=0, keepdims=True))
        top_scr[h * PEER_TOPK:(h + 1) * PEER_TOPK, :] = e / jnp.sum(e, axis=0, keepdims=True)
    code_t = code_scr[...].T
    first = jnp.floor(code_t * (1.0 / PEER_NKEYS))
    a_ref[...] = first.astype(I32)
    b_ref[...] = (code_t - first * float(PEER_NKEYS)).astype(I32)
    g_ref[...] = top_scr[...].T


def _route(hn2, wq, keys):
    n, d = hn2.shape
    t = min(256, n)
    qd = wq.shape[1]
    out = jax.ShapeDtypeStruct((n, PEER_SLOTS), I32)
    return pl.pallas_call(
        _route_kernel,
        grid=(n // t,),
        in_specs=[pl.BlockSpec((t, d), lambda i: (i, 0)),
                  pl.BlockSpec((d, qd), lambda i: (0, 0)),
                  pl.BlockSpec(keys.shape, lambda i: (0, 0, 0))],
        out_specs=[pl.BlockSpec((t, PEER_SLOTS), lambda i: (i, 0))] * 3,
        out_shape=[out, out, jax.ShapeDtypeStruct((n, PEER_SLOTS), F32)],
        scratch_shapes=[pltpu.VMEM((PEER_SLOTS, t), F32),
                        pltpu.VMEM((PEER_SLOTS, t), F32)],
        compiler_params=_cparams("arbitrary"),
    )(hn2, wq, keys)


PAIR = 2 * PEER_NKEYS
DOWN_PAIRS = 8
UP_KEYS = 16
TOKEN_UNROLL = 16
DENSE_PITCH = PEER_NKEYS + 8


def _peer_down_kernel(x_ref, dn_ref, a_ref, b_ref, pre_ref):
    j = pl.program_id(1)

    @pl.when(j == 0)
    def _():
        pre_ref[...] = jnp.zeros(pre_ref.shape, F32)

    x = x_ref[...]
    a = a_ref[...]
    b = b_ref[...]
    pre = pre_ref[...]
    for q in range(DOWN_PAIRS):
        p = lax.dot_general(x, dn_ref[q * PAIR:(q + 1) * PAIR, :], NT_DIMS,
                            preferred_element_type=F32)
        for half in range(2):
            g = jnp.take_along_axis(p[:, half * LANES:(half + 1) * LANES], b, axis=1)
            pre = jnp.where(a == 2 * (DOWN_PAIRS * j + q) + half, g, pre)
    pre_ref[...] = pre


def _peer_down(hn2, down16, aidx, bidx):
    n, d = hn2.shape
    t = min(1024, n)
    slot_spec = pl.BlockSpec((t, PEER_SLOTS), lambda i, j: (i, 0))
    return pl.pallas_call(
        _peer_down_kernel,
        grid=(n // t, down16.shape[0] // (DOWN_PAIRS * PAIR)),
        in_specs=[pl.BlockSpec((t, d), lambda i, j: (i, 0)),
                  pl.BlockSpec((DOWN_PAIRS * PAIR, d), lambda i, j: (j, 0)),
                  slot_spec, slot_spec],
        out_specs=slot_spec,
        out_shape=jax.ShapeDtypeStruct((n, PEER_SLOTS), F32),
        compiler_params=_cparams("arbitrary", "arbitrary"),
    )(hn2, down16, aidx, bidx)


def _peer_up_kernel(pre_ref, g_ref, a_ref, b_ref, up_ref, h_ref, mod_ref, o_ref,
                    act_scr, dense_scr, acc_scr):
    j = pl.program_id(1)
    t = pre_ref.shape[0]
    nk = PEER_NKEYS

    @pl.when(j == 0)
    def _():
        pre = pre_ref[...]
        act_scr[...] = 0.5 * pre * (1.0 + lax.erf(pre * (1.0 / math.sqrt(2.0)))) * g_ref[...]
        acc_scr[...] = jnp.zeros(acc_scr.shape, F32)
        row = lax.broadcasted_iota(I32, (nk, PEER_SLOTS), 0)

        def body(i, carry):
            for grp in range(TOKEN_UNROLL // 8):
                base = pl.multiple_of(i * TOKEN_UNROLL + grp * 8, 8)
                a8 = a_ref[pl.ds(base, 8), :]
                b8 = b_ref[pl.ds(base, 8), :]
                c8 = act_scr[pl.ds(base, 8), :]
                for u in range(8):
                    xa = jnp.where(row == a8[u:u + 1, :], c8[u:u + 1, :], 0.0).astype(BF16)
                    yb = jnp.where(row == b8[u:u + 1, :], 1.0, 0.0).astype(BF16)
                    dense_scr[pl.ds(pl.multiple_of((base + u) * DENSE_PITCH, 8), nk), :] = lax.dot_general(
                        xa, yb, NT_DIMS, preferred_element_type=F32)
            return carry

        lax.fori_loop(0, t // TOKEN_UNROLL, body, 0)

    lhs = jnp.concatenate([dense_scr[pl.ds(UP_KEYS * j + u, t, stride=DENSE_PITCH), :]
                           for u in range(UP_KEYS)], axis=1).astype(BF16)
    acc_scr[...] += jnp.dot(lhs, up_ref[...], preferred_element_type=F32)

    @pl.when(j == pl.num_programs(1) - 1)
    def _():
        o_ref[...] = h_ref[...] + mod_ref[0][5:6, :] * acc_scr[...]


def _peer_up(pre, gate, aidx, bidx, up16, h1, mod3, seq):
    n, d = h1.shape
    t = min(256, seq)
    slot_spec = pl.BlockSpec((t, PEER_SLOTS), lambda i, j: (i, 0))
    return pl.pallas_call(
        _peer_up_kernel,
        grid=(n // t, up16.shape[0] // (UP_KEYS * PEER_NKEYS)),
        in_specs=[slot_spec, slot_spec, slot_spec, slot_spec,
                  pl.BlockSpec((UP_KEYS * PEER_NKEYS, d), lambda i, j: (j, 0)),
                  pl.BlockSpec((t, d), lambda i, j: (i, 0)),
                  pl.BlockSpec((1, 6, d), lambda i, j: (i * t // seq, 0, 0))],
        out_specs=pl.BlockSpec((t, d), lambda i, j: (i, 0)),
        out_shape=jax.ShapeDtypeStruct((n, d), F32),
        scratch_shapes=[pltpu.VMEM((t, PEER_SLOTS), F32),
                        pltpu.VMEM((t * DENSE_PITCH, PEER_NKEYS), F32),
                        pltpu.VMEM((t, d), F32)],
        compiler_params=_cparams("arbitrary", "arbitrary"),
    )(pre, gate, aidx, bidx, up16, h1, mod3)


def _pad_lanes(v):
    return jnp.pad(v.astype(F32), (0, LANES - v.shape[0])).reshape(1, LANES)


def _layer(h2, mod3, l, batch, seq, norm1_w, w_in, q_norm_w, k_norm_w, rel_bias, lambda_q1, lambda_k1,
           lambda_q2, lambda_k2, subln_w, conv_w, conv_b, dt_bias, a_log, d_skip, ssm_norm_w, w_out,
           norm2_w, peer_wq, peer_keys, expert_down, expert_up):
    d = h2.shape[1]
    lam_init = 0.8 - 0.6 * math.exp(-0.3 * l)
    w16 = w_in.astype(BF16)
    w_main = w16[:, :MAIN_COLS]
    w_dt = jnp.pad(w16[:, MAIN_COLS:], ((0, 0), (0, LANES - SSM_HEADS)))
    qn = jnp.tile(q_norm_w.astype(F32) * (HEAD_DIM ** -0.5 * LOG2E), 2).reshape(1, LANES)
    kn = jnp.tile(k_norm_w.astype(F32), 2).reshape(1, LANES)
    proj, dt_raw = _in_proj(h2, norm1_w.reshape(1, d), mod3, w_main, w_dt, qn, kn, seq)

    lamv = jnp.pad(jnp.stack([lambda_q1, lambda_k1, lambda_q2, lambda_k2]).astype(F32),
                   ((0, 4), (0, LANES - HEAD_DIM)))
    att = _attention(rel_bias.astype(F32).reshape(-1), proj, lamv, subln_w.reshape(1, LANES),
                     batch, seq, lam_init)
    ssm = _ssd(proj, dt_raw, conv_w, conv_b.reshape(1, -1), _pad_lanes(dt_bias), _pad_lanes(a_log),
               _pad_lanes(d_skip), ssm_norm_w.reshape(1, -1), batch, seq)
    h1, hn2 = _out_proj(h2, att, ssm, w_out.astype(BF16), mod3, norm2_w.reshape(1, d), seq)

    keys = peer_keys.astype(BF16).reshape(2 * PEER_HEADS, PEER_NKEYS, -1)
    aidx, bidx, gate = _route(hn2, peer_wq.astype(BF16), keys)
    pre = _peer_down(hn2, expert_down.astype(BF16), aidx, bidx)
    return _peer_up(pre, gate, aidx, bidx, expert_up.astype(BF16), h1, mod3, seq)


def kernel(x, c, ada_w, ada_b, norm1_w, w_in, q_norm_w, k_norm_w, rel_bias, lambda_q1, lambda_k1, lambda_q2, lambda_k2, subln_w, conv_w, conv_b, dt_bias, a_log, d_skip, ssm_norm_w, w_out, norm2_w, peer_wq, peer_keys, expert_down, expert_up):
    batch, seq, d = x.shape
    depth = ada_w.shape[0]
    h2 = x.reshape(batch * seq, d)
    c_pad = jnp.pad(c, ((0, 8 - batch), (0, 0)))
    for l in range(depth):
        mod = _ada(c_pad, ada_w[l], ada_b[l].reshape(1, -1))
        mod3 = mod[:batch].reshape(batch, 6, d)
        h2 = _layer(h2, mod3, l, batch, seq, norm1_w[l], w_in[l], q_norm_w[l], k_norm_w[l], rel_bias,
                    lambda_q1[l], lambda_k1[l], lambda_q2[l], lambda_k2[l], subln_w[l], conv_w[l],
                    conv_b[l], dt_bias[l], a_log[l], d_skip[l], ssm_norm_w[l], w_out[l], norm2_w[l],
                    peer_wq[l], peer_keys[l], expert_down[l], expert_up[l])
    return h2.reshape(batch, seq, d)
```

```python
import functools
import math

import jax
import jax.numpy as jnp
from jax import lax
from jax.experimental import pallas as pl
from jax.experimental.pallas import tpu as pltpu

F32 = jnp.float32
BF16 = jnp.bfloat16
I32 = jnp.int32

LANES = 128
VMEM_LIMIT = 56 * 1024 * 1024

NORM_EPS = 1e-6
HEAD_DIM = 64
ATT_HEADS = 8
ATT_WIDTH = 1024
SSM_WIDTH = 1024
SSM_HEADS = 16
SSM_GROUPS = 2
SSM_STATE = 128
SSM_CONV = 4
SSM_CHUNK = 128
SSM_BC = 2 * SSM_GROUPS * SSM_STATE
REL_BUCKETS = 32
REL_MAX_DIST = 128
PEER_HEADS = 8
PEER_NKEYS = 128
PEER_TOPK = 16
PEER_SLOTS = PEER_HEADS * PEER_TOPK
MAIN_COLS = 3 * ATT_WIDTH + SSM_WIDTH + SSM_WIDTH + SSM_BC
NEG = -1e30
LOG2E = math.log2(math.e)

NT_DIMS = (((1,), (1,)), ((), ()))


def _cparams(*sem):
    return pltpu.CompilerParams(dimension_semantics=sem, vmem_limit_bytes=VMEM_LIMIT)


def _sigmoid(x):
    return 1.0 / (1.0 + jnp.exp(-x))


def _ada_kernel(c_ref, w_ref, b_ref, o_ref):
    c = c_ref[...]
    sc = (c * _sigmoid(c)).astype(BF16)
    o_ref[...] = jnp.dot(sc, w_ref[...].astype(BF16), preferred_element_type=F32) + b_ref[...]


def _ada(c_pad, ada_w, ada_b):
    rows, d = c_pad.shape
    n = ada_w.shape[1]
    tn = 1536
    return pl.pallas_call(
        _ada_kernel,
        grid=(n // tn,),
        in_specs=[pl.BlockSpec((rows, d), lambda j: (0, 0)),
                  pl.BlockSpec((d, tn), lambda j: (0, j)),
                  pl.BlockSpec((1, tn), lambda j: (0, j))],
        out_specs=pl.BlockSpec((rows, tn), lambda j: (0, j)),
        out_shape=jax.ShapeDtypeStruct((rows, n), F32),
        compiler_params=_cparams("arbitrary"),
    )(c_pad, ada_w, ada_b)


def _group_rms(blk, w_row, lo):
    sq = blk * blk
    s_all = jnp.sum(sq, axis=1, keepdims=True)
    s_lo = jnp.sum(jnp.where(lo, sq, 0.0), axis=1, keepdims=True)
    s = jnp.where(lo, s_lo, s_all - s_lo)
    return blk * lax.rsqrt(s * (1.0 / HEAD_DIM) + NORM_EPS) * w_row


def _in_proj_kernel(x_ref, nw_ref, mod_ref, w_ref, wdt_ref, qn_ref, kn_ref,
                    o_ref, dt_ref, hn_scr, *, tn):
    j = pl.program_id(1)

    @pl.when(j == 0)
    def _():
        x = x_ref[...]
        y = x * lax.rsqrt(jnp.mean(x * x, axis=1, keepdims=True) + NORM_EPS) * nw_ref[...]
        mod = mod_ref[0]
        hn = (y * (1.0 + mod[1:2, :]) + mod[0:1, :]).astype(BF16)
        hn_scr[...] = hn
        dt_ref[...] = jnp.dot(hn, wdt_ref[...], preferred_element_type=F32)

    acc = jnp.dot(hn_scr[...], w_ref[...], preferred_element_type=F32)
    qk_tiles = 2 * ATT_WIDTH // tn

    @pl.when(j < qk_tiles)
    def _():
        lo = lax.broadcasted_iota(I32, (1, LANES), 1) < HEAD_DIM
        w_row = jnp.where(j < qk_tiles // 2, qn_ref[...], kn_ref[...])
        for cb in range(tn // LANES):
            blk = acc[:, cb * LANES:(cb + 1) * LANES]
            o_ref[:, cb * LANES:(cb + 1) * LANES] = _group_rms(blk, w_row, lo).astype(BF16)

    @pl.when(j >= qk_tiles)
    def _():
        o_ref[...] = acc.astype(BF16)


def _in_proj(x2, norm_w, mod3, w_main, w_dt, qn, kn, seq):
    n, d = x2.shape
    tm = min(1024, seq)
    tn = 512
    return pl.pallas_call(
        functools.partial(_in_proj_kernel, tn=tn),
        grid=(n // tm, MAIN_COLS // tn),
        in_specs=[pl.BlockSpec((tm, d), lambda i, j: (i, 0)),
                  pl.BlockSpec((1, d), lambda i, j: (0, 0)),
                  pl.BlockSpec((1, 6, d), lambda i, j: (i * tm // seq, 0, 0)),
                  pl.BlockSpec((d, tn), lambda i, j: (0, j)),
                  pl.BlockSpec((d, LANES), lambda i, j: (0, 0)),
                  pl.BlockSpec((1, LANES), lambda i, j: (0, 0)),
                  pl.BlockSpec((1, LANES), lambda i, j: (0, 0))],
        out_specs=[pl.BlockSpec((tm, tn), lambda i, j: (i, j)),
                   pl.BlockSpec((tm, LANES), lambda i, j: (i, 0))],
        out_shape=[jax.ShapeDtypeStruct((n, MAIN_COLS), BF16),
                   jax.ShapeDtypeStruct((n, LANES), F32)],
        scratch_shapes=[pltpu.VMEM((tm, d), BF16)],
        compiler_params=_cparams("arbitrary", "arbitrary"),
    )(x2, norm_w, mod3, w_main, w_dt, qn, kn)


ATT_GROUP = 2
ATT_UNROLL = 4
ATT_CHUNK = 256
VT_ROWS = LANES + 16


def _attn_kernel(relb_ref, q_ref, k_ref, vt_ref, lamv_ref, subw_ref, o_ref,
                 q2t_scr, acc_scr, bias_scr, s_scr, smax_scr, *, t, lam_init):
    hp = pl.program_id(1)
    qi = pl.program_id(2)
    tq = 2 * t
    nchunk = vt_ref.shape[0] // ATT_GROUP

    @pl.when(qi == 0)
    def _():
        kk = lax.broadcasted_iota(I32, (t, t), 0)
        qq = lax.broadcasted_iota(I32, (t, t), 1)
        max_exact = REL_BUCKETS // 2
        buckets = []
        for off in (0, t):
            nn = jnp.maximum(qq - kk + off, 0)
            nf = jnp.maximum(nn, 1).astype(F32)
            large = max_exact + (jnp.log(nf / max_exact) / math.log(REL_MAX_DIST / max_exact)
                                 * (REL_BUCKETS - max_exact)).astype(I32)
            buckets.append(jnp.where(nn < max_exact, nn, jnp.minimum(large, REL_BUCKETS - 1)))
        zeros = jnp.zeros((t, t), F32)
        masked = jnp.full((t, t), NEG, F32)
        for hh in range(ATT_GROUP):
            head = hp * ATT_GROUP + hh
            for m in range(2):
                far = relb_ref[(REL_BUCKETS - 1) * 2 * ATT_HEADS + head * 2 + m]
                diag, sub = zeros, zeros
                for b in range(REL_BUCKETS - 1):
                    delta = (relb_ref[b * 2 * ATT_HEADS + head * 2 + m] - far) * LOG2E
                    diag = jnp.where(buckets[0] == b, delta, diag)
                    sub = jnp.where(buckets[1] == b, delta, sub)
                diag = jnp.where(qq >= kk, diag, NEG)
                for tile, (first, last) in enumerate(((sub, zeros), (diag, sub), (masked, diag))):
                    bias_scr[hh, tile, :, m * tq:m * tq + t] = first
                    bias_scr[hh, tile, :, m * tq + t:(m + 1) * tq] = last

    d_lo = lax.broadcasted_iota(I32, (LANES, tq), 0) < HEAD_DIM
    for hh in range(ATT_GROUP):
        qt = q_ref[:, hh * LANES:(hh + 1) * LANES].astype(F32).T
        q2t_scr[hh, :, 0:tq] = jnp.where(d_lo, qt, 0.0).astype(BF16)
        q2t_scr[hh, :, tq:2 * tq] = jnp.where(d_lo, 0.0, qt).astype(BF16)
        acc_scr[hh] = jnp.zeros((VT_ROWS, 2 * tq), F32)

    def scores(c, hh):
        k_c = k_ref[pl.ds(pl.multiple_of(c * t, t), t), hh * LANES:(hh + 1) * LANES]
        return jnp.dot(k_c, q2t_scr[hh], preferred_element_type=F32)

    def issue(c, slot, tile):
        for hh in range(ATT_GROUP):
            s = scores(c, hh)
            if tile is not None:
                s = s + bias_scr[hh, tile]
            s_scr[slot, hh] = s
            row = slot * ATT_GROUP + hh
            smax_scr[row:row + 1, :] = jnp.max(s, axis=0, keepdims=True)

    def consume(c, carry, slot):
        out = []
        for hh in range(ATT_GROUP):
            m_prev = carry[hh]
            row = slot * ATT_GROUP + hh
            m_new = jnp.maximum(m_prev, smax_scr[row:row + 1, :])
            alpha = jnp.exp2(m_prev - m_new)
            p = jnp.exp2(s_scr[slot, hh] - m_new).astype(BF16)
            pv = jnp.dot(vt_ref[hh * nchunk + c], p, preferred_element_type=F32)
            acc_scr[hh] = alpha * acc_scr[hh] + pv
            out.append(m_new)
        return tuple(out)

    def run(first, tiles, more, reissue, carry):
        if reissue:
            issue(first, 0, tiles[0])
        for k in range(len(tiles)):
            if k + 1 < len(tiles):
                issue(first + k + 1, (k + 1) % 2, tiles[k + 1])
            elif more:
                issue(first + k + 1, (k + 1) % 2, None)
            carry = consume(first + k, carry, k % 2)
        return carry

    nfar = jnp.maximum(2 * qi - 1, 0)
    ntrip = nfar // ATT_UNROLL
    issue(0, 0, None)

    def trip(i, carry):
        return run(i * ATT_UNROLL, [None] * ATT_UNROLL, True, False, carry)

    init = tuple(jnp.full((1, 2 * tq), NEG, F32) for _ in range(ATT_GROUP))
    carry = lax.fori_loop(0, ntrip, trip, init)
    rest = ntrip * ATT_UNROLL
    tails = [functools.partial(run, rest, [None] * r + [0, 1, 2], False, False)
             for r in range(1, ATT_UNROLL, 2)]
    tails.append(functools.partial(run, rest, [1, 2], False, True))
    carry = lax.switch(jnp.where(qi == 0, len(tails) - 1, (nfar - rest) // 2), tails, carry)

    lv = lamv_ref[...]
    lam = (jnp.exp(jnp.sum(lv[0:1, :] * lv[1:2, :], axis=1, keepdims=True))
           - jnp.exp(jnp.sum(lv[2:3, :] * lv[3:4, :], axis=1, keepdims=True)) + lam_init)
    for hh in range(ATT_GROUP):
        acc = acc_scr[hh, 0:LANES, :] * (1.0 / acc_scr[hh, LANES:LANES + 1, :])
        o = (acc[:, 0:tq] - lam * acc[:, tq:2 * tq]).T
        o = o * lax.rsqrt(jnp.mean(o * o, axis=1, keepdims=True) + NORM_EPS) * subw_ref[...]
        o_ref[:, hh * LANES:(hh + 1) * LANES] = (o * (1.0 - lam_init)).astype(BF16)


def _attention(relb, proj, lamv, subw, batch, seq, lam_init):
    n = batch * seq
    t = ATT_CHUNK
    tq = 2 * t
    nq = seq // tq
    nk = seq // t
    gw = ATT_GROUP * LANES
    ngroups = ATT_HEADS // ATT_GROUP
    vt = proj[:, 2 * ATT_WIDTH:3 * ATT_WIDTH].reshape(batch, nk, t, ATT_HEADS, LANES)
    vt = vt.transpose(0, 3, 1, 4, 2).reshape(batch * ATT_HEADS * nk, LANES, t)
    ones_rows = jnp.zeros((vt.shape[0], VT_ROWS - LANES, t), BF16).at[:, 0, :].set(1.0)
    vt = jnp.concatenate([vt, ones_rows], axis=1)
    return pl.pallas_call(
        functools.partial(_attn_kernel, t=t, lam_init=lam_init),
        grid=(batch, ngroups, nq),
        in_specs=[pl.BlockSpec(memory_space=pltpu.SMEM),
                  pl.BlockSpec((tq, gw), lambda b, g, i: (b * nq + i, g)),
                  pl.BlockSpec((seq, gw), lambda b, g, i: (b, ngroups + g)),
                  pl.BlockSpec((ATT_GROUP * nk, VT_ROWS, t), lambda b, g, i: (b * ngroups + g, 0, 0)),
                  pl.BlockSpec((8, LANES), lambda b, g, i: (0, 0)),
                  pl.BlockSpec((1, LANES), lambda b, g, i: (0, 0))],
        out_specs=pl.BlockSpec((tq, gw), lambda b, g, i: (b * nq + i, g)),
        out_shape=jax.ShapeDtypeStruct((n, ATT_WIDTH), BF16),
        scratch_shapes=[pltpu.VMEM((ATT_GROUP, LANES, 2 * tq), BF16),
                        pltpu.VMEM((ATT_GROUP, VT_ROWS, 2 * tq), F32),
                        pltpu.VMEM((ATT_GROUP, 3, t, 2 * tq), F32),
                        pltpu.VMEM((2, ATT_GROUP, t, 2 * tq), F32),
                        pltpu.VMEM((8, 2 * tq), F32)],
        compiler_params=_cparams("arbitrary", "arbitrary", "arbitrary"),
    )(relb, proj, proj, vt, lamv, subw)


def _split3(x):
    hi = x.astype(BF16)
    r1 = x - hi.astype(F32)
    mid = r1.astype(BF16)
    lo = (r1 - mid.astype(F32)).astype(BF16)
    return hi, mid, lo


def _ssd_kernel(z_ref, xs_ref, bc_ref, dt_ref, cw_ref, cb_ref, dtb_ref, alog_ref, dskip_ref, nw_ref,
                o_ref, xpad_scr, state_scr):
    L = SSM_CHUNK
    W = SSM_WIDTH
    P2 = LANES
    nblk = W // P2
    gw = W // SSM_GROUPS

    @pl.when(pl.program_id(1) == 0)
    def _():
        xpad_scr[0:8, :] = jnp.zeros((8, W + SSM_BC), F32)
        state_scr[...] = jnp.zeros((SSM_STATE, W), F32)

    xpad_scr[8:8 + L, 0:W] = xs_ref[...].astype(F32)
    xpad_scr[8:8 + L, W:W + SSM_BC] = bc_ref[...].astype(F32)
    conv = cb_ref[...] + cw_ref[0:1, :] * xpad_scr[5:5 + L, :]
    for kk in range(1, SSM_CONV):
        conv = conv + cw_ref[kk:kk + 1, :] * xpad_scr[5 + kk:5 + kk + L, :]
    xpad_scr[0:8, :] = xpad_scr[L:L + 8, :]
    u = conv * _sigmoid(conv)

    dtr = dt_ref[...] + dtb_ref[...]
    dt = jnp.maximum(dtr, 0.0) + jnp.log1p(jnp.exp(-jnp.abs(dtr)))
    a = -jnp.exp(alog_ref[...])
    da = dt * a

    ri = lax.broadcasted_iota(I32, (L, L), 0)
    ci = lax.broadcasted_iota(I32, (L, L), 1)
    tril = ri >= ci
    tri = jnp.where(tril, 1.0, 0.0).astype(BF16)
    hi, mid, lo3 = _split3(da)
    a_cs = (jnp.dot(tri, hi, preferred_element_type=F32) + jnp.dot(tri, mid, preferred_element_type=F32)
            + jnp.dot(tri, lo3, preferred_element_type=F32))
    a_cs_t = a_cs.T
    a_last = a_cs[L - 1:L, :]
    e_cs = jnp.exp(a_cs)
    dt_ds = dt * jnp.exp(a_last - a_cs)

    lane_lo = lax.broadcasted_iota(I32, (1, P2), 1) < HEAD_DIM

    def expand(mat, i):
        return jnp.where(lane_lo, mat[:, 2 * i:2 * i + 1], mat[:, 2 * i + 1:2 * i + 2])

    y_blocks = []
    for g in range(SSM_GROUPS):
        bm = u[:, W + g * SSM_STATE:W + (g + 1) * SSM_STATE]
        cm = u[:, W + (SSM_GROUPS + g) * SSM_STATE:W + (SSM_GROUPS + g + 1) * SSM_STATE]
        bm16 = bm.astype(BF16)
        cm16 = cm.astype(BF16)
        cb = lax.dot_general(cm16, bm16, NT_DIMS, preferred_element_type=F32)
        st_g = state_scr[:, g * gw:(g + 1) * gw]
        y_off = jnp.dot(cm16, st_g.astype(BF16), preferred_element_type=F32)
        xd_blocks = []
        for ib in range(nblk // SSM_GROUPS):
            i = g * (nblk // SSM_GROUPS) + ib
            xs_blk = u[:, i * P2:(i + 1) * P2]
            xc = xs_blk * expand(dt, i)
            yd = jnp.zeros((L, P2), F32)
            for hh in range(2):
                head = 2 * i + hh
                seg = a_cs[:, head:head + 1] - a_cs_t[head:head + 1, :]
                wmat = (cb * jnp.where(tril, jnp.exp(seg), 0.0)).astype(BF16)
                keep = lane_lo if hh == 0 else jnp.logical_not(lane_lo)
                yd = yd + jnp.dot(wmat, jnp.where(keep, xc, 0.0).astype(BF16), preferred_element_type=F32)
            y = yd + y_off[:, ib * P2:(ib + 1) * P2] * expand(e_cs, i) + expand(dskip_ref[...], i) * xs_blk
            zf = z_ref[:, i * P2:(i + 1) * P2].astype(F32)
            y_blocks.append(y * (zf * _sigmoid(zf)))
            xd_blocks.append((xs_blk * expand(dt_ds, i)).astype(BF16))
        xd = jnp.concatenate(xd_blocks, axis=1)
        st_new = jnp.dot(bm.T.astype(BF16), xd, preferred_element_type=F32)
        decay = jnp.concatenate([expand(jnp.exp(a_last), g * (nblk // SSM_GROUPS) + ib)
                                 for ib in range(nblk // SSM_GROUPS)], axis=1)
        state_scr[:, g * gw:(g + 1) * gw] = st_g * decay + st_new

    per_g = nblk // SSM_GROUPS
    for g in range(SSM_GROUPS):
        blks = y_blocks[g * per_g:(g + 1) * per_g]
        ss = sum(jnp.sum(b * b, axis=1, keepdims=True) for b in blks) * (1.0 / gw)
        inv = lax.rsqrt(ss + NORM_EPS)
        for ib, b in enumerate(blks):
            i = g * per_g + ib
            o_ref[:, i * P2:(i + 1) * P2] = (b * inv * nw_ref[:, i * P2:(i + 1) * P2]).astype(BF16)


def _ssd(proj, dt_raw, conv_w, conv_b, dt_bias, a_log, d_skip, norm_w, batch, seq):
    n = batch * seq
    L = SSM_CHUNK
    nc = seq // L
    cd = SSM_WIDTH + SSM_BC
    row = lambda b, c: (b * nc + c)
    return pl.pallas_call(
        _ssd_kernel,
        grid=(batch, nc),
        in_specs=[pl.BlockSpec((L, SSM_WIDTH), lambda b, c: (row(b, c), 3)),
                  pl.BlockSpec((L, SSM_WIDTH), lambda b, c: (row(b, c), 4)),
                  pl.BlockSpec((L, SSM_BC), lambda b, c: (row(b, c), 10)),
                  pl.BlockSpec((L, LANES), lambda b, c: (row(b, c), 0)),
                  pl.BlockSpec((SSM_CONV, cd), lambda b, c: (0, 0)),
                  pl.BlockSpec((1, cd), lambda b, c: (0, 0)),
                  pl.BlockSpec((1, LANES), lambda b, c: (0, 0)),
                  pl.BlockSpec((1, LANES), lambda b, c: (0, 0)),
                  pl.BlockSpec((1, LANES), lambda b, c: (0, 0)),
                  pl.BlockSpec((1, SSM_WIDTH), lambda b, c: (0, 0))],
        out_specs=pl.BlockSpec((L, SSM_WIDTH), lambda b, c: (row(b, c), 0)),
        out_shape=jax.ShapeDtypeStruct((n, SSM_WIDTH), BF16),
        scratch_shapes=[pltpu.VMEM((L + 8, cd), F32),
                        pltpu.VMEM((SSM_STATE, SSM_WIDTH), F32)],
        compiler_params=_cparams("arbitrary", "arbitrary"),
    )(proj, proj, proj, dt_raw, conv_w, conv_b, dt_bias, a_log, d_skip, norm_w)


def _out_proj_kernel(x_ref, att_ref, ssm_ref, w_ref, mod_ref, nw_ref, h_ref, hn_ref):
    mix = (jnp.dot(att_ref[...], w_ref[0:ATT_WIDTH, :], preferred_element_type=F32)
           + jnp.dot(ssm_ref[...], w_ref[ATT_WIDTH:, :], preferred_element_type=F32))
    mod = mod_ref[0]
    h1 = x_ref[...] + mod[2:3, :] * mix
    h_ref[...] = h1
    y = h1 * lax.rsqrt(jnp.mean(h1 * h1, axis=1, keepdims=True) + NORM_EPS) * nw_ref[...]
    hn_ref[...] = (y * (1.0 + mod[4:5, :]) + mod[3:4, :]).astype(BF16)


def _out_proj(x2, att, ssm, w_out, mod3, norm2_w, seq):
    n, d = x2.shape
    tm = min(256, seq)
    return pl.pallas_call(
        _out_proj_kernel,
        grid=(n // tm,),
        in_specs=[pl.BlockSpec((tm, d), lambda i: (i, 0)),
                  pl.BlockSpec((tm, ATT_WIDTH), lambda i: (i, 0)),
                  pl.BlockSpec((tm, SSM_WIDTH), lambda i: (i, 0)),
                  pl.BlockSpec((ATT_WIDTH + SSM_WIDTH, d), lambda i: (0, 0)),
                  pl.BlockSpec((1, 6, d), lambda i: (i * tm // seq, 0, 0)),
                  pl.BlockSpec((1, d), lambda i: (0, 0))],
        out_specs=[pl.BlockSpec((tm, d), lambda i: (i, 0)),
                   pl.BlockSpec((tm, d), lambda i: (i, 0))],
        out_shape=[jax.ShapeDtypeStruct((n, d), F32),
                   jax.ShapeDtypeStruct((n, d), BF16)],
        compiler_params=_cparams("arbitrary"),
    )(x2, att, ssm, w_out, mod3, norm2_w)


def _topk_rows(s, k, rows):
    row = lax.broadcasted_iota(I32, s.shape, 0).astype(F32)
    vals, idxs = [], []
    for _ in range(k):
        m = jnp.max(s, axis=0, keepdims=True)
        idx = jnp.min(jnp.where(s == m, row, float(rows)), axis=0, keepdims=True)
        s = jnp.where(row == idx, -jnp.inf, s)
        vals.append(m)
        idxs.append(idx)
    return vals, idxs


_CAND = [(i, j) for i in range(PEER_TOPK) for j in range(PEER_TOPK) if (i + 1) * (j + 1) <= PEER_TOPK]


def _route_kernel(hn_ref, wq_ref, keys_ref, a_ref, b_ref, g_ref, top_scr, code_scr):
    t = hn_ref.shape[0]
    qp = jnp.dot(hn_ref[...], wq_ref[...], preferred_element_type=F32).astype(BF16)
    ncand = len(_CAND)
    pad = (-ncand) % 8
    for h in range(PEER_HEADS):
        sub = []
        for c in range(2):
            hc = 2 * h + c
            sc = lax.dot_general(keys_ref[hc], qp[:, hc * LANES:(hc + 1) * LANES], NT_DIMS,
                                 preferred_element_type=F32)
            sub.append(_topk_rows(sc, PEER_TOPK, PEER_NKEYS))
        (s1, i1), (s2, i2) = sub
        cand = jnp.concatenate([s1[i] + s2[j] for i, j in _CAND]
                               + [jnp.full((pad, t), -jnp.inf, F32)], axis=0)
        a_hi = [v * float(PEER_NKEYS) for v in i1]
        code = jnp.concatenate([a_hi[i] + i2[j] for i, j in _CAND] + [jnp.zeros((pad, t), F32)], axis=0)
        row = lax.broadcasted_iota(I32, cand.shape, 0).astype(F32)
        for kk in range(PEER_TOPK):
            m = jnp.max(cand, axis=0, keepdims=True)
            idx = jnp.min(jnp.where(cand == m, row, float(ncand + pad)), axis=0, keepdims=True)
            sel = row == idx
            slot = h * PEER_TOPK + kk
            top_scr[slot:slot + 1, :] = m
            code_scr[slot:slot + 1, :] = jnp.max(jnp.where(sel, code, -1.0), axis=0, keepdims=True)
            cand = jnp.where(sel, -jnp.inf, cand)
        top = top_scr[h * PEER_TOPK:(h + 1) * PEER_TOPK, :]
        e = jnp.exp(top - jnp.max(top, axis=0, keepdims=True))
        top_scr[h * PEER_TOPK:(h + 1) * PEER_TOPK, :] = e / jnp.sum(e, axis=0, keepdims=True)
    code_t = code_scr[...].T
    first = jnp.floor(code_t * (1.0 / PEER_NKEYS))
    a_ref[...] = first.astype(I32)
    b_ref[...] = (code_t - first * float(PEER_NKEYS)).astype(I32)
    g_ref[...] = top_scr[...].T


def _route(hn2, wq, keys):
    n, d = hn2.shape
    t = min(256, n)
    qd = wq.shape[1]
    out = jax.ShapeDtypeStruct((n, PEER_SLOTS), I32)
    return pl.pallas_call(
        _route_kernel,
        grid=(n // t,),
        in_specs=[pl.BlockSpec((t, d), lambda i: (i, 0)),
                  pl.BlockSpec((d, qd), lambda i: (0, 0)),
                  pl.BlockSpec(keys.shape, lambda i: (0, 0, 0))],
        out_specs=[pl.BlockSpec((t, PEER_SLOTS), lambda i: (i, 0))] * 3,
        out_shape=[out, out, jax.ShapeDtypeStruct((n, PEER_SLOTS), F32)],
        scratch_shapes=[pltpu.VMEM((PEER_SLOTS, t), F32),
                        pltpu.VMEM((PEER_SLOTS, t), F32)],
        compiler_params=_cparams("arbitrary"),
    )(hn2, wq, keys)


PAIR = 2 * PEER_NKEYS
DOWN_PAIRS = 8
UP_KEYS = 16
TOKEN_UNROLL = 16
DENSE_PITCH = PEER_NKEYS + 8


def _peer_down_kernel(x_ref, dn_ref, a_ref, b_ref, pre_ref):
    j = pl.program_id(1)

    @pl.when(j == 0)
    def _():
        pre_ref[...] = jnp.zeros(pre_ref.shape, F32)

    x = x_ref[...]
    a = a_ref[...]
    b = b_ref[...]
    pre = pre_ref[...]
    for q in range(DOWN_PAIRS):
        p = lax.dot_general(x, dn_ref[q * PAIR:(q + 1) * PAIR, :], NT_DIMS,
                            preferred_element_type=F32)
        for half in range(2):
            g = jnp.take_along_axis(p[:, half * LANES:(half + 1) * LANES], b, axis=1)
            pre = jnp.where(a == 2 * (DOWN_PAIRS * j + q) + half, g, pre)
    pre_ref[...] = pre


def _peer_down(hn2, down16, aidx, bidx):
    n, d = hn2.shape
    t = min(1024, n)
    slot_spec = pl.BlockSpec((t, PEER_SLOTS), lambda i, j: (i, 0))
    return pl.pallas_call(
        _peer_down_kernel,
        grid=(n // t, down16.shape[0] // (DOWN_PAIRS * PAIR)),
        in_specs=[pl.BlockSpec((t, d), lambda i, j: (i, 0)),
                  pl.BlockSpec((DOWN_PAIRS * PAIR, d), lambda i, j: (j, 0)),
                  slot_spec, slot_spec],
        out_specs=slot_spec,
        out_shape=jax.ShapeDtypeStruct((n, PEER_SLOTS), F32),
        compiler_params=_cparams("arbitrary", "arbitrary"),
    )(hn2, down16, aidx, bidx)


def _peer_up_kernel(pre_ref, g_ref, a_ref, b_ref, up_ref, h_ref, mod_ref, o_ref,
                    act_scr, dense_scr, acc_scr):
    j = pl.program_id(1)
    t = pre_ref.shape[0]
    nk = PEER_NKEYS

    @pl.when(j == 0)
    def _():
        pre = pre_ref[...]
        act_scr[...] = 0.5 * pre * (1.0 + lax.erf(pre * (1.0 / math.sqrt(2.0)))) * g_ref[...]
        acc_scr[...] = jnp.zeros(acc_scr.shape, F32)
        row = lax.broadcasted_iota(I32, (nk, PEER_SLOTS), 0)

        def body(i, carry):
            for grp in range(TOKEN_UNROLL // 8):
                base = pl.multiple_of(i * TOKEN_UNROLL + grp * 8, 8)
                a8 = a_ref[pl.ds(base, 8), :]
                b8 = b_ref[pl.ds(base, 8), :]
                c8 = act_scr[pl.ds(base, 8), :]
                for u in range(8):
                    xa = jnp.where(row == a8[u:u + 1, :], c8[u:u + 1, :], 0.0).astype(BF16)
                    yb = jnp.where(row == b8[u:u + 1, :], 1.0, 0.0).astype(BF16)
                    dense_scr[pl.ds(pl.multiple_of((base + u) * DENSE_PITCH, 8), nk), :] = lax.dot_general(
                        xa, yb, NT_DIMS, preferred_element_type=F32)
            return carry

        lax.fori_loop(0, t // TOKEN_UNROLL, body, 0)

    lhs = jnp.concatenate([dense_scr[pl.ds(UP_KEYS * j + u, t, stride=DENSE_PITCH), :]
                           for u in range(UP_KEYS)], axis=1).astype(BF16)
    acc_scr[...] += jnp.dot(lhs, up_ref[...], preferred_element_type=F32)

    @pl.when(j == pl.num_programs(1) - 1)
    def _():
        o_ref[...] = h_ref[...] + mod_ref[0][5:6, :] * acc_scr[...]


def _peer_up(pre, gate, aidx, bidx, up16, h1, mod3, seq):
    n, d = h1.shape
    t = min(256, seq)
    slot_spec = pl.BlockSpec((t, PEER_SLOTS), lambda i, j: (i, 0))
    return pl.pallas_call(
        _peer_up_kernel,
        grid=(n // t, up16.shape[0] // (UP_KEYS * PEER_NKEYS)),
        in_specs=[slot_spec, slot_spec, slot_spec, slot_spec,
                  pl.BlockSpec((UP_KEYS * PEER_NKEYS, d), lambda i, j: (j, 0)),
                  pl.BlockSpec((t, d), lambda i, j: (i, 0)),
                  pl.BlockSpec((1, 6, d), lambda i, j: (i * t // seq, 0, 0))],
        out_specs=pl.BlockSpec((t, d), lambda i, j: (i, 0)),
        out_shape=jax.ShapeDtypeStruct((n, d), F32),
        scratch_shapes=[pltpu.VMEM((t, PEER_SLOTS), F32),
                        pltpu.VMEM((t * DENSE_PITCH, PEER_NKEYS), F32),
                        pltpu.VMEM((t, d), F32)],
        compiler_params=_cparams("arbitrary", "arbitrary"),
    )(pre, gate, aidx, bidx, up16, h1, mod3)


def _pad_lanes(v):
    return jnp.pad(v.astype(F32), (0, LANES - v.shape[0])).reshape(1, LANES)


def _layer(h2, mod3, l, batch, seq, norm1_w, w_in, q_norm_w, k_norm_w, rel_bias, lambda_q1, lambda_k1,
           lambda_q2, lambda_k2, subln_w, conv_w, conv_b, dt_bias, a_log, d_skip, ssm_norm_w, w_out,
           norm2_w, peer_wq, peer_keys, expert_down, expert_up):
    d = h2.shape[1]
    lam_init = 0.8 - 0.6 * math.exp(-0.3 * l)
    w16 = w_in.astype(BF16)
    w_main = w16[:, :MAIN_COLS]
    w_dt = jnp.pad(w16[:, MAIN_COLS:], ((0, 0), (0, LANES - SSM_HEADS)))
    qn = jnp.tile(q_norm_w.astype(F32) * (HEAD_DIM ** -0.5 * LOG2E), 2).reshape(1, LANES)
    kn = jnp.tile(k_norm_w.astype(F32), 2).reshape(1, LANES)
    proj, dt_raw = _in_proj(h2, norm1_w.reshape(1, d), mod3, w_main, w_dt, qn, kn, seq)

    lamv = jnp.pad(jnp.stack([lambda_q1, lambda_k1, lambda_q2, lambda_k2]).astype(F32),
                   ((0, 4), (0, LANES - HEAD_DIM)))
    att = _attention(rel_bias.astype(F32).reshape(-1), proj, lamv, subln_w.reshape(1, LANES),
                     batch, seq, lam_init)
    ssm = _ssd(proj, dt_raw, conv_w, conv_b.reshape(1, -1), _pad_lanes(dt_bias), _pad_lanes(a_log),
               _pad_lanes(d_skip), ssm_norm_w.reshape(1, -1), batch, seq)
    h1, hn2 = _out_proj(h2, att, ssm, w_out.astype(BF16), mod3, norm2_w.reshape(1, d), seq)

    keys = peer_keys.astype(BF16).reshape(2 * PEER_HEADS, PEER_NKEYS, -1)
    aidx, bidx, gate = _route(hn2, peer_wq.astype(BF16), keys)
    pre = _peer_down(hn2, expert_down.astype(BF16), aidx, bidx)
    return _peer_up(pre, gate, aidx, bidx, expert_up.astype(BF16), h1, mod3, seq)


def kernel(x, c, ada_w, ada_b, norm1_w, w_in, q_norm_w, k_norm_w, rel_bias, lambda_q1, lambda_k1, lambda_q2, lambda_k2, subln_w, conv_w, conv_b, dt_bias, a_log, d_skip, ssm_norm_w, w_out, norm2_w, peer_wq, peer_keys, expert_down, expert_up):
    batch, seq, d = x.shape
    depth = ada_w.shape[0]
    h2 = x.reshape(batch * seq, d)
    c_pad = jnp.pad(c, ((0, 8 - batch), (0, 0)))
    for l in range(depth):
        mod = _ada(c_pad, ada_w[l], ada_b[l].reshape(1, -1))
        mod3 = mod[:batch].reshape(batch, 6, d)
        h2 = _layer(h2, mod3, l, batch, seq, norm1_w[l], w_in[l], q_norm_w[l], k_norm_w[l], rel_bias,
                    lambda_q1[l], lambda_k1[l], lambda_q2[l], lambda_k2[l], subln_w[l], conv_w[l],
                    conv_b[l], dt_bias[l], a_log[l], d_skip[l], ssm_norm_w[l], w_out[l], norm2_w[l],
                    peer_wq[l], peer_keys[l], expert_down[l], expert_up[l])
    return h2.reshape(batch, seq, d)
```

```python
import functools
import math

import jax
import jax.numpy as jnp
from jax import lax
from jax.experimental import pallas as pl
from jax.experimental.pallas import tpu as pltpu

F32 = jnp.float32
BF16 = jnp.bfloat16
I32 = jnp.int32

LANES = 128
VMEM_LIMIT = 56 * 1024 * 1024

NORM_EPS = 1e-6
HEAD_DIM = 64
ATT_HEADS = 8
ATT_WIDTH = 1024
SSM_WIDTH = 1024
SSM_HEADS = 16
SSM_GROUPS = 2
SSM_STATE = 128
SSM_CONV = 4
SSM_CHUNK = 128
SSM_BC = 2 * SSM_GROUPS * SSM_STATE
REL_BUCKETS = 32
REL_MAX_DIST = 128
PEER_HEADS = 8
PEER_NKEYS = 128
PEER_TOPK = 16
PEER_SLOTS = PEER_HEADS * PEER_TOPK
MAIN_COLS = 3 * ATT_WIDTH + SSM_WIDTH + SSM_WIDTH + SSM_BC
NEG = -1e30
LOG2E = math.log2(math.e)

NT_DIMS = (((1,), (1,)), ((), ()))


def _cparams(*sem):
    return pltpu.CompilerParams(dimension_semantics=sem, vmem_limit_bytes=VMEM_LIMIT)


def _sigmoid(x):
    return 1.0 / (1.0 + jnp.exp(-x))


def _ada_kernel(c_ref, w_ref, b_ref, o_ref):
    c = c_ref[...]
    sc = (c * _sigmoid(c)).astype(BF16)
    o_ref[...] = jnp.dot(sc, w_ref[...].astype(BF16), preferred_element_type=F32) + b_ref[...]


def _ada(c_pad, ada_w, ada_b):
    rows, d = c_pad.shape
    n = ada_w.shape[1]
    tn = 1536
    return pl.pallas_call(
        _ada_kernel,
        grid=(n // tn,),
        in_specs=[pl.BlockSpec((rows, d), lambda j: (0, 0)),
                  pl.BlockSpec((d, tn), lambda j: (0, j)),
                  pl.BlockSpec((1, tn), lambda j: (0, j))],
        out_specs=pl.BlockSpec((rows, tn), lambda j: (0, j)),
        out_shape=jax.ShapeDtypeStruct((rows, n), F32),
        compiler_params=_cparams("arbitrary"),
    )(c_pad, ada_w, ada_b)


def _group_rms(blk, w_row, lo):
    sq = blk * blk
    s_all = jnp.sum(sq, axis=1, keepdims=True)
    s_lo = jnp.sum(jnp.where(lo, sq, 0.0), axis=1, keepdims=True)
    s = jnp.where(lo, s_lo, s_all - s_lo)
    return blk * lax.rsqrt(s * (1.0 / HEAD_DIM) + NORM_EPS) * w_row


def _in_proj_kernel(x_ref, nw_ref, mod_ref, w_ref, wdt_ref, qn_ref, kn_ref,
                    o_ref, dt_ref, hn_scr, *, tn):
    j = pl.program_id(1)

    @pl.when(j == 0)
    def _():
        x = x_ref[...]
        y = x * lax.rsqrt(jnp.mean(x * x, axis=1, keepdims=True) + NORM_EPS) * nw_ref[...]
        mod = mod_ref[0]
        hn = (y * (1.0 + mod[1:2, :]) + mod[0:1, :]).astype(BF16)
        hn_scr[...] = hn
        dt_ref[...] = jnp.dot(hn, wdt_ref[...], preferred_element_type=F32)

    acc = jnp.dot(hn_scr[...], w_ref[...], preferred_element_type=F32)
    qk_tiles = 2 * ATT_WIDTH // tn

    @pl.when(j < qk_tiles)
    def _():
        lo = lax.broadcasted_iota(I32, (1, LANES), 1) < HEAD_DIM
        w_row = jnp.where(j < qk_tiles // 2, qn_ref[...], kn_ref[...])
        for cb in range(tn // LANES):
            blk = acc[:, cb * LANES:(cb + 1) * LANES]
            o_ref[:, cb * LANES:(cb + 1) * LANES] = _group_rms(blk, w_row, lo).astype(BF16)

    @pl.when(j >= qk_tiles)
    def _():
        o_ref[...] = acc.astype(BF16)


def _in_proj(x2, norm_w, mod3, w_main, w_dt, qn, kn, seq):
    n, d = x2.shape
    tm = min(1024, seq)
    tn = 512
    return pl.pallas_call(
        functools.partial(_in_proj_kernel, tn=tn),
        grid=(n // tm, MAIN_COLS // tn),
        in_specs=[pl.BlockSpec((tm, d), lambda i, j: (i, 0)),
                  pl.BlockSpec((1, d), lambda i, j: (0, 0)),
                  pl.BlockSpec((1, 6, d), lambda i, j: (i * tm // seq, 0, 0)),
                  pl.BlockSpec((d, tn), lambda i, j: (0, j)),
                  pl.BlockSpec((d, LANES), lambda i, j: (0, 0)),
                  pl.BlockSpec((1, LANES), lambda i, j: (0, 0)),
                  pl.BlockSpec((1, LANES), lambda i, j: (0, 0))],
        out_specs=[pl.BlockSpec((tm, tn), lambda i, j: (i, j)),
                   pl.BlockSpec((tm, LANES), lambda i, j: (i, 0))],
        out_shape=[jax.ShapeDtypeStruct((n, MAIN_COLS), BF16),
                   jax.ShapeDtypeStruct((n, LANES), F32)],
        scratch_shapes=[pltpu.VMEM((tm, d), BF16)],
        compiler_params=_cparams("arbitrary", "arbitrary"),
    )(x2, norm_w, mod3, w_main, w_dt, qn, kn)


ATT_GROUP = 2
ATT_UNROLL = 4
ATT_CHUNK = 256
VT_ROWS = LANES + 16


def _attn_kernel(relb_ref, q_ref, k_ref, vt_ref, lamv_ref, subw_ref, o_ref,
                 q2t_scr, acc_scr, bias_scr, s_scr, smax_scr, *, t, lam_init):
    hp = pl.program_id(1)
    qi = pl.program_id(2)
    tq = 2 * t
    nchunk = vt_ref.shape[0] // ATT_GROUP

    @pl.when(qi == 0)
    def _():
        kk = lax.broadcasted_iota(I32, (t, t), 0)
        qq = lax.broadcasted_iota(I32, (t, t), 1)
        max_exact = REL_BUCKETS // 2
        buckets = []
        for off in (0, t):
            nn = jnp.maximum(qq - kk + off, 0)
            nf = jnp.maximum(nn, 1).astype(F32)
            large = max_exact + (jnp.log(nf / max_exact) / math.log(REL_MAX_DIST / max_exact)
                                 * (REL_BUCKETS - max_exact)).astype(I32)
            buckets.append(jnp.where(nn < max_exact, nn, jnp.minimum(large, REL_BUCKETS - 1)))
        zeros = jnp.zeros((t, t), F32)
        masked = jnp.full((t, t), NEG, F32)
        for hh in range(ATT_GROUP):
            head = hp * ATT_GROUP + hh
            for m in range(2):
                far = relb_ref[(REL_BUCKETS - 1) * 2 * ATT_HEADS + head * 2 + m]
                diag, sub = zeros, zeros
                for b in range(REL_BUCKETS - 1):
                    delta = (relb_ref[b * 2 * ATT_HEADS + head * 2 + m] - far) * LOG2E
                    diag = jnp.where(buckets[0] == b, delta, diag)
                    sub = jnp.where(buckets[1] == b, delta, sub)
                diag = jnp.where(qq >= kk, diag, NEG)
                for tile, (first, last) in enumerate(((sub, zeros), (diag, sub), (masked, diag))):
                    bias_scr[hh, tile, :, m * tq:m * tq + t] = first
                    bias_scr[hh, tile, :, m * tq + t:(m + 1) * tq] = last

    d_lo = lax.broadcasted_iota(I32, (LANES, tq), 0) < HEAD_DIM
    for hh in range(ATT_GROUP):
        qt = q_ref[:, hh * LANES:(hh + 1) * LANES].astype(F32).T
        q2t_scr[hh, :, 0:tq] = jnp.where(d_lo, qt, 0.0).astype(BF16)
        q2t_scr[hh, :, tq:2 * tq] = jnp.where(d_lo, 0.0, qt).astype(BF16)
        acc_scr[hh] = jnp.zeros((VT_ROWS, 2 * tq), F32)

    def scores(c, hh):
        k_c = k_ref[pl.ds(pl.multiple_of(c * t, t), t), hh * LANES:(hh + 1) * LANES]
        return jnp.dot(k_c, q2t_scr[hh], preferred_element_type=F32)

    def issue(c, slot, tile):
        for hh in range(ATT_GROUP):
            s = scores(c, hh)
            if tile is not None:
                s = s + bias_scr[hh, tile]
            s_scr[slot, hh] = s
            row = slot * ATT_GROUP + hh
            smax_scr[row:row + 1, :] = jnp.max(s, axis=0, keepdims=True)

    def consume(c, carry, slot):
        out = []
        for hh in range(ATT_GROUP):
            m_prev = carry[hh]
            row = slot * ATT_GROUP + hh
            m_new = jnp.maximum(m_prev, smax_scr[row:row + 1, :])
            alpha = jnp.exp2(m_prev - m_new)
            p = jnp.exp2(s_scr[slot, hh] - m_new).astype(BF16)
            pv = jnp.dot(vt_ref[hh * nchunk + c], p, preferred_element_type=F32)
            acc_scr[hh] = alpha * acc_scr[hh] + pv
            out.append(m_new)
        return tuple(out)

    def run(first, tiles, more, reissue, carry):
        if reissue:
            issue(first, 0, tiles[0])
        for k in range(len(tiles)):
            if k + 1 < len(tiles):
                issue(first + k + 1, (k + 1) % 2, tiles[k + 1])
            elif more:
                issue(first + k + 1, (k + 1) % 2, None)
            carry = consume(first + k, carry, k % 2)
        return carry

    nfar = jnp.maximum(2 * qi - 1, 0)
    ntrip = nfar // ATT_UNROLL
    issue(0, 0, None)

    def trip(i, carry):
        return run(i * ATT_UNROLL, [None] * ATT_UNROLL, True, False, carry)

    init = tuple(jnp.full((1, 2 * tq), NEG, F32) for _ in range(ATT_GROUP))
    carry = lax.fori_loop(0, ntrip, trip, init)
    rest = ntrip * ATT_UNROLL
    tails = [functools.partial(run, rest, [None] * r + [0, 1, 2], False, False)
             for r in range(1, ATT_UNROLL, 2)]
    tails.append(functools.partial(run, rest, [1, 2], False, True))
    carry = lax.switch(jnp.where(qi == 0, len(tails) - 1, (nfar - rest) // 2), tails, carry)

    lv = lamv_ref[...]
    lam = (jnp.exp(jnp.sum(lv[0:1, :] * lv[1:2, :], axis=1, keepdims=True))
           - jnp.exp(jnp.sum(lv[2:3, :] * lv[3:4, :], axis=1, keepdims=True)) + lam_init)
    for hh in range(ATT_GROUP):
        acc = acc_scr[hh, 0:LANES, :] * (1.0 / acc_scr[hh, LANES:LANES + 1, :])
        o = (acc[:, 0:tq] - lam * acc[:, tq:2 * tq]).T
        o = o * lax.rsqrt(jnp.mean(o * o, axis=1, keepdims=True) + NORM_EPS) * subw_ref[...]
        o_ref[:, hh * LANES:(hh + 1) * LANES] = (o * (1.0 - lam_init)).astype(BF16)


def _attention(relb, proj, lamv, subw, batch, seq, lam_init):
    n = batch * seq
    t = ATT_CHUNK
    tq = 2 * t
    nq = seq // tq
    nk = seq // t
    gw = ATT_GROUP * LANES
    ngroups = ATT_HEADS // ATT_GROUP
    vt = proj[:, 2 * ATT_WIDTH:3 * ATT_WIDTH].reshape(batch, nk, t, ATT_HEADS, LANES)
    vt = vt.transpose(0, 3, 1, 4, 2).reshape(batch * ATT_HEADS * nk, LANES, t)
    ones_rows = jnp.zeros((vt.shape[0], VT_ROWS - LANES, t), BF16).at[:, 0, :].set(1.0)
    vt = jnp.concatenate([vt, ones_rows], axis=1)
    return pl.pallas_call(
        functools.partial(_attn_kernel, t=t, lam_init=lam_init),
        grid=(batch, ngroups, nq),
        in_specs=[pl.BlockSpec(memory_space=pltpu.SMEM),
                  pl.BlockSpec((tq, gw), lambda b, g, i: (b * nq + i, g)),
                  pl.BlockSpec((seq, gw), lambda b, g, i: (b, ngroups + g)),
                  pl.BlockSpec((ATT_GROUP * nk, VT_ROWS, t), lambda b, g, i: (b * ngroups + g, 0, 0)),
                  pl.BlockSpec((8, LANES), lambda b, g, i: (0, 0)),
                  pl.BlockSpec((1, LANES), lambda b, g, i: (0, 0))],
        out_specs=pl.BlockSpec((tq, gw), lambda b, g, i: (b * nq + i, g)),
        out_shape=jax.ShapeDtypeStruct((n, ATT_WIDTH), BF16),
        scratch_shapes=[pltpu.VMEM((ATT_GROUP, LANES, 2 * tq), BF16),
                        pltpu.VMEM((ATT_GROUP, VT_ROWS, 2 * tq), F32),
                        pltpu.VMEM((ATT_GROUP, 3, t, 2 * tq), F32),
                        pltpu.VMEM((2, ATT_GROUP, t, 2 * tq), F32),
                        pltpu.VMEM((8, 2 * tq), F32)],
        compiler_params=_cparams("arbitrary", "arbitrary", "arbitrary"),
    )(relb, proj, proj, vt, lamv, subw)


def _split3(x):
    hi = x.astype(BF16)
    r1 = x - hi.astype(F32)
    mid = r1.astype(BF16)
    lo = (r1 - mid.astype(F32)).astype(BF16)
    return hi, mid, lo


def _ssd_kernel(z_ref, xs_ref, bc_ref, dt_ref, cw_ref, cb_ref, dtb_ref, alog_ref, dskip_ref, nw_ref,
                o_ref, xpad_scr, state_scr):
    L = SSM_CHUNK
    W = SSM_WIDTH
    P2 = LANES
    nblk = W // P2
    gw = W // SSM_GROUPS

    @pl.when(pl.program_id(0) == 0)
    def _():
        xpad_scr[0:8, :] = jnp.zeros((8, W + SSM_BC), F32)
        state_scr[...] = jnp.zeros((SSM_STATE, W), F32)

    xpad_scr[8:8 + L, 0:W] = xs_ref[...].astype(F32)
    xpad_scr[8:8 + L, W:W + SSM_BC] = bc_ref[...].astype(F32)
    conv = cb_ref[...] + cw_ref[0:1, :] * xpad_scr[5:5 + L, :]
    for kk in range(1, SSM_CONV):
        conv = conv + cw_ref[kk:kk + 1, :] * xpad_scr[5 + kk:5 + kk + L, :]
    xpad_scr[0:8, :] = xpad_scr[L:L + 8, :]
    u = conv * _sigmoid(conv)

    dtr = dt_ref[...] + dtb_ref[...]
    dt = jnp.maximum(dtr, 0.0) + jnp.log1p(jnp.exp(-jnp.abs(dtr)))
    a = -jnp.exp(alog_ref[...])
    da = dt * a

    ri = lax.broadcasted_iota(I32, (L, L), 0)
    ci = lax.broadcasted_iota(I32, (L, L), 1)
    tril = ri >= ci
    tri = jnp.where(tril, 1.0, 0.0).astype(BF16)
    hi, mid, lo3 = _split3(da)
    a_cs = (jnp.dot(tri, hi, preferred_element_type=F32) + jnp.dot(tri, mid, preferred_element_type=F32)
            + jnp.dot(tri, lo3, preferred_element_type=F32))
    a_cs_t = a_cs.T
    a_last = a_cs[L - 1:L, :]
    e_cs = jnp.exp(a_cs)
    dt_ds = dt * jnp.exp(a_last - a_cs)

    lane_lo = lax.broadcasted_iota(I32, (1, P2), 1) < HEAD_DIM

    def expand(mat, i):
        return jnp.where(lane_lo, mat[:, 2 * i:2 * i + 1], mat[:, 2 * i + 1:2 * i + 2])

    y_blocks = []
    for g in range(SSM_GROUPS):
        bm = u[:, W + g * SSM_STATE:W + (g + 1) * SSM_STATE]
        cm = u[:, W + (SSM_GROUPS + g) * SSM_STATE:W + (SSM_GROUPS + g + 1) * SSM_STATE]
        bm16 = bm.astype(BF16)
        cm16 = cm.astype(BF16)
        cb = lax.dot_general(cm16, bm16, NT_DIMS, preferred_element_type=F32)
        st_g = state_scr[:, g * gw:(g + 1) * gw]
        y_off = jnp.dot(cm16, st_g.astype(BF16), preferred_element_type=F32)
        xd_blocks = []
        for ib in range(nblk // SSM_GROUPS):
            i = g * (nblk // SSM_GROUPS) + ib
            xs_blk = u[:, i * P2:(i + 1) * P2]
            xc = xs_blk * expand(dt, i)
            yd = jnp.zeros((L, P2), F32)
            for hh in range(2):
                head = 2 * i + hh
                seg = a_cs[:, head:head + 1] - a_cs_t[head:head + 1, :]
                wmat = (cb * jnp.where(tril, jnp.exp(seg), 0.0)).astype(BF16)
                keep = lane_lo if hh == 0 else jnp.logical_not(lane_lo)
                yd = yd + jnp.dot(wmat, jnp.where(keep, xc, 0.0).astype(BF16), preferred_element_type=F32)
            y = yd + y_off[:, ib * P2:(ib + 1) * P2] * expand(e_cs, i) + expand(dskip_ref[...], i) * xs_blk
            zf = z_ref[:, i * P2:(i + 1) * P2].astype(F32)
            y_blocks.append(y * (zf * _sigmoid(zf)))
            xd_blocks.append((xs_blk * expand(dt_ds, i)).astype(BF16))
        xd = jnp.concatenate(xd_blocks, axis=1)
        st_new = jnp.dot(bm.T.astype(BF16), xd, preferred_element_type=F32)
        decay = jnp.concatenate([expand(jnp.exp(a_last), g * (nblk // SSM_GROUPS) + ib)
                                 for ib in range(nblk // SSM_GROUPS)], axis=1)
        state_scr[:, g * gw:(g + 1) * gw] = st_g * decay + st_new

    per_g = nblk // SSM_GROUPS
    for g in range(SSM_GROUPS):
        blks = y_blocks[g * per_g:(g + 1) * per_g]
        ss = sum(jnp.sum(b * b, axis=1, keepdims=True) for b in blks) * (1.0 / gw)
        inv = lax.rsqrt(ss + NORM_EPS)
        for ib, b in enumerate(blks):
            i = g * per_g + ib
            o_ref[:, i * P2:(i + 1) * P2] = (b * inv * nw_ref[:, i * P2:(i + 1) * P2]).astype(BF16)


def _ssd_batched_kernel(z_ref, xs_ref, bc_ref, dt_ref, cw_ref, cb_ref, dtb_ref, alog_ref, dskip_ref, nw_ref,
                        o_ref, xpad_scr, state_scr):
    for b in range(z_ref.shape[0]):
        _ssd_kernel(z_ref.at[b], xs_ref.at[b], bc_ref.at[b], dt_ref.at[b], cw_ref, cb_ref, dtb_ref,
                    alog_ref, dskip_ref, nw_ref, o_ref.at[b], xpad_scr.at[b], state_scr.at[b])


def _ssd(proj, dt_raw, conv_w, conv_b, dt_bias, a_log, d_skip, norm_w, batch, seq):
    L = SSM_CHUNK
    cd = SSM_WIDTH + SSM_BC
    proj3 = proj.reshape(batch, seq, MAIN_COLS)
    dt3 = dt_raw.reshape(batch, seq, LANES)
    out = pl.pallas_call(
        _ssd_batched_kernel,
        grid=(seq // L,),
        in_specs=[pl.BlockSpec((batch, L, SSM_WIDTH), lambda c: (0, c, 3)),
                  pl.BlockSpec((batch, L, SSM_WIDTH), lambda c: (0, c, 4)),
                  pl.BlockSpec((batch, L, SSM_BC), lambda c: (0, c, 10)),
                  pl.BlockSpec((batch, L, LANES), lambda c: (0, c, 0)),
                  pl.BlockSpec((SSM_CONV, cd), lambda c: (0, 0)),
                  pl.BlockSpec((1, cd), lambda c: (0, 0)),
                  pl.BlockSpec((1, LANES), lambda c: (0, 0)),
                  pl.BlockSpec((1, LANES), lambda c: (0, 0)),
                  pl.BlockSpec((1, LANES), lambda c: (0, 0)),
                  pl.BlockSpec((1, SSM_WIDTH), lambda c: (0, 0))],
        out_specs=pl.BlockSpec((batch, L, SSM_WIDTH), lambda c: (0, c, 0)),
        out_shape=jax.ShapeDtypeStruct((batch, seq, SSM_WIDTH), BF16),
        scratch_shapes=[pltpu.VMEM((batch, L + 8, cd), F32),
                        pltpu.VMEM((batch, SSM_STATE, SSM_WIDTH), F32)],
        compiler_params=_cparams("arbitrary"),
    )(proj3, proj3, proj3, dt3, conv_w, conv_b, dt_bias, a_log, d_skip, norm_w)
    return out.reshape(batch * seq, SSM_WIDTH)


def _out_proj_kernel(x_ref, att_ref, ssm_ref, w_ref, mod_ref, nw_ref, h_ref, hn_ref):
    mix = (jnp.dot(att_ref[...], w_ref[0:ATT_WIDTH, :], preferred_element_type=F32)
           + jnp.dot(ssm_ref[...], w_ref[ATT_WIDTH:, :], preferred_element_type=F32))
    mod = mod_ref[0]
    h1 = x_ref[...] + mod[2:3, :] * mix
    h_ref[...] = h1
    y = h1 * lax.rsqrt(jnp.mean(h1 * h1, axis=1, keepdims=True) + NORM_EPS) * nw_ref[...]
    hn_ref[...] = (y * (1.0 + mod[4:5, :]) + mod[3:4, :]).astype(BF16)


def _out_proj(x2, att, ssm, w_out, mod3, norm2_w, seq):
    n, d = x2.shape
    tm = min(256, seq)
    return pl.pallas_call(
        _out_proj_kernel,
        grid=(n // tm,),
        in_specs=[pl.BlockSpec((tm, d), lambda i: (i, 0)),
                  pl.BlockSpec((tm, ATT_WIDTH), lambda i: (i, 0)),
                  pl.BlockSpec((tm, SSM_WIDTH), lambda i: (i, 0)),
                  pl.BlockSpec((ATT_WIDTH + SSM_WIDTH, d), lambda i: (0, 0)),
                  pl.BlockSpec((1, 6, d), lambda i: (i * tm // seq, 0, 0)),
                  pl.BlockSpec((1, d), lambda i: (0, 0))],
        out_specs=[pl.BlockSpec((tm, d), lambda i: (i, 0)),
                   pl.BlockSpec((tm, d), lambda i: (i, 0))],
        out_shape=[jax.ShapeDtypeStruct((n, d), F32),
                   jax.ShapeDtypeStruct((n, d), BF16)],
        compiler_params=_cparams("arbitrary"),
    )(x2, att, ssm, w_out, mod3, norm2_w)


def _topk_rows(s, k, rows):
    row = lax.broadcasted_iota(I32, s.shape, 0).astype(F32)
    vals, idxs = [], []
    for _ in range(k):
        m, idx = _argmax_rows(s, rows)
        s = jnp.where(row == idx, -jnp.inf, s)
        vals.append(m)
        idxs.append(idx)
    return vals, idxs


def _argmax_rows(s, rows):
    vs = [s[8 * i:8 * i + 8] for i in range(rows // 8)]
    sub = lax.broadcasted_iota(I32, vs[0].shape, 0).astype(F32)
    rs = [sub + float(8 * i) for i in range(rows // 8)]
    while len(vs) > 1:
        nv, nr = [], []
        for i in range(0, len(vs) - 1, 2):
            take_b = vs[i + 1] > vs[i]
            nv.append(jnp.where(take_b, vs[i + 1], vs[i]))
            nr.append(jnp.where(take_b, rs[i + 1], rs[i]))
        if len(vs) % 2:
            nv.append(vs[-1])
            nr.append(rs[-1])
        vs, rs = nv, nr
    m = jnp.max(vs[0], axis=0, keepdims=True)
    idx = jnp.min(jnp.where(vs[0] == m, rs[0], float(rows)), axis=0, keepdims=True)
    return m, idx


_CAND = [(i, j) for i in range(PEER_TOPK) for j in range(PEER_TOPK) if (i + 1) * (j + 1) <= PEER_TOPK]


def _route_kernel(hn_ref, wq_ref, keys_ref, a_ref, b_ref, g_ref, top_scr, code_scr):
    t = hn_ref.shape[0]
    qp = jnp.dot(hn_ref[...], wq_ref[...], preferred_element_type=F32).astype(BF16)
    ncand = len(_CAND)
    pad = (-ncand) % 8
    for h in range(PEER_HEADS):
        sub = []
        for c in range(2):
            hc = 2 * h + c
            sc = lax.dot_general(keys_ref[hc], qp[:, hc * LANES:(hc + 1) * LANES], NT_DIMS,
                                 preferred_element_type=F32)
            sub.append(_topk_rows(sc, PEER_TOPK, PEER_NKEYS))
        (s1, i1), (s2, i2) = sub
        cand = jnp.concatenate([s1[i] + s2[j] for i, j in _CAND]
                               + [jnp.full((pad, t), -jnp.inf, F32)], axis=0)
        a_hi = [v * float(PEER_NKEYS) for v in i1]
        code = jnp.concatenate([a_hi[i] + i2[j] for i, j in _CAND] + [jnp.zeros((pad, t), F32)], axis=0)
        row = lax.broadcasted_iota(I32, cand.shape, 0).astype(F32)
        for kk in range(PEER_TOPK):
            m, idx = _argmax_rows(cand, ncand + pad)
            sel = row == idx
            slot = h * PEER_TOPK + kk
            top_scr[slot:slot + 1, :] = m
            code_scr[slot:slot + 1, :] = jnp.max(jnp.where(sel, code, -1.0), axis=0, keepdims=True)
            cand = jnp.where(sel, -jnp.inf, cand)
        top = top_scr[h * PEER_TOPK:(h + 1) * PEER_TOPK, :]
        e = jnp.exp(top - jnp.max(top, axis=0, keepdims=True))
        top_scr[h * PEER_TOPK:(h + 1) * PEER_TOPK, :] = e / jnp.sum(e, axis=0, keepdims=True)
    code_t = code_scr[...].T
    first = jnp.floor(code_t * (1.0 / PEER_NKEYS))
    a_ref[...] = first.astype(I32)
    b_ref[...] = (code_t - first * float(PEER_NKEYS)).astype(I32)
    g_ref[...] = top_scr[...].T


def _route(hn2, wq, keys):
    n, d = hn2.shape
    t = min(256, n)
    qd = wq.shape[1]
    out = jax.ShapeDtypeStruct((n, PEER_SLOTS), I32)
    return pl.pallas_call(
        _route_kernel,
        grid=(n // t,),
        in_specs=[pl.BlockSpec((t, d), lambda i: (i, 0)),
                  pl.BlockSpec((d, qd), lambda i: (0, 0)),
                  pl.BlockSpec(keys.shape, lambda i: (0, 0, 0))],
        out_specs=[pl.BlockSpec((t, PEER_SLOTS), lambda i: (i, 0))] * 3,
        out_shape=[out, out, jax.ShapeDtypeStruct((n, PEER_SLOTS), F32)],
        scratch_shapes=[pltpu.VMEM((PEER_SLOTS, t), F32),
                        pltpu.VMEM((PEER_SLOTS, t), F32)],
        compiler_params=_cparams("arbitrary"),
    )(hn2, wq, keys)


PAIR = 2 * PEER_NKEYS
DOWN_PAIRS = 8
UP_KEYS = 16
TOKEN_UNROLL = 32
DENSE_PITCH = PEER_NKEYS + 8


def _peer_down_kernel(x_ref, dn_ref, a_ref, b_ref, pre_ref):
    j = pl.program_id(1)

    @pl.when(j == 0)
    def _():
        pre_ref[...] = jnp.zeros(pre_ref.shape, F32)

    x = x_ref[...]
    a = a_ref[...]
    b = b_ref[...]
    pre = pre_ref[...]
    for q in range(DOWN_PAIRS):
        p = lax.dot_general(x, dn_ref[q * PAIR:(q + 1) * PAIR, :], NT_DIMS,
                            preferred_element_type=F32)
        for half in range(2):
            g = jnp.take_along_axis(p[:, half * LANES:(half + 1) * LANES], b, axis=1)
            pre = jnp.where(a == 2 * (DOWN_PAIRS * j + q) + half, g, pre)
    pre_ref[...] = pre


def _peer_down(hn2, down16, aidx, bidx):
    n, d = hn2.shape
    t = min(1024, n)
    slot_spec = pl.BlockSpec((t, PEER_SLOTS), lambda i, j: (i, 0))
    return pl.pallas_call(
        _peer_down_kernel,
        grid=(n // t, down16.shape[0] // (DOWN_PAIRS * PAIR)),
        in_specs=[pl.BlockSpec((t, d), lambda i, j: (i, 0)),
                  pl.BlockSpec((DOWN_PAIRS * PAIR, d), lambda i, j: (j, 0)),
                  slot_spec, slot_spec],
        out_specs=slot_spec,
        out_shape=jax.ShapeDtypeStruct((n, PEER_SLOTS), F32),
        compiler_params=_cparams("arbitrary", "arbitrary"),
    )(hn2, down16, aidx, bidx)


def _peer_up_kernel(pre_ref, g_ref, a_ref, b_ref, up_ref, h_ref, mod_ref, o_ref,
                    act_scr, dense_scr, acc_scr):
    j = pl.program_id(1)
    t = pre_ref.shape[0]
    nk = PEER_NKEYS

    @pl.when(j == 0)
    def _():
        pre = pre_ref[...]
        act_scr[...] = 0.5 * pre * (1.0 + lax.erf(pre * (1.0 / math.sqrt(2.0)))) * g_ref[...]
        acc_scr[...] = jnp.zeros(acc_scr.shape, F32)
        row = lax.broadcasted_iota(I32, (nk, PEER_SLOTS), 0)

        def body(i, carry):
            for grp in range(TOKEN_UNROLL // 8):
                base = pl.multiple_of(i * TOKEN_UNROLL + grp * 8, 8)
                a8 = a_ref[pl.ds(base, 8), :]
                b8 = b_ref[pl.ds(base, 8), :]
                c8 = act_scr[pl.ds(base, 8), :]
                for u in range(8):
                    xa = jnp.where(row == a8[u:u + 1, :], c8[u:u + 1, :], 0.0).astype(BF16)
                    yb = jnp.where(row == b8[u:u + 1, :], 1.0, 0.0).astype(BF16)
                    dense_scr[pl.ds(pl.multiple_of((base + u) * DENSE_PITCH, 8), nk), :] = lax.dot_general(
                        xa, yb, NT_DIMS, preferred_element_type=F32)
            return carry

        lax.fori_loop(0, t // TOKEN_UNROLL, body, 0)

    lhs = jnp.concatenate([dense_scr[pl.ds(UP_KEYS * j + u, t, stride=DENSE_PITCH), :]
                           for u in range(UP_KEYS)], axis=1).astype(BF16)
    acc_scr[...] += jnp.dot(lhs, up_ref[...], preferred_element_type=F32)

    @pl.when(j == pl.num_programs(1) - 1)
    def _():
        o_ref[...] = h_ref[...] + mod_ref[0][5:6, :] * acc_scr[...]


def _peer_up(pre, gate, aidx, bidx, up16, h1, mod3, seq):
    n, d = h1.shape
    t = min(256, seq)
    slot_spec = pl.BlockSpec((t, PEER_SLOTS), lambda i, j: (i, 0))
    return pl.pallas_call(
        _peer_up_kernel,
        grid=(n // t, up16.shape[0] // (UP_KEYS * PEER_NKEYS)),
        in_specs=[slot_spec, slot_spec, slot_spec, slot_spec,
                  pl.BlockSpec((UP_KEYS * PEER_NKEYS, d), lambda i, j: (j, 0)),
                  pl.BlockSpec((t, d), lambda i, j: (i, 0)),
                  pl.BlockSpec((1, 6, d), lambda i, j: (i * t // seq, 0, 0))],
        out_specs=pl.BlockSpec((t, d), lambda i, j: (i, 0)),
        out_shape=jax.ShapeDtypeStruct((n, d), F32),
        scratch_shapes=[pltpu.VMEM((t, PEER_SLOTS), F32),
                        pltpu.VMEM((t * DENSE_PITCH, PEER_NKEYS), F32),
                        pltpu.VMEM((t, d), F32)],
        compiler_params=_cparams("arbitrary", "arbitrary"),
    )(pre, gate, aidx, bidx, up16, h1, mod3)


def _pad_lanes(v):
    return jnp.pad(v.astype(F32), (0, LANES - v.shape[0])).reshape(1, LANES)


def _layer(h2, mod3, l, batch, seq, norm1_w, w_in, q_norm_w, k_norm_w, rel_bias, lambda_q1, lambda_k1,
           lambda_q2, lambda_k2, subln_w, conv_w, conv_b, dt_bias, a_log, d_skip, ssm_norm_w, w_out,
           norm2_w, peer_wq, peer_keys, expert_down, expert_up):
    d = h2.shape[1]
    lam_init = 0.8 - 0.6 * math.exp(-0.3 * l)
    w16 = w_in.astype(BF16)
    w_main = w16[:, :MAIN_COLS]
    w_dt = jnp.pad(w16[:, MAIN_COLS:], ((0, 0), (0, LANES - SSM_HEADS)))
    qn = jnp.tile(q_norm_w.astype(F32) * (HEAD_DIM ** -0.5 * LOG2E), 2).reshape(1, LANES)
    kn = jnp.tile(k_norm_w.astype(F32), 2).reshape(1, LANES)
    proj, dt_raw = _in_proj(h2, norm1_w.reshape(1, d), mod3, w_main, w_dt, qn, kn, seq)

    lamv = jnp.pad(jnp.stack([lambda_q1, lambda_k1, lambda_q2, lambda_k2]).astype(F32),
                   ((0, 4), (0, LANES - HEAD_DIM)))
    att = _attention(rel_bias.astype(F32).reshape(-1), proj, lamv, subln_w.reshape(1, LANES),
                     batch, seq, lam_init)
    ssm = _ssd(proj, dt_raw, conv_w, conv_b.reshape(1, -1), _pad_lanes(dt_bias), _pad_lanes(a_log),
               _pad_lanes(d_skip), ssm_norm_w.reshape(1, -1), batch, seq)
    h1, hn2 = _out_proj(h2, att, ssm, w_out.astype(BF16), mod3, norm2_w.reshape(1, d), seq)

    keys = peer_keys.astype(BF16).reshape(2 * PEER_HEADS, PEER_NKEYS, -1)
    aidx, bidx, gate = _route(hn2, peer_wq.astype(BF16), keys)
    pre = _peer_down(hn2, expert_down.astype(BF16), aidx, bidx)
    return _peer_up(pre, gate, aidx, bidx, expert_up.astype(BF16), h1, mod3, seq)


def kernel(x, c, ada_w, ada_b, norm1_w, w_in, q_norm_w, k_norm_w, rel_bias, lambda_q1, lambda_k1, lambda_q2, lambda_k2, subln_w, conv_w, conv_b, dt_bias, a_log, d_skip, ssm_norm_w, w_out, norm2_w, peer_wq, peer_keys, expert_down, expert_up):
    batch, seq, d = x.shape
    depth = ada_w.shape[0]
    h2 = x.reshape(batch * seq, d)
    c_pad = jnp.pad(c, ((0, 8 - batch), (0, 0)))
    for l in range(depth):
        mod = _ada(c_pad, ada_w[l], ada_b[l].reshape(1, -1))
        mod3 = mod[:batch].reshape(batch, 6, d)
        h2 = _layer(h2, mod3, l, batch, seq, norm1_w[l], w_in[l], q_norm_w[l], k_norm_w[l], rel_bias,
                    lambda_q1[l], lambda_k1[l], lambda_q2[l], lambda_k2[l], subln_w[l], conv_w[l],
                    conv_b[l], dt_bias[l], a_log[l], d_skip[l], ssm_norm_w[l], w_out[l], norm2_w[l],
                    peer_wq[l], peer_keys[l], expert_down[l], expert_up[l])
    return h2.reshape(batch, seq, d)
```

```python
import functools
import math

import jax
import jax.numpy as jnp
from jax import lax
from jax.experimental import pallas as pl
from jax.experimental.pallas import tpu as pltpu

F32 = jnp.float32
BF16 = jnp.bfloat16
I32 = jnp.int32

LANES = 128
VMEM_LIMIT = 56 * 1024 * 1024

NORM_EPS = 1e-6
HEAD_DIM = 64
ATT_HEADS = 8
ATT_WIDTH = 1024
SSM_WIDTH = 1024
SSM_HEADS = 16
SSM_GROUPS = 2
SSM_STATE = 128
SSM_CONV = 4
SSM_CHUNK = 128
SSM_BC = 2 * SSM_GROUPS * SSM_STATE
REL_BUCKETS = 32
REL_MAX_DIST = 128
PEER_HEADS = 8
PEER_NKEYS = 128
PEER_TOPK = 16
PEER_SLOTS = PEER_HEADS * PEER_TOPK
MAIN_COLS = 3 * ATT_WIDTH + SSM_WIDTH + SSM_WIDTH + SSM_BC
NEG = -1e30
LOG2E = math.log2(math.e)

NT_DIMS = (((1,), (1,)), ((), ()))


def _cparams(*sem):
    return pltpu.CompilerParams(dimension_semantics=sem, vmem_limit_bytes=VMEM_LIMIT)


def _sigmoid(x):
    return 1.0 / (1.0 + jnp.exp(-x))


def _ada_kernel(c_ref, w_ref, b_ref, o_ref):
    c = c_ref[...]
    sc = (c * _sigmoid(c)).astype(BF16)
    o_ref[...] = jnp.dot(sc, w_ref[...].astype(BF16), preferred_element_type=F32) + b_ref[...]


def _ada(c_pad, ada_w, ada_b):
    rows, d = c_pad.shape
    n = ada_w.shape[1]
    tn = 1536
    return pl.pallas_call(
        _ada_kernel,
        grid=(n // tn,),
        in_specs=[pl.BlockSpec((rows, d), lambda j: (0, 0)),
                  pl.BlockSpec((d, tn), lambda j: (0, j)),
                  pl.BlockSpec((1, tn), lambda j: (0, j))],
        out_specs=pl.BlockSpec((rows, tn), lambda j: (0, j)),
        out_shape=jax.ShapeDtypeStruct((rows, n), F32),
        compiler_params=_cparams("arbitrary"),
    )(c_pad, ada_w, ada_b)


def _group_rms(blk, w_row, lo):
    sq = blk * blk
    s_all = jnp.sum(sq, axis=1, keepdims=True)
    s_lo = jnp.sum(jnp.where(lo, sq, 0.0), axis=1, keepdims=True)
    s = jnp.where(lo, s_lo, s_all - s_lo)
    return blk * lax.rsqrt(s * (1.0 / HEAD_DIM) + NORM_EPS) * w_row


def _in_proj_kernel(x_ref, nw_ref, mod_ref, w_ref, wdt_ref, qn_ref, kn_ref,
                    o_ref, dt_ref, hn_scr, *, tn):
    j = pl.program_id(1)

    @pl.when(j == 0)
    def _():
        x = x_ref[...]
        y = x * lax.rsqrt(jnp.mean(x * x, axis=1, keepdims=True) + NORM_EPS) * nw_ref[...]
        mod = mod_ref[0]
        hn = (y * (1.0 + mod[1:2, :]) + mod[0:1, :]).astype(BF16)
        hn_scr[...] = hn
        dt_ref[...] = jnp.dot(hn, wdt_ref[...], preferred_element_type=F32)

    acc = jnp.dot(hn_scr[...], w_ref[...], preferred_element_type=F32)
    qk_tiles = 2 * ATT_WIDTH // tn

    @pl.when(j < qk_tiles)
    def _():
        lo = lax.broadcasted_iota(I32, (1, LANES), 1) < HEAD_DIM
        w_row = jnp.where(j < qk_tiles // 2, qn_ref[...], kn_ref[...])
        for cb in range(tn // LANES):
            blk = acc[:, cb * LANES:(cb + 1) * LANES]
            o_ref[:, cb * LANES:(cb + 1) * LANES] = _group_rms(blk, w_row, lo).astype(BF16)

    @pl.when(j >= qk_tiles)
    def _():
        o_ref[...] = acc.astype(BF16)


def _in_proj(x2, norm_w, mod3, w_main, w_dt, qn, kn, seq):
    n, d = x2.shape
    tm = min(1024, seq)
    tn = 512
    return pl.pallas_call(
        functools.partial(_in_proj_kernel, tn=tn),
        grid=(n // tm, MAIN_COLS // tn),
        in_specs=[pl.BlockSpec((tm, d), lambda i, j: (i, 0)),
                  pl.BlockSpec((1, d), lambda i, j: (0, 0)),
                  pl.BlockSpec((1, 6, d), lambda i, j: (i * tm // seq, 0, 0)),
                  pl.BlockSpec((d, tn), lambda i, j: (0, j)),
                  pl.BlockSpec((d, LANES), lambda i, j: (0, 0)),
                  pl.BlockSpec((1, LANES), lambda i, j: (0, 0)),
                  pl.BlockSpec((1, LANES), lambda i, j: (0, 0))],
        out_specs=[pl.BlockSpec((tm, tn), lambda i, j: (i, j)),
                   pl.BlockSpec((tm, LANES), lambda i, j: (i, 0))],
        out_shape=[jax.ShapeDtypeStruct((n, MAIN_COLS), BF16),
                   jax.ShapeDtypeStruct((n, LANES), F32)],
        scratch_shapes=[pltpu.VMEM((tm, d), BF16)],
        compiler_params=_cparams("arbitrary", "arbitrary"),
    )(x2, norm_w, mod3, w_main, w_dt, qn, kn)


ATT_GROUP = 2
ATT_UNROLL = 4
ATT_CHUNK = 256
VT_ROWS = LANES + 16


def _attn_kernel(relb_ref, q_ref, k_ref, vt_ref, lamv_ref, subw_ref, o_ref,
                 q2t_scr, acc_scr, bias_scr, s_scr, smax_scr, *, t, lam_init):
    hp = pl.program_id(1)
    qi = pl.program_id(2)
    tq = 2 * t
    nchunk = vt_ref.shape[0] // ATT_GROUP

    @pl.when(qi == 0)
    def _():
        kk = lax.broadcasted_iota(I32, (t, t), 0)
        qq = lax.broadcasted_iota(I32, (t, t), 1)
        max_exact = REL_BUCKETS // 2
        buckets = []
        for off in (0, t):
            nn = jnp.maximum(qq - kk + off, 0)
            nf = jnp.maximum(nn, 1).astype(F32)
            large = max_exact + (jnp.log(nf / max_exact) / math.log(REL_MAX_DIST / max_exact)
                                 * (REL_BUCKETS - max_exact)).astype(I32)
            buckets.append(jnp.where(nn < max_exact, nn, jnp.minimum(large, REL_BUCKETS - 1)))
        zeros = jnp.zeros((t, t), F32)
        masked = jnp.full((t, t), NEG, F32)
        for hh in range(ATT_GROUP):
            head = hp * ATT_GROUP + hh
            for m in range(2):
                far = relb_ref[(REL_BUCKETS - 1) * 2 * ATT_HEADS + head * 2 + m]
                diag, sub = zeros, zeros
                for b in range(REL_BUCKETS - 1):
                    delta = (relb_ref[b * 2 * ATT_HEADS + head * 2 + m] - far) * LOG2E
                    diag = jnp.where(buckets[0] == b, delta, diag)
                    sub = jnp.where(buckets[1] == b, delta, sub)
                diag = jnp.where(qq >= kk, diag, NEG)
                for tile, (first, last) in enumerate(((sub, zeros), (diag, sub), (masked, diag))):
                    bias_scr[hh, tile, :, m * tq:m * tq + t] = first
                    bias_scr[hh, tile, :, m * tq + t:(m + 1) * tq] = last

    d_lo = lax.broadcasted_iota(I32, (LANES, tq), 0) < HEAD_DIM
    for hh in range(ATT_GROUP):
        qt = q_ref[:, hh * LANES:(hh + 1) * LANES].astype(F32).T
        q2t_scr[hh, :, 0:tq] = jnp.where(d_lo, qt, 0.0).astype(BF16)
        q2t_scr[hh, :, tq:2 * tq] = jnp.where(d_lo, 0.0, qt).astype(BF16)
        acc_scr[hh] = jnp.zeros((VT_ROWS, 2 * tq), F32)

    def scores(c, hh):
        k_c = k_ref[pl.ds(pl.multiple_of(c * t, t), t), hh * LANES:(hh + 1) * LANES]
        return jnp.dot(k_c, q2t_scr[hh], preferred_element_type=F32)

    def issue(c, slot, tile):
        for hh in range(ATT_GROUP):
            s = scores(c, hh)
            if tile is not None:
                s = s + bias_scr[hh, tile]
            s_scr[slot, hh] = s
            row = slot * ATT_GROUP + hh
            smax_scr[row:row + 1, :] = jnp.max(s, axis=0, keepdims=True)

    def consume(c, carry, slot):
        out = []
        for hh in range(ATT_GROUP):
            m_prev = carry[hh]
            row = slot * ATT_GROUP + hh
            m_new = jnp.maximum(m_prev, smax_scr[row:row + 1, :])
            alpha = jnp.exp2(m_prev - m_new)
            p = jnp.exp2(s_scr[slot, hh] - m_new).astype(BF16)
            pv = jnp.dot(vt_ref[hh * nchunk + c], p, preferred_element_type=F32)
            acc_scr[hh] = alpha * acc_scr[hh] + pv
            out.append(m_new)
        return tuple(out)

    def run(first, tiles, more, reissue, carry):
        if reissue:
            issue(first, 0, tiles[0])
        for k in range(len(tiles)):
            if k + 1 < len(tiles):
                issue(first + k + 1, (k + 1) % 2, tiles[k + 1])
            elif more:
                issue(first + k + 1, (k + 1) % 2, None)
            carry = consume(first + k, carry, k % 2)
        return carry

    nfar = jnp.maximum(2 * qi - 1, 0)
    ntrip = nfar // ATT_UNROLL
    issue(0, 0, None)

    def trip(i, carry):
        return run(i * ATT_UNROLL, [None] * ATT_UNROLL, True, False, carry)

    init = tuple(jnp.full((1, 2 * tq), NEG, F32) for _ in range(ATT_GROUP))
    carry = lax.fori_loop(0, ntrip, trip, init)
    rest = ntrip * ATT_UNROLL
    tails = [functools.partial(run, rest, [None] * r + [0, 1, 2], False, False)
             for r in range(1, ATT_UNROLL, 2)]
    tails.append(functools.partial(run, rest, [1, 2], False, True))
    carry = lax.switch(jnp.where(qi == 0, len(tails) - 1, (nfar - rest) // 2), tails, carry)

    lv = lamv_ref[...]
    lam = (jnp.exp(jnp.sum(lv[0:1, :] * lv[1:2, :], axis=1, keepdims=True))
           - jnp.exp(jnp.sum(lv[2:3, :] * lv[3:4, :], axis=1, keepdims=True)) + lam_init)
    for hh in range(ATT_GROUP):
        acc = acc_scr[hh, 0:LANES, :] * (1.0 / acc_scr[hh, LANES:LANES + 1, :])
        o = (acc[:, 0:tq] - lam * acc[:, tq:2 * tq]).T
        o = o * lax.rsqrt(jnp.mean(o * o, axis=1, keepdims=True) + NORM_EPS) * subw_ref[...]
        o_ref[:, hh * LANES:(hh + 1) * LANES] = (o * (1.0 - lam_init)).astype(BF16)


def _attention(relb, proj, lamv, subw, batch, seq, lam_init):
    n = batch * seq
    t = ATT_CHUNK
    tq = 2 * t
    nq = seq // tq
    nk = seq // t
    gw = ATT_GROUP * LANES
    ngroups = ATT_HEADS // ATT_GROUP
    vt = proj[:, 2 * ATT_WIDTH:3 * ATT_WIDTH].reshape(batch, nk, t, ATT_HEADS, LANES)
    vt = vt.transpose(0, 3, 1, 4, 2).reshape(batch * ATT_HEADS * nk, LANES, t)
    ones_rows = jnp.zeros((vt.shape[0], VT_ROWS - LANES, t), BF16).at[:, 0, :].set(1.0)
    vt = jnp.concatenate([vt, ones_rows], axis=1)
    return pl.pallas_call(
        functools.partial(_attn_kernel, t=t, lam_init=lam_init),
        grid=(batch, ngroups, nq),
        in_specs=[pl.BlockSpec(memory_space=pltpu.SMEM),
                  pl.BlockSpec((tq, gw), lambda b, g, i: (b * nq + i, g)),
                  pl.BlockSpec((seq, gw), lambda b, g, i: (b, ngroups + g)),
                  pl.BlockSpec((ATT_GROUP * nk, VT_ROWS, t), lambda b, g, i: (b * ngroups + g, 0, 0)),
                  pl.BlockSpec((8, LANES), lambda b, g, i: (0, 0)),
                  pl.BlockSpec((1, LANES), lambda b, g, i: (0, 0))],
        out_specs=pl.BlockSpec((tq, gw), lambda b, g, i: (b * nq + i, g)),
        out_shape=jax.ShapeDtypeStruct((n, ATT_WIDTH), BF16),
        scratch_shapes=[pltpu.VMEM((ATT_GROUP, LANES, 2 * tq), BF16),
                        pltpu.VMEM((ATT_GROUP, VT_ROWS, 2 * tq), F32),
                        pltpu.VMEM((ATT_GROUP, 3, t, 2 * tq), F32),
                        pltpu.VMEM((2, ATT_GROUP, t, 2 * tq), F32),
                        pltpu.VMEM((8, 2 * tq), F32)],
        compiler_params=_cparams("arbitrary", "arbitrary", "arbitrary"),
    )(relb, proj, proj, vt, lamv, subw)


def _split3(x):
    hi = x.astype(BF16)
    r1 = x - hi.astype(F32)
    mid = r1.astype(BF16)
    lo = (r1 - mid.astype(F32)).astype(BF16)
    return hi, mid, lo


def _ssd_kernel(z_ref, xs_ref, bc_ref, dt_ref, cw_ref, cb_ref, dtb_ref, alog_ref, dskip_ref, nw_ref,
                o_ref, xpad_scr, state_scr):
    L = SSM_CHUNK
    W = SSM_WIDTH
    P2 = LANES
    nblk = W // P2
    gw = W // SSM_GROUPS

    @pl.when(pl.program_id(0) == 0)
    def _():
        xpad_scr[0:8, :] = jnp.zeros((8, W + SSM_BC), F32)
        state_scr[...] = jnp.zeros((SSM_STATE, W), F32)

    xpad_scr[8:8 + L, 0:W] = xs_ref[...].astype(F32)
    xpad_scr[8:8 + L, W:W + SSM_BC] = bc_ref[...].astype(F32)
    conv = cb_ref[...] + cw_ref[0:1, :] * xpad_scr[5:5 + L, :]
    for kk in range(1, SSM_CONV):
        conv = conv + cw_ref[kk:kk + 1, :] * xpad_scr[5 + kk:5 + kk + L, :]
    xpad_scr[0:8, :] = xpad_scr[L:L + 8, :]
    u = conv * _sigmoid(conv)

    dtr = dt_ref[...] + dtb_ref[...]
    dt = jnp.maximum(dtr, 0.0) + jnp.log1p(jnp.exp(-jnp.abs(dtr)))
    a = -jnp.exp(alog_ref[...])
    da = dt * a

    ri = lax.broadcasted_iota(I32, (L, L), 0)
    ci = lax.broadcasted_iota(I32, (L, L), 1)
    tril = ri >= ci
    tri = jnp.where(tril, 1.0, 0.0).astype(BF16)
    hi, mid, lo3 = _split3(da)
    a_cs = (jnp.dot(tri, hi, preferred_element_type=F32) + jnp.dot(tri, mid, preferred_element_type=F32)
            + jnp.dot(tri, lo3, preferred_element_type=F32))
    a_cs_t = a_cs.T
    a_last = a_cs[L - 1:L, :]
    e_cs = jnp.exp(a_cs)
    dt_ds = dt * jnp.exp(a_last - a_cs)

    lane_lo = lax.broadcasted_iota(I32, (1, P2), 1) < HEAD_DIM

    def expand(mat, i):
        return jnp.where(lane_lo, mat[:, 2 * i:2 * i + 1], mat[:, 2 * i + 1:2 * i + 2])

    y_blocks = []
    for g in range(SSM_GROUPS):
        bm = u[:, W + g * SSM_STATE:W + (g + 1) * SSM_STATE]
        cm = u[:, W + (SSM_GROUPS + g) * SSM_STATE:W + (SSM_GROUPS + g + 1) * SSM_STATE]
        bm16 = bm.astype(BF16)
        cm16 = cm.astype(BF16)
        cb = lax.dot_general(cm16, bm16, NT_DIMS, preferred_element_type=F32)
        st_g = state_scr[:, g * gw:(g + 1) * gw]
        y_off = jnp.dot(cm16, st_g.astype(BF16), preferred_element_type=F32)
        xd_blocks = []
        for ib in range(nblk // SSM_GROUPS):
            i = g * (nblk // SSM_GROUPS) + ib
            xs_blk = u[:, i * P2:(i + 1) * P2]
            xc = xs_blk * expand(dt, i)
            yd = jnp.zeros((L, P2), F32)
            for hh in range(2):
                head = 2 * i + hh
                seg = a_cs[:, head:head + 1] - a_cs_t[head:head + 1, :]
                wmat = (cb * jnp.where(tril, jnp.exp(seg), 0.0)).astype(BF16)
                keep = lane_lo if hh == 0 else jnp.logical_not(lane_lo)
                yd = yd + jnp.dot(wmat, jnp.where(keep, xc, 0.0).astype(BF16), preferred_element_type=F32)
            y = yd + y_off[:, ib * P2:(ib + 1) * P2] * expand(e_cs, i) + expand(dskip_ref[...], i) * xs_blk
            zf = z_ref[:, i * P2:(i + 1) * P2].astype(F32)
            y_blocks.append(y * (zf * _sigmoid(zf)))
            xd_blocks.append((xs_blk * expand(dt_ds, i)).astype(BF16))
        xd = jnp.concatenate(xd_blocks, axis=1)
        st_new = jnp.dot(bm.T.astype(BF16), xd, preferred_element_type=F32)
        decay = jnp.concatenate([expand(jnp.exp(a_last), g * (nblk // SSM_GROUPS) + ib)
                                 for ib in range(nblk // SSM_GROUPS)], axis=1)
        state_scr[:, g * gw:(g + 1) * gw] = st_g * decay + st_new

    per_g = nblk // SSM_GROUPS
    for g in range(SSM_GROUPS):
        blks = y_blocks[g * per_g:(g + 1) * per_g]
        ss = sum(jnp.sum(b * b, axis=1, keepdims=True) for b in blks) * (1.0 / gw)
        inv = lax.rsqrt(ss + NORM_EPS)
        for ib, b in enumerate(blks):
            i = g * per_g + ib
            o_ref[:, i * P2:(i + 1) * P2] = (b * inv * nw_ref[:, i * P2:(i + 1) * P2]).astype(BF16)


def _ssd_batched_kernel(z_ref, xs_ref, bc_ref, dt_ref, cw_ref, cb_ref, dtb_ref, alog_ref, dskip_ref, nw_ref,
                        o_ref, xpad_scr, state_scr):
    for b in range(z_ref.shape[0]):
        _ssd_kernel(z_ref.at[b], xs_ref.at[b], bc_ref.at[b], dt_ref.at[b], cw_ref, cb_ref, dtb_ref,
                    alog_ref, dskip_ref, nw_ref, o_ref.at[b], xpad_scr.at[b], state_scr.at[b])


def _ssd(proj, dt_raw, conv_w, conv_b, dt_bias, a_log, d_skip, norm_w, batch, seq):
    L = SSM_CHUNK
    cd = SSM_WIDTH + SSM_BC
    proj3 = proj.reshape(batch, seq, MAIN_COLS)
    dt3 = dt_raw.reshape(batch, seq, LANES)
    out = pl.pallas_call(
        _ssd_batched_kernel,
        grid=(seq // L,),
        in_specs=[pl.BlockSpec((batch, L, SSM_WIDTH), lambda c: (0, c, 3)),
                  pl.BlockSpec((batch, L, SSM_WIDTH), lambda c: (0, c, 4)),
                  pl.BlockSpec((batch, L, SSM_BC), lambda c: (0, c, 10)),
                  pl.BlockSpec((batch, L, LANES), lambda c: (0, c, 0)),
                  pl.BlockSpec((SSM_CONV, cd), lambda c: (0, 0)),
                  pl.BlockSpec((1, cd), lambda c: (0, 0)),
                  pl.BlockSpec((1, LANES), lambda c: (0, 0)),
                  pl.BlockSpec((1, LANES), lambda c: (0, 0)),
                  pl.BlockSpec((1, LANES), lambda c: (0, 0)),
                  pl.BlockSpec((1, SSM_WIDTH), lambda c: (0, 0))],
        out_specs=pl.BlockSpec((batch, L, SSM_WIDTH), lambda c: (0, c, 0)),
        out_shape=jax.ShapeDtypeStruct((batch, seq, SSM_WIDTH), BF16),
        scratch_shapes=[pltpu.VMEM((batch, L + 8, cd), F32),
                        pltpu.VMEM((batch, SSM_STATE, SSM_WIDTH), F32)],
        compiler_params=_cparams("arbitrary"),
    )(proj3, proj3, proj3, dt3, conv_w, conv_b, dt_bias, a_log, d_skip, norm_w)
    return out.reshape(batch * seq, SSM_WIDTH)


def _out_proj_kernel(x_ref, att_ref, ssm_ref, w_ref, mod_ref, nw_ref, h_ref, hn_ref):
    mix = (jnp.dot(att_ref[...], w_ref[0:ATT_WIDTH, :], preferred_element_type=F32)
           + jnp.dot(ssm_ref[...], w_ref[ATT_WIDTH:, :], preferred_element_type=F32))
    mod = mod_ref[0]
    h1 = x_ref[...] + mod[2:3, :] * mix
    h_ref[...] = h1
    y = h1 * lax.rsqrt(jnp.mean(h1 * h1, axis=1, keepdims=True) + NORM_EPS) * nw_ref[...]
    hn_ref[...] = (y * (1.0 + mod[4:5, :]) + mod[3:4, :]).astype(BF16)


def _out_proj(x2, att, ssm, w_out, mod3, norm2_w, seq):
    n, d = x2.shape
    tm = min(256, seq)
    return pl.pallas_call(
        _out_proj_kernel,
        grid=(n // tm,),
        in_specs=[pl.BlockSpec((tm, d), lambda i: (i, 0)),
                  pl.BlockSpec((tm, ATT_WIDTH), lambda i: (i, 0)),
                  pl.BlockSpec((tm, SSM_WIDTH), lambda i: (i, 0)),
                  pl.BlockSpec((ATT_WIDTH + SSM_WIDTH, d), lambda i: (0, 0)),
                  pl.BlockSpec((1, 6, d), lambda i: (i * tm // seq, 0, 0)),
                  pl.BlockSpec((1, d), lambda i: (0, 0))],
        out_specs=[pl.BlockSpec((tm, d), lambda i: (i, 0)),
                   pl.BlockSpec((tm, d), lambda i: (i, 0))],
        out_shape=[jax.ShapeDtypeStruct((n, d), F32),
                   jax.ShapeDtypeStruct((n, d), BF16)],
        compiler_params=_cparams("arbitrary"),
    )(x2, att, ssm, w_out, mod3, norm2_w)


def _topk_rows(s, k, rows):
    row = lax.broadcasted_iota(I32, s.shape, 0).astype(F32)
    vals, idxs = [], []
    for _ in range(k):
        m, idx = _argmax_rows(s, rows)
        s = jnp.where(row == idx, -jnp.inf, s)
        vals.append(m)
        idxs.append(idx)
    return vals, idxs


def _argmax_rows(s, rows):
    vs = [s[8 * i:8 * i + 8] for i in range(rows // 8)]
    sub = lax.broadcasted_iota(I32, vs[0].shape, 0).astype(F32)
    rs = [sub + float(8 * i) for i in range(rows // 8)]
    while len(vs) > 1:
        nv, nr = [], []
        for i in range(0, len(vs) - 1, 2):
            take_b = vs[i + 1] > vs[i]
            nv.append(jnp.where(take_b, vs[i + 1], vs[i]))
            nr.append(jnp.where(take_b, rs[i + 1], rs[i]))
        if len(vs) % 2:
            nv.append(vs[-1])
            nr.append(rs[-1])
        vs, rs = nv, nr
    m = jnp.max(vs[0], axis=0, keepdims=True)
    idx = jnp.min(jnp.where(vs[0] == m, rs[0], float(rows)), axis=0, keepdims=True)
    return m, idx


_CAND = [(i, j) for i in range(PEER_TOPK) for j in range(PEER_TOPK) if (i + 1) * (j + 1) <= PEER_TOPK]


def _route_kernel(hn_ref, wq_ref, keys_ref, a_ref, b_ref, g_ref, top_scr, code_scr):
    t = hn_ref.shape[0]
    qp = jnp.dot(hn_ref[...], wq_ref[...], preferred_element_type=F32).astype(BF16)
    ncand = len(_CAND)
    pad = (-ncand) % 8
    for h in range(PEER_HEADS):
        sub = []
        for c in range(2):
            hc = 2 * h + c
            sc = lax.dot_general(keys_ref[hc], qp[:, hc * LANES:(hc + 1) * LANES], NT_DIMS,
                                 preferred_element_type=F32)
            sub.append(_topk_rows(sc, PEER_TOPK, PEER_NKEYS))
        (s1, i1), (s2, i2) = sub
        cand = jnp.concatenate([s1[i] + s2[j] for i, j in _CAND]
                               + [jnp.full((pad, t), -jnp.inf, F32)], axis=0)
        a_hi = [v * float(PEER_NKEYS) for v in i1]
        code = jnp.concatenate([a_hi[i] + i2[j] for i, j in _CAND] + [jnp.zeros((pad, t), F32)], axis=0)
        row = lax.broadcasted_iota(I32, cand.shape, 0).astype(F32)
        for kk in range(PEER_TOPK):
            m, idx = _argmax_rows(cand, ncand + pad)
            sel = row == idx
            slot = h * PEER_TOPK + kk
            top_scr[slot:slot + 1, :] = m
            code_scr[slot:slot + 1, :] = jnp.max(jnp.where(sel, code, -1.0), axis=0, keepdims=True)
            cand = jnp.where(sel, -jnp.inf, cand)
        top = top_scr[h * PEER_TOPK:(h + 1) * PEER_TOPK, :]
        e = jnp.exp(top - jnp.max(top, axis=0, keepdims=True))
        top_scr[h * PEER_TOPK:(h + 1) * PEER_TOPK, :] = e / jnp.sum(e, axis=0, keepdims=True)
    code_t = code_scr[...].T
    first = jnp.floor(code_t * (1.0 / PEER_NKEYS))
    a_ref[...] = first.astype(I32)
    b_ref[...] = (code_t - first * float(PEER_NKEYS)).astype(I32)
    g_ref[...] = top_scr[...].T


def _route(hn2, wq, keys):
    n, d = hn2.shape
    t = min(256, n)
    qd = wq.shape[1]
    out = jax.ShapeDtypeStruct((n, PEER_SLOTS), I32)
    return pl.pallas_call(
        _route_kernel,
        grid=(n // t,),
        in_specs=[pl.BlockSpec((t, d), lambda i: (i, 0)),
                  pl.BlockSpec((d, qd), lambda i: (0, 0)),
                  pl.BlockSpec(keys.shape, lambda i: (0, 0, 0))],
        out_specs=[pl.BlockSpec((t, PEER_SLOTS), lambda i: (i, 0))] * 3,
        out_shape=[out, out, jax.ShapeDtypeStruct((n, PEER_SLOTS), F32)],
        scratch_shapes=[pltpu.VMEM((PEER_SLOTS, t), F32),
                        pltpu.VMEM((PEER_SLOTS, t), F32)],
        compiler_params=_cparams("arbitrary"),
    )(hn2, wq, keys)


PAIR = 2 * PEER_NKEYS
DOWN_PAIRS = 8
UP_KEYS = 16
UP_TOKENS = 512
TOKEN_UNROLL = 16
DENSE_PITCH = PEER_NKEYS + 8


def _peer_down_kernel(x_ref, dn_ref, a_ref, b_ref, pre_ref):
    j = pl.program_id(1)

    @pl.when(j == 0)
    def _():
        pre_ref[...] = jnp.zeros(pre_ref.shape, F32)

    x = x_ref[...]
    a = a_ref[...]
    b = b_ref[...]
    pre = pre_ref[...]
    for q in range(DOWN_PAIRS):
        p = lax.dot_general(x, dn_ref[q * PAIR:(q + 1) * PAIR, :], NT_DIMS,
                            preferred_element_type=F32)
        for half in range(2):
            g = jnp.take_along_axis(p[:, half * LANES:(half + 1) * LANES], b, axis=1)
            pre = jnp.where(a == 2 * (DOWN_PAIRS * j + q) + half, g, pre)
    pre_ref[...] = pre


def _peer_down(hn2, down16, aidx, bidx):
    n, d = hn2.shape
    t = min(1024, n)
    slot_spec = pl.BlockSpec((t, PEER_SLOTS), lambda i, j: (i, 0))
    return pl.pallas_call(
        _peer_down_kernel,
        grid=(n // t, down16.shape[0] // (DOWN_PAIRS * PAIR)),
        in_specs=[pl.BlockSpec((t, d), lambda i, j: (i, 0)),
                  pl.BlockSpec((DOWN_PAIRS * PAIR, d), lambda i, j: (j, 0)),
                  slot_spec, slot_spec],
        out_specs=slot_spec,
        out_shape=jax.ShapeDtypeStruct((n, PEER_SLOTS), F32),
        compiler_params=_cparams("arbitrary", "arbitrary"),
    )(hn2, down16, aidx, bidx)


U32 = jnp.uint32
HI_HALF = 0xFFFF0000


def _bf16_bits(x):
    return lax.bitcast_convert_type(x.astype(BF16).astype(F32), U32)


def _peer_up_kernel(pre_ref, g_ref, a_ref, b_ref, up_ref, h_ref, mod_ref, o_ref,
                    act_scr, dense_scr):
    j = pl.program_id(1)
    t = pre_ref.shape[0]
    half = t // 2
    nk = PEER_NKEYS

    @pl.when(j == 0)
    def _():
        pre = pre_ref[...]
        act_scr[...] = 0.5 * pre * (1.0 + lax.erf(pre * (1.0 / math.sqrt(2.0)))) * g_ref[...]
        o_ref[...] = jnp.zeros(o_ref.shape, F32)
        row = lax.broadcasted_iota(I32, (nk, PEER_SLOTS), 0)

        def scatter(a_row, b_row, c_row):
            xa = jnp.where(row == a_row, c_row, 0.0).astype(BF16)
            yb = jnp.where(row == b_row, 1.0, 0.0).astype(BF16)
            return lax.dot_general(xa, yb, NT_DIMS, preferred_element_type=F32)

        def body(i, carry):
            for grp in range(TOKEN_UNROLL // 8):
                base = pl.multiple_of(i * TOKEN_UNROLL + grp * 8, 8)
                lo = [r[pl.ds(base, 8), :] for r in (a_ref, b_ref, act_scr)]
                hi = [r[pl.ds(base + half, 8), :] for r in (a_ref, b_ref, act_scr)]
                for u in range(8):
                    d_lo = scatter(*[v[u:u + 1, :] for v in lo])
                    d_hi = scatter(*[v[u:u + 1, :] for v in hi])
                    dense_scr[pl.ds(pl.multiple_of((base + u) * DENSE_PITCH, 8), nk), :] = (
                        lax.shift_right_logical(_bf16_bits(d_lo), U32(16)) | _bf16_bits(d_hi))
            return carry

        lax.fori_loop(0, half // TOKEN_UNROLL, body, 0)

    words = [dense_scr[pl.ds(UP_KEYS * j + u, half, stride=DENSE_PITCH), :] for u in range(UP_KEYS)]
    lo = jnp.concatenate([lax.bitcast_convert_type(lax.shift_left(w, U32(16)), F32) for w in words], axis=1)
    hi = jnp.concatenate([lax.bitcast_convert_type(w & U32(HI_HALF), F32) for w in words], axis=1)
    lhs = jnp.concatenate([lo, hi], axis=0).astype(BF16)
    o_ref[...] += jnp.dot(lhs, up_ref[...], preferred_element_type=F32)

    @pl.when(j == pl.num_programs(1) - 1)
    def _():
        o_ref[...] = h_ref[...] + mod_ref[0][5:6, :] * o_ref[...]


def _peer_up(pre, gate, aidx, bidx, up16, h1, mod3, seq):
    n, d = h1.shape
    t = min(UP_TOKENS, seq)
    slot_spec = pl.BlockSpec((t, PEER_SLOTS), lambda i, j: (i, 0))
    return pl.pallas_call(
        _peer_up_kernel,
        grid=(n // t, up16.shape[0] // (UP_KEYS * PEER_NKEYS)),
        in_specs=[slot_spec, slot_spec, slot_spec, slot_spec,
                  pl.BlockSpec((UP_KEYS * PEER_NKEYS, d), lambda i, j: (j, 0)),
                  pl.BlockSpec((t, d), lambda i, j: (i, 0)),
                  pl.BlockSpec((1, 6, d), lambda i, j: (i * t // seq, 0, 0))],
        out_specs=pl.BlockSpec((t, d), lambda i, j: (i, 0)),
        out_shape=jax.ShapeDtypeStruct((n, d), F32),
        scratch_shapes=[pltpu.VMEM((t, PEER_SLOTS), F32),
                        pltpu.VMEM((t // 2 * DENSE_PITCH, PEER_NKEYS), U32)],
        compiler_params=_cparams("arbitrary", "arbitrary"),
    )(pre, gate, aidx, bidx, up16, h1, mod3)


def _pad_lanes(v):
    return jnp.pad(v.astype(F32), (0, LANES - v.shape[0])).reshape(1, LANES)


def _layer(h2, mod3, l, batch, seq, norm1_w, w_in, q_norm_w, k_norm_w, rel_bias, lambda_q1, lambda_k1,
           lambda_q2, lambda_k2, subln_w, conv_w, conv_b, dt_bias, a_log, d_skip, ssm_norm_w, w_out,
           norm2_w, peer_wq, peer_keys, expert_down, expert_up):
    d = h2.shape[1]
    lam_init = 0.8 - 0.6 * math.exp(-0.3 * l)
    w16 = w_in.astype(BF16)
    w_main = w16[:, :MAIN_COLS]
    w_dt = jnp.pad(w16[:, MAIN_COLS:], ((0, 0), (0, LANES - SSM_HEADS)))
    qn = jnp.tile(q_norm_w.astype(F32) * (HEAD_DIM ** -0.5 * LOG2E), 2).reshape(1, LANES)
    kn = jnp.tile(k_norm_w.astype(F32), 2).reshape(1, LANES)
    proj, dt_raw = _in_proj(h2, norm1_w.reshape(1, d), mod3, w_main, w_dt, qn, kn, seq)

    lamv = jnp.pad(jnp.stack([lambda_q1, lambda_k1, lambda_q2, lambda_k2]).astype(F32),
                   ((0, 4), (0, LANES - HEAD_DIM)))
    att = _attention(rel_bias.astype(F32).reshape(-1), proj, lamv, subln_w.reshape(1, LANES),
                     batch, seq, lam_init)
    ssm = _ssd(proj, dt_raw, conv_w, conv_b.reshape(1, -1), _pad_lanes(dt_bias), _pad_lanes(a_log),
               _pad_lanes(d_skip), ssm_norm_w.reshape(1, -1), batch, seq)
    h1, hn2 = _out_proj(h2, att, ssm, w_out.astype(BF16), mod3, norm2_w.reshape(1, d), seq)

    keys = peer_keys.astype(BF16).reshape(2 * PEER_HEADS, PEER_NKEYS, -1)
    aidx, bidx, gate = _route(hn2, peer_wq.astype(BF16), keys)
    pre = _peer_down(hn2, expert_down.astype(BF16), aidx, bidx)
    return _peer_up(pre, gate, aidx, bidx, expert_up.astype(BF16), h1, mod3, seq)


def kernel(x, c, ada_w, ada_b, norm1_w, w_in, q_norm_w, k_norm_w, rel_bias, lambda_q1, lambda_k1, lambda_q2, lambda_k2, subln_w, conv_w, conv_b, dt_bias, a_log, d_skip, ssm_norm_w, w_out, norm2_w, peer_wq, peer_keys, expert_down, expert_up):
    batch, seq, d = x.shape
    depth = ada_w.shape[0]
    h2 = x.reshape(batch * seq, d)
    c_pad = jnp.pad(c, ((0, 8 - batch), (0, 0)))
    for l in range(depth):
        mod = _ada(c_pad, ada_w[l], ada_b[l].reshape(1, -1))
        mod3 = mod[:batch].reshape(batch, 6, d)
        h2 = _layer(h2, mod3, l, batch, seq, norm1_w[l], w_in[l], q_norm_w[l], k_norm_w[l], rel_bias,
                    lambda_q1[l], lambda_k1[l], lambda_q2[l], lambda_k2[l], subln_w[l], conv_w[l],
                    conv_b[l], dt_bias[l], a_log[l], d_skip[l], ssm_norm_w[l], w_out[l], norm2_w[l],
                    peer_wq[l], peer_keys[l], expert_down[l], expert_up[l])
    return h2.reshape(batch, seq, d)
```

```python
import functools
import math

import jax
import jax.numpy as jnp
from jax import lax
from jax.experimental import pallas as pl
from jax.experimental.pallas import tpu as pltpu

F32 = jnp.float32
BF16 = jnp.bfloat16
I32 = jnp.int32

LANES = 128
VMEM_LIMIT = 56 * 1024 * 1024

NORM_EPS = 1e-6
HEAD_DIM = 64
ATT_HEADS = 8
ATT_WIDTH = 1024
SSM_WIDTH = 1024
SSM_HEADS = 16
SSM_GROUPS = 2
SSM_STATE = 128
SSM_CONV = 4
SSM_CHUNK = 128
SSM_BC = 2 * SSM_GROUPS * SSM_STATE
REL_BUCKETS = 32
REL_MAX_DIST = 128
PEER_HEADS = 8
PEER_NKEYS = 128
PEER_TOPK = 16
PEER_SLOTS = PEER_HEADS * PEER_TOPK
MAIN_COLS = 3 * ATT_WIDTH + SSM_WIDTH + SSM_WIDTH + SSM_BC
NEG = -1e30
LOG2E = math.log2(math.e)

NT_DIMS = (((1,), (1,)), ((), ()))


def _cparams(*sem):
    return pltpu.CompilerParams(dimension_semantics=sem, vmem_limit_bytes=VMEM_LIMIT)


def _sigmoid(x):
    return 1.0 / (1.0 + jnp.exp(-x))


def _ada_kernel(c_ref, w_ref, b_ref, o_ref):
    c = c_ref[...]
    sc = (c * _sigmoid(c)).astype(BF16)
    o_ref[...] = jnp.dot(sc, w_ref[...].astype(BF16), preferred_element_type=F32) + b_ref[...]


def _ada(c_pad, ada_w, ada_b):
    rows, d = c_pad.shape
    n = ada_w.shape[1]
    tn = 1536
    return pl.pallas_call(
        _ada_kernel,
        grid=(n // tn,),
        in_specs=[pl.BlockSpec((rows, d), lambda j: (0, 0)),
                  pl.BlockSpec((d, tn), lambda j: (0, j)),
                  pl.BlockSpec((1, tn), lambda j: (0, j))],
        out_specs=pl.BlockSpec((rows, tn), lambda j: (0, j)),
        out_shape=jax.ShapeDtypeStruct((rows, n), F32),
        compiler_params=_cparams("arbitrary"),
    )(c_pad, ada_w, ada_b)


def _group_rms(blk, w_row, lo):
    sq = blk * blk
    s_all = jnp.sum(sq, axis=1, keepdims=True)
    s_lo = jnp.sum(jnp.where(lo, sq, 0.0), axis=1, keepdims=True)
    s = jnp.where(lo, s_lo, s_all - s_lo)
    return blk * lax.rsqrt(s * (1.0 / HEAD_DIM) + NORM_EPS) * w_row


def _in_proj_kernel(x_ref, nw_ref, mod_ref, w_ref, wdt_ref, qn_ref, kn_ref,
                    o_ref, dt_ref, hn_scr, raw_scr, *, tn, ntiles):
    j = pl.program_id(1)
    qk_tiles = 2 * ATT_WIDTH // tn

    def finish(k):
        raw = raw_scr[k % 2]
        if k < qk_tiles:
            lo = lax.broadcasted_iota(I32, (1, LANES), 1) < HEAD_DIM
            w_row = qn_ref[...] if k < qk_tiles // 2 else kn_ref[...]
            for cb in range(tn // LANES):
                blk = raw[:, cb * LANES:(cb + 1) * LANES]
                o_ref[:, cb * LANES:(cb + 1) * LANES] = _group_rms(blk, w_row, lo).astype(BF16)
        else:
            o_ref[...] = raw.astype(BF16)

    for step in range(ntiles + 1):
        @pl.when(j == step)
        def _(step=step):
            if step == 0:
                mod = mod_ref[0]
                rows = x_ref.shape[0] // 2
                for r0 in (0, rows):
                    x = x_ref[r0:r0 + rows, :]
                    y = x * lax.rsqrt(jnp.mean(x * x, axis=1, keepdims=True) + NORM_EPS) * nw_ref[...]
                    hn = (y * (1.0 + mod[1:2, :]) + mod[0:1, :]).astype(BF16)
                    hn_scr[r0:r0 + rows, :] = hn
                    dt_ref[r0:r0 + rows, :] = jnp.dot(hn, wdt_ref[...], preferred_element_type=F32)
                    raw_scr[0, r0:r0 + rows, :] = jnp.dot(hn, w_ref[...], preferred_element_type=F32)
            elif step < ntiles:
                raw_scr[step % 2] = jnp.dot(hn_scr[...], w_ref[...], preferred_element_type=F32)
            if step > 0:
                finish(step - 1)


def _in_proj(x2, norm_w, mod3, w_main, w_dt, qn, kn, seq):
    n, d = x2.shape
    tm = min(1024, seq)
    tn = 512
    ntiles = MAIN_COLS // tn
    return pl.pallas_call(
        functools.partial(_in_proj_kernel, tn=tn, ntiles=ntiles),
        grid=(n // tm, ntiles + 1),
        in_specs=[pl.BlockSpec((tm, d), lambda i, j: (i, 0)),
                  pl.BlockSpec((1, d), lambda i, j: (0, 0)),
                  pl.BlockSpec((1, 6, d), lambda i, j: (i * tm // seq, 0, 0)),
                  pl.BlockSpec((d, tn), lambda i, j: (0, jnp.minimum(j, ntiles - 1))),
                  pl.BlockSpec((d, LANES), lambda i, j: (0, 0)),
                  pl.BlockSpec((1, LANES), lambda i, j: (0, 0)),
                  pl.BlockSpec((1, LANES), lambda i, j: (0, 0))],
        out_specs=[pl.BlockSpec((tm, tn), lambda i, j: (i, jnp.maximum(j - 1, 0))),
                   pl.BlockSpec((tm, LANES), lambda i, j: (i, 0))],
        out_shape=[jax.ShapeDtypeStruct((n, MAIN_COLS), BF16),
                   jax.ShapeDtypeStruct((n, LANES), F32)],
        scratch_shapes=[pltpu.VMEM((tm, d), BF16),
                        pltpu.VMEM((2, tm, tn), F32)],
        compiler_params=_cparams("arbitrary", "arbitrary"),
    )(x2, norm_w, mod3, w_main, w_dt, qn, kn)


ATT_GROUP = 2
ATT_UNROLL = 4
ATT_CHUNK = 256
VT_ROWS = LANES + 16


def _attn_kernel(relb_ref, q_ref, k_ref, vt_ref, lamv_ref, subw_ref, o_ref,
                 q2t_scr, acc_scr, bias_scr, s_scr, smax_scr, *, t, lam_init):
    hp = pl.program_id(1)
    qi = pl.program_id(2)
    tq = 2 * t
    nchunk = vt_ref.shape[0] // ATT_GROUP

    @pl.when(qi == 0)
    def _():
        kk = lax.broadcasted_iota(I32, (t, t), 0)
        qq = lax.broadcasted_iota(I32, (t, t), 1)
        max_exact = REL_BUCKETS // 2
        buckets = []
        for off in (0, t):
            nn = jnp.maximum(qq - kk + off, 0)
            nf = jnp.maximum(nn, 1).astype(F32)
            large = max_exact + (jnp.log(nf / max_exact) / math.log(REL_MAX_DIST / max_exact)
                                 * (REL_BUCKETS - max_exact)).astype(I32)
            buckets.append(jnp.where(nn < max_exact, nn, jnp.minimum(large, REL_BUCKETS - 1)))
        zeros = jnp.zeros((t, t), F32)
        masked = jnp.full((t, t), NEG, F32)
        for hh in range(ATT_GROUP):
            head = hp * ATT_GROUP + hh
            for m in range(2):
                far = relb_ref[(REL_BUCKETS - 1) * 2 * ATT_HEADS + head * 2 + m]
                diag, sub = zeros, zeros
                for b in range(REL_BUCKETS - 1):
                    delta = (relb_ref[b * 2 * ATT_HEADS + head * 2 + m] - far) * LOG2E
                    diag = jnp.where(buckets[0] == b, delta, diag)
                    sub = jnp.where(buckets[1] == b, delta, sub)
                diag = jnp.where(qq >= kk, diag, NEG)
                for tile, (first, last) in enumerate(((sub, zeros), (diag, sub), (masked, diag))):
                    bias_scr[hh, tile, :, m * tq:m * tq + t] = first
                    bias_scr[hh, tile, :, m * tq + t:(m + 1) * tq] = last

    d_lo = lax.broadcasted_iota(I32, (LANES, tq), 0) < HEAD_DIM
    for hh in range(ATT_GROUP):
        qt = q_ref[:, hh * LANES:(hh + 1) * LANES].astype(F32).T
        q2t_scr[hh, :, 0:tq] = jnp.where(d_lo, qt, 0.0).astype(BF16)
        q2t_scr[hh, :, tq:2 * tq] = jnp.where(d_lo, 0.0, qt).astype(BF16)
        acc_scr[hh] = jnp.zeros((VT_ROWS, 2 * tq), F32)

    def scores(c, hh):
        k_c = k_ref[pl.ds(pl.multiple_of(c * t, t), t), hh * LANES:(hh + 1) * LANES]
        return jnp.dot(k_c, q2t_scr[hh], preferred_element_type=F32)

    def issue(c, slot, tile):
        for hh in range(ATT_GROUP):
            s = scores(c, hh)
            if tile is not None:
                s = s + bias_scr[hh, tile]
            s_scr[slot, hh] = s
            row = slot * ATT_GROUP + hh
            smax_scr[row:row + 1, :] = jnp.max(s, axis=0, keepdims=True)

    def consume(c, carry, slot):
        out = []
        for hh in range(ATT_GROUP):
            m_prev = carry[hh]
            row = slot * ATT_GROUP + hh
            m_new = jnp.maximum(m_prev, smax_scr[row:row + 1, :])
            alpha = jnp.exp2(m_prev - m_new)
            p = jnp.exp2(s_scr[slot, hh] - m_new).astype(BF16)
            pv = jnp.dot(vt_ref[hh * nchunk + c], p, preferred_element_type=F32)
            acc_scr[hh] = alpha * acc_scr[hh] + pv
            out.append(m_new)
        return tuple(out)

    def run(first, tiles, more, reissue, carry):
        if reissue:
            issue(first, 0, tiles[0])
        for k in range(len(tiles)):
            if k + 1 < len(tiles):
                issue(first + k + 1, (k + 1) % 2, tiles[k + 1])
            elif more:
                issue(first + k + 1, (k + 1) % 2, None)
            carry = consume(first + k, carry, k % 2)
        return carry

    nfar = jnp.maximum(2 * qi - 1, 0)
    ntrip = nfar // ATT_UNROLL
    issue(0, 0, None)

    def trip(i, carry):
        return run(i * ATT_UNROLL, [None] * ATT_UNROLL, True, False, carry)

    init = tuple(jnp.full((1, 2 * tq), NEG, F32) for _ in range(ATT_GROUP))
    carry = lax.fori_loop(0, ntrip, trip, init)
    rest = ntrip * ATT_UNROLL
    tails = [functools.partial(run, rest, [None] * r + [0, 1, 2], False, False)
             for r in range(1, ATT_UNROLL, 2)]
    tails.append(functools.partial(run, rest, [1, 2], False, True))
    carry = lax.switch(jnp.where(qi == 0, len(tails) - 1, (nfar - rest) // 2), tails, carry)

    lv = lamv_ref[...]
    lam = (jnp.exp(jnp.sum(lv[0:1, :] * lv[1:2, :], axis=1, keepdims=True))
           - jnp.exp(jnp.sum(lv[2:3, :] * lv[3:4, :], axis=1, keepdims=True)) + lam_init)
    for hh in range(ATT_GROUP):
        acc = acc_scr[hh, 0:LANES, :] * (1.0 / acc_scr[hh, LANES:LANES + 1, :])
        o = (acc[:, 0:tq] - lam * acc[:, tq:2 * tq]).T
        o = o * lax.rsqrt(jnp.mean(o * o, axis=1, keepdims=True) + NORM_EPS) * subw_ref[...]
        o_ref[:, hh * LANES:(hh + 1) * LANES] = (o * (1.0 - lam_init)).astype(BF16)


def _attention(relb, proj, lamv, subw, batch, seq, lam_init):
    n = batch * seq
    t = ATT_CHUNK
    tq = 2 * t
    nq = seq // tq
    nk = seq // t
    gw = ATT_GROUP * LANES
    ngroups = ATT_HEADS // ATT_GROUP
    vt = proj[:, 2 * ATT_WIDTH:3 * ATT_WIDTH].reshape(batch, nk, t, ATT_HEADS, LANES)
    vt = vt.transpose(0, 3, 1, 4, 2).reshape(batch * ATT_HEADS * nk, LANES, t)
    ones_rows = jnp.zeros((vt.shape[0], VT_ROWS - LANES, t), BF16).at[:, 0, :].set(1.0)
    vt = jnp.concatenate([vt, ones_rows], axis=1)
    return pl.pallas_call(
        functools.partial(_attn_kernel, t=t, lam_init=lam_init),
        grid=(batch, ngroups, nq),
        in_specs=[pl.BlockSpec(memory_space=pltpu.SMEM),
                  pl.BlockSpec((tq, gw), lambda b, g, i: (b * nq + i, g)),
                  pl.BlockSpec((seq, gw), lambda b, g, i: (b, ngroups + g)),
                  pl.BlockSpec((ATT_GROUP * nk, VT_ROWS, t), lambda b, g, i: (b * ngroups + g, 0, 0)),
                  pl.BlockSpec((8, LANES), lambda b, g, i: (0, 0)),
                  pl.BlockSpec((1, LANES), lambda b, g, i: (0, 0))],
        out_specs=pl.BlockSpec((tq, gw), lambda b, g, i: (b * nq + i, g)),
        out_shape=jax.ShapeDtypeStruct((n, ATT_WIDTH), BF16),
        scratch_shapes=[pltpu.VMEM((ATT_GROUP, LANES, 2 * tq), BF16),
                        pltpu.VMEM((ATT_GROUP, VT_ROWS, 2 * tq), F32),
                        pltpu.VMEM((ATT_GROUP, 3, t, 2 * tq), F32),
                        pltpu.VMEM((2, ATT_GROUP, t, 2 * tq), F32),
                        pltpu.VMEM((8, 2 * tq), F32)],
        compiler_params=_cparams("arbitrary", "arbitrary", "arbitrary"),
    )(relb, proj, proj, vt, lamv, subw)


def _split3(x):
    hi = x.astype(BF16)
    r1 = x - hi.astype(F32)
    mid = r1.astype(BF16)
    lo = (r1 - mid.astype(F32)).astype(BF16)
    return hi, mid, lo


def _ssd_kernel(z_ref, xs_ref, bc_ref, dt_ref, cw_ref, cb_ref, dtb_ref, alog_ref, dskip_ref, nw_ref,
                o_ref, xpad_scr, state_scr):
    L = SSM_CHUNK
    W = SSM_WIDTH
    P2 = LANES
    nblk = W // P2
    gw = W // SSM_GROUPS

    @pl.when(pl.program_id(0) == 0)
    def _():
        xpad_scr[0:8, :] = jnp.zeros((8, W + SSM_BC), F32)
        state_scr[...] = jnp.zeros((SSM_STATE, W), F32)

    xpad_scr[8:8 + L, 0:W] = xs_ref[...].astype(F32)
    xpad_scr[8:8 + L, W:W + SSM_BC] = bc_ref[...].astype(F32)
    conv = cb_ref[...] + cw_ref[0:1, :] * xpad_scr[5:5 + L, :]
    for kk in range(1, SSM_CONV):
        conv = conv + cw_ref[kk:kk + 1, :] * xpad_scr[5 + kk:5 + kk + L, :]
    xpad_scr[0:8, :] = xpad_scr[L:L + 8, :]
    u = conv * _sigmoid(conv)

    dtr = dt_ref[...] + dtb_ref[...]
    dt = jnp.maximum(dtr, 0.0) + jnp.log1p(jnp.exp(-jnp.abs(dtr)))
    a = -jnp.exp(alog_ref[...])
    da = dt * a

    ri = lax.broadcasted_iota(I32, (L, L), 0)
    ci = lax.broadcasted_iota(I32, (L, L), 1)
    tril = ri >= ci
    tri = jnp.where(tril, 1.0, 0.0).astype(BF16)
    hi, mid, lo3 = _split3(da)
    a_cs = (jnp.dot(tri, hi, preferred_element_type=F32) + jnp.dot(tri, mid, preferred_element_type=F32)
            + jnp.dot(tri, lo3, preferred_element_type=F32))
    a_cs_t = a_cs.T
    a_last = a_cs[L - 1:L, :]
    e_cs = jnp.exp(a_cs)
    dt_ds = dt * jnp.exp(a_last - a_cs)

    lane_lo = lax.broadcasted_iota(I32, (1, P2), 1) < HEAD_DIM

    def expand(mat, i):
        return jnp.where(lane_lo, mat[:, 2 * i:2 * i + 1], mat[:, 2 * i + 1:2 * i + 2])

    y_blocks = []
    for g in range(SSM_GROUPS):
        bm = u[:, W + g * SSM_STATE:W + (g + 1) * SSM_STATE]
        cm = u[:, W + (SSM_GROUPS + g) * SSM_STATE:W + (SSM_GROUPS + g + 1) * SSM_STATE]
        bm16 = bm.astype(BF16)
        cm16 = cm.astype(BF16)
        cb = lax.dot_general(cm16, bm16, NT_DIMS, preferred_element_type=F32)
        st_g = state_scr[:, g * gw:(g + 1) * gw]
        y_off = jnp.dot(cm16, st_g.astype(BF16), preferred_element_type=F32)
        xd_blocks = []
        for ib in range(nblk // SSM_GROUPS):
            i = g * (nblk // SSM_GROUPS) + ib
            xs_blk = u[:, i * P2:(i + 1) * P2]
            xc = xs_blk * expand(dt, i)
            yd = jnp.zeros((L, P2), F32)
            for hh in range(2):
                head = 2 * i + hh
                seg = a_cs[:, head:head + 1] - a_cs_t[head:head + 1, :]
                wmat = (cb * jnp.where(tril, jnp.exp(seg), 0.0)).astype(BF16)
                keep = lane_lo if hh == 0 else jnp.logical_not(lane_lo)
                yd = yd + jnp.dot(wmat, jnp.where(keep, xc, 0.0).astype(BF16), preferred_element_type=F32)
            y = yd + y_off[:, ib * P2:(ib + 1) * P2] * expand(e_cs, i) + expand(dskip_ref[...], i) * xs_blk
            zf = z_ref[:, i * P2:(i + 1) * P2].astype(F32)
            y_blocks.append(y * (zf * _sigmoid(zf)))
            xd_blocks.append((xs_blk * expand(dt_ds, i)).astype(BF16))
        xd = jnp.concatenate(xd_blocks, axis=1)
        st_new = jnp.dot(bm.T.astype(BF16), xd, preferred_element_type=F32)
        decay = jnp.concatenate([expand(jnp.exp(a_last), g * (nblk // SSM_GROUPS) + ib)
                                 for ib in range(nblk // SSM_GROUPS)], axis=1)
        state_scr[:, g * gw:(g + 1) * gw] = st_g * decay + st_new

    per_g = nblk // SSM_GROUPS
    for g in range(SSM_GROUPS):
        blks = y_blocks[g * per_g:(g + 1) * per_g]
        ss = sum(jnp.sum(b * b, axis=1, keepdims=True) for b in blks) * (1.0 / gw)
        inv = lax.rsqrt(ss + NORM_EPS)
        for ib, b in enumerate(blks):
            i = g * per_g + ib
            o_ref[:, i * P2:(i + 1) * P2] = (b * inv * nw_ref[:, i * P2:(i + 1) * P2]).astype(BF16)


def _ssd_batched_kernel(z_ref, xs_ref, bc_ref, dt_ref, cw_ref, cb_ref, dtb_ref, alog_ref, dskip_ref, nw_ref,
                        o_ref, xpad_scr, state_scr):
    for b in range(z_ref.shape[0]):
        _ssd_kernel(z_ref.at[b], xs_ref.at[b], bc_ref.at[b], dt_ref.at[b], cw_ref, cb_ref, dtb_ref,
                    alog_ref, dskip_ref, nw_ref, o_ref.at[b], xpad_scr.at[b], state_scr.at[b])


def _ssd(proj, dt_raw, conv_w, conv_b, dt_bias, a_log, d_skip, norm_w, batch, seq):
    L = SSM_CHUNK
    cd = SSM_WIDTH + SSM_BC
    proj3 = proj.reshape(batch, seq, MAIN_COLS)
    dt3 = dt_raw.reshape(batch, seq, LANES)
    out = pl.pallas_call(
        _ssd_batched_kernel,
        grid=(seq // L,),
        in_specs=[pl.BlockSpec((batch, L, SSM_WIDTH), lambda c: (0, c, 3)),
                  pl.BlockSpec((batch, L, SSM_WIDTH), lambda c: (0, c, 4)),
                  pl.BlockSpec((batch, L, SSM_BC), lambda c: (0, c, 10)),
                  pl.BlockSpec((batch, L, LANES), lambda c: (0, c, 0)),
                  pl.BlockSpec((SSM_CONV, cd), lambda c: (0, 0)),
                  pl.BlockSpec((1, cd), lambda c: (0, 0)),
                  pl.BlockSpec((1, LANES), lambda c: (0, 0)),
                  pl.BlockSpec((1, LANES), lambda c: (0, 0)),
                  pl.BlockSpec((1, LANES), lambda c: (0, 0)),
                  pl.BlockSpec((1, SSM_WIDTH), lambda c: (0, 0))],
        out_specs=pl.BlockSpec((batch, L, SSM_WIDTH), lambda c: (0, c, 0)),
        out_shape=jax.ShapeDtypeStruct((batch, seq, SSM_WIDTH), BF16),
        scratch_shapes=[pltpu.VMEM((batch, L + 8, cd), F32),
                        pltpu.VMEM((batch, SSM_STATE, SSM_WIDTH), F32)],
        compiler_params=_cparams("arbitrary"),
    )(proj3, proj3, proj3, dt3, conv_w, conv_b, dt_bias, a_log, d_skip, norm_w)
    return out.reshape(batch * seq, SSM_WIDTH)


def _out_proj_kernel(x_ref, att_ref, ssm_ref, w_ref, mod_ref, nw_ref, h_ref, hn_ref):
    mix = (jnp.dot(att_ref[...], w_ref[0:ATT_WIDTH, :], preferred_element_type=F32)
           + jnp.dot(ssm_ref[...], w_ref[ATT_WIDTH:, :], preferred_element_type=F32))
    mod = mod_ref[0]
    h1 = x_ref[...] + mod[2:3, :] * mix
    h_ref[...] = h1
    y = h1 * lax.rsqrt(jnp.mean(h1 * h1, axis=1, keepdims=True) + NORM_EPS) * nw_ref[...]
    hn_ref[...] = (y * (1.0 + mod[4:5, :]) + mod[3:4, :]).astype(BF16)


def _out_proj(x2, att, ssm, w_out, mod3, norm2_w, seq):
    n, d = x2.shape
    tm = min(256, seq)
    return pl.pallas_call(
        _out_proj_kernel,
        grid=(n // tm,),
        in_specs=[pl.BlockSpec((tm, d), lambda i: (i, 0)),
                  pl.BlockSpec((tm, ATT_WIDTH), lambda i: (i, 0)),
                  pl.BlockSpec((tm, SSM_WIDTH), lambda i: (i, 0)),
                  pl.BlockSpec((ATT_WIDTH + SSM_WIDTH, d), lambda i: (0, 0)),
                  pl.BlockSpec((1, 6, d), lambda i: (i * tm // seq, 0, 0)),
                  pl.BlockSpec((1, d), lambda i: (0, 0))],
        out_specs=[pl.BlockSpec((tm, d), lambda i: (i, 0)),
                   pl.BlockSpec((tm, d), lambda i: (i, 0))],
        out_shape=[jax.ShapeDtypeStruct((n, d), F32),
                   jax.ShapeDtypeStruct((n, d), BF16)],
        compiler_params=_cparams("arbitrary"),
    )(x2, att, ssm, w_out, mod3, norm2_w)


def _topk_rows(s, k, rows):
    row = lax.broadcasted_iota(I32, s.shape, 0).astype(F32)
    vals, idxs = [], []
    for _ in range(k):
        m, idx = _argmax_rows(s, rows)
        s = jnp.where(row == idx, -jnp.inf, s)
        vals.append(m)
        idxs.append(idx)
    return vals, idxs


def _argmax_rows(s, rows):
    vs = [s[8 * i:8 * i + 8] for i in range(rows // 8)]
    sub = lax.broadcasted_iota(I32, vs[0].shape, 0).astype(F32)
    rs = [sub + float(8 * i) for i in range(rows // 8)]
    while len(vs) > 1:
        nv, nr = [], []
        for i in range(0, len(vs) - 1, 2):
            take_b = vs[i + 1] > vs[i]
            nv.append(jnp.where(take_b, vs[i + 1], vs[i]))
            nr.append(jnp.where(take_b, rs[i + 1], rs[i]))
        if len(vs) % 2:
            nv.append(vs[-1])
            nr.append(rs[-1])
        vs, rs = nv, nr
    m = jnp.max(vs[0], axis=0, keepdims=True)
    idx = jnp.min(jnp.where(vs[0] == m, rs[0], float(rows)), axis=0, keepdims=True)
    return m, idx


_CAND = [(i, j) for i in range(PEER_TOPK) for j in range(PEER_TOPK) if (i + 1) * (j + 1) <= PEER_TOPK]


def _route_kernel(hn_ref, wq_ref, keys_ref, a_ref, b_ref, g_ref, top_scr, code_scr):
    t = hn_ref.shape[0]
    qp = jnp.dot(hn_ref[...], wq_ref[...], preferred_element_type=F32).astype(BF16)
    ncand = len(_CAND)
    pad = (-ncand) % 8
    for h in range(PEER_HEADS):
        sub = []
        for c in range(2):
            hc = 2 * h + c
            sc = lax.dot_general(keys_ref[hc], qp[:, hc * LANES:(hc + 1) * LANES], NT_DIMS,
                                 preferred_element_type=F32)
            sub.append(_topk_rows(sc, PEER_TOPK, PEER_NKEYS))
        (s1, i1), (s2, i2) = sub
        cand = jnp.concatenate([s1[i] + s2[j] for i, j in _CAND]
                               + [jnp.full((pad, t), -jnp.inf, F32)], axis=0)
        a_hi = [v * float(PEER_NKEYS) for v in i1]
        code = jnp.concatenate([a_hi[i] + i2[j] for i, j in _CAND] + [jnp.zeros((pad, t), F32)], axis=0)
        row = lax.broadcasted_iota(I32, cand.shape, 0).astype(F32)
        for kk in range(PEER_TOPK):
            m, idx = _argmax_rows(cand, ncand + pad)
            sel = row == idx
            slot = h * PEER_TOPK + kk
            top_scr[slot:slot + 1, :] = m
            code_scr[slot:slot + 1, :] = jnp.max(jnp.where(sel, code, -1.0), axis=0, keepdims=True)
            cand = jnp.where(sel, -jnp.inf, cand)
        top = top_scr[h * PEER_TOPK:(h + 1) * PEER_TOPK, :]
        e = jnp.exp(top - jnp.max(top, axis=0, keepdims=True))
        top_scr[h * PEER_TOPK:(h + 1) * PEER_TOPK, :] = e / jnp.sum(e, axis=0, keepdims=True)
    code_t = code_scr[...].T
    first = jnp.floor(code_t * (1.0 / PEER_NKEYS))
    a_ref[...] = first.astype(I32)
    b_ref[...] = (code_t - first * float(PEER_NKEYS)).astype(I32)
    g_ref[...] = top_scr[...].T


def _route(hn2, wq, keys):
    n, d = hn2.shape
    t = min(256, n)
    qd = wq.shape[1]
    out = jax.ShapeDtypeStruct((n, PEER_SLOTS), I32)
    return pl.pallas_call(
        _route_kernel,
        grid=(n // t,),
        in_specs=[pl.BlockSpec((t, d), lambda i: (i, 0)),
                  pl.BlockSpec((d, qd), lambda i: (0, 0)),
                  pl.BlockSpec(keys.shape, lambda i: (0, 0, 0))],
        out_specs=[pl.BlockSpec((t, PEER_SLOTS), lambda i: (i, 0))] * 3,
        out_shape=[out, out, jax.ShapeDtypeStruct((n, PEER_SLOTS), F32)],
        scratch_shapes=[pltpu.VMEM((PEER_SLOTS, t), F32),
                        pltpu.VMEM((PEER_SLOTS, t), F32)],
        compiler_params=_cparams("arbitrary"),
    )(hn2, wq, keys)


PAIR = 2 * PEER_NKEYS
DOWN_PAIRS = 8
UP_KEYS = 16
UP_TOKENS = 512
TOKEN_UNROLL = 16
DENSE_PITCH = PEER_NKEYS + 8


def _peer_down_kernel(x_ref, dn_ref, a_ref, b_ref, pre_ref):
    j = pl.program_id(1)

    @pl.when(j == 0)
    def _():
        pre_ref[...] = jnp.zeros(pre_ref.shape, F32)

    x = x_ref[...]
    a = a_ref[...]
    b = b_ref[...]
    pre = pre_ref[...]
    for q in range(DOWN_PAIRS):
        p = lax.dot_general(x, dn_ref[q * PAIR:(q + 1) * PAIR, :], NT_DIMS,
                            preferred_element_type=F32)
        for half in range(2):
            g = jnp.take_along_axis(p[:, half * LANES:(half + 1) * LANES], b, axis=1)
            pre = jnp.where(a == 2 * (DOWN_PAIRS * j + q) + half, g, pre)
    pre_ref[...] = pre


def _peer_down(hn2, down16, aidx, bidx):
    n, d = hn2.shape
    t = min(1024, n)
    slot_spec = pl.BlockSpec((t, PEER_SLOTS), lambda i, j: (i, 0))
    return pl.pallas_call(
        _peer_down_kernel,
        grid=(n // t, down16.shape[0] // (DOWN_PAIRS * PAIR)),
        in_specs=[pl.BlockSpec((t, d), lambda i, j: (i, 0)),
                  pl.BlockSpec((DOWN_PAIRS * PAIR, d), lambda i, j: (j, 0)),
                  slot_spec, slot_spec],
        out_specs=slot_spec,
        out_shape=jax.ShapeDtypeStruct((n, PEER_SLOTS), F32),
        compiler_params=_cparams("arbitrary", "arbitrary"),
    )(hn2, down16, aidx, bidx)


U32 = jnp.uint32
HI_HALF = 0xFFFF0000


def _bf16_bits(x):
    return lax.bitcast_convert_type(x.astype(BF16).astype(F32), U32)


def _peer_up_kernel(pre_ref, g_ref, a_ref, b_ref, up_ref, h_ref, mod_ref, o_ref,
                    act_scr, dense_scr):
    j = pl.program_id(1)
    t = pre_ref.shape[0]
    half = t // 2
    nk = PEER_NKEYS

    @pl.when(j == 0)
    def _():
        pre = pre_ref[...]
        act_scr[...] = 0.5 * pre * (1.0 + lax.erf(pre * (1.0 / math.sqrt(2.0)))) * g_ref[...]
        o_ref[...] = jnp.zeros(o_ref.shape, F32)
        row = lax.broadcasted_iota(I32, (nk, PEER_SLOTS), 0)

        def scatter(a_row, b_row, c_row):
            xa = jnp.where(row == a_row, c_row, 0.0).astype(BF16)
            yb = jnp.where(row == b_row, 1.0, 0.0).astype(BF16)
            return lax.dot_general(xa, yb, NT_DIMS, preferred_element_type=F32)

        def body(i, carry):
            for grp in range(TOKEN_UNROLL // 8):
                base = pl.multiple_of(i * TOKEN_UNROLL + grp * 8, 8)
                lo = [r[pl.ds(base, 8), :] for r in (a_ref, b_ref, act_scr)]
                hi = [r[pl.ds(base + half, 8), :] for r in (a_ref, b_ref, act_scr)]
                for u in range(8):
                    d_lo = scatter(*[v[u:u + 1, :] for v in lo])
                    d_hi = scatter(*[v[u:u + 1, :] for v in hi])
                    dense_scr[pl.ds(pl.multiple_of((base + u) * DENSE_PITCH, 8), nk), :] = (
                        lax.shift_right_logical(_bf16_bits(d_lo), U32(16)) | _bf16_bits(d_hi))
            return carry

        lax.fori_loop(0, half // TOKEN_UNROLL, body, 0)

    words = [dense_scr[pl.ds(UP_KEYS * j + u, half, stride=DENSE_PITCH), :] for u in range(UP_KEYS)]
    lo = jnp.concatenate([lax.bitcast_convert_type(lax.shift_left(w, U32(16)), F32) for w in words], axis=1)
    hi = jnp.concatenate([lax.bitcast_convert_type(w & U32(HI_HALF), F32) for w in words], axis=1)
    lhs = jnp.concatenate([lo, hi], axis=0).astype(BF16)
    o_ref[...] += jnp.dot(lhs, up_ref[...], preferred_element_type=F32)

    @pl.when(j == pl.num_programs(1) - 1)
    def _():
        o_ref[...] = h_ref[...] + mod_ref[0][5:6, :] * o_ref[...]


def _peer_up(pre, gate, aidx, bidx, up16, h1, mod3, seq):
    n, d = h1.shape
    t = min(UP_TOKENS, seq)
    slot_spec = pl.BlockSpec((t, PEER_SLOTS), lambda i, j: (i, 0))
    return pl.pallas_call(
        _peer_up_kernel,
        grid=(n // t, up16.shape[0] // (UP_KEYS * PEER_NKEYS)),
        in_specs=[slot_spec, slot_spec, slot_spec, slot_spec,
                  pl.BlockSpec((UP_KEYS * PEER_NKEYS, d), lambda i, j: (j, 0)),
                  pl.BlockSpec((t, d), lambda i, j: (i, 0)),
                  pl.BlockSpec((1, 6, d), lambda i, j: (i * t // seq, 0, 0))],
        out_specs=pl.BlockSpec((t, d), lambda i, j: (i, 0)),
        out_shape=jax.ShapeDtypeStruct((n, d), F32),
        scratch_shapes=[pltpu.VMEM((t, PEER_SLOTS), F32),
                        pltpu.VMEM((t // 2 * DENSE_PITCH, PEER_NKEYS), U32)],
        compiler_params=_cparams("arbitrary", "arbitrary"),
    )(pre, gate, aidx, bidx, up16, h1, mod3)


def _pad_lanes(v):
    return jnp.pad(v.astype(F32), (0, LANES - v.shape[0])).reshape(1, LANES)


def _layer(h2, mod3, l, batch, seq, norm1_w, w_in, q_norm_w, k_norm_w, rel_bias, lambda_q1, lambda_k1,
           lambda_q2, lambda_k2, subln_w, conv_w, conv_b, dt_bias, a_log, d_skip, ssm_norm_w, w_out,
           norm2_w, peer_wq, peer_keys, expert_down, expert_up):
    d = h2.shape[1]
    lam_init = 0.8 - 0.6 * math.exp(-0.3 * l)
    w16 = w_in.astype(BF16)
    w_main = w16[:, :MAIN_COLS]
    w_dt = jnp.pad(w16[:, MAIN_COLS:], ((0, 0), (0, LANES - SSM_HEADS)))
    qn = jnp.tile(q_norm_w.astype(F32) * (HEAD_DIM ** -0.5 * LOG2E), 2).reshape(1, LANES)
    kn = jnp.tile(k_norm_w.astype(F32), 2).reshape(1, LANES)
    proj, dt_raw = _in_proj(h2, norm1_w.reshape(1, d), mod3, w_main, w_dt, qn, kn, seq)

    lamv = jnp.pad(jnp.stack([lambda_q1, lambda_k1, lambda_q2, lambda_k2]).astype(F32),
                   ((0, 4), (0, LANES - HEAD_DIM)))
    att = _attention(rel_bias.astype(F32).reshape(-1), proj, lamv, subln_w.reshape(1, LANES),
                     batch, seq, lam_init)
    ssm = _ssd(proj, dt_raw, conv_w, conv_b.reshape(1, -1), _pad_lanes(dt_bias), _pad_lanes(a_log),
               _pad_lanes(d_skip), ssm_norm_w.reshape(1, -1), batch, seq)
    h1, hn2 = _out_proj(h2, att, ssm, w_out.astype(BF16), mod3, norm2_w.reshape(1, d), seq)

    keys = peer_keys.astype(BF16).reshape(2 * PEER_HEADS, PEER_NKEYS, -1)
    aidx, bidx, gate = _route(hn2, peer_wq.astype(BF16), keys)
    pre = _peer_down(hn2, expert_down.astype(BF16), aidx, bidx)
    return _peer_up(pre, gate, aidx, bidx, expert_up.astype(BF16), h1, mod3, seq)


def kernel(x, c, ada_w, ada_b, norm1_w, w_in, q_norm_w, k_norm_w, rel_bias, lambda_q1, lambda_k1, lambda_q2, lambda_k2, subln_w, conv_w, conv_b, dt_bias, a_log, d_skip, ssm_norm_w, w_out, norm2_w, peer_wq, peer_keys, expert_down, expert_up):
    batch, seq, d = x.shape
    depth = ada_w.shape[0]
    h2 = x.reshape(batch * seq, d)
    c_pad = jnp.pad(c, ((0, 8 - batch), (0, 0)))
    for l in range(depth):
        mod = _ada(c_pad, ada_w[l], ada_b[l].reshape(1, -1))
        mod3 = mod[:batch].reshape(batch, 6, d)
        h2 = _layer(h2, mod3, l, batch, seq, norm1_w[l], w_in[l], q_norm_w[l], k_norm_w[l], rel_bias,
                    lambda_q1[l], lambda_k1[l], lambda_q2[l], lambda_k2[l], subln_w[l], conv_w[l],
                    conv_b[l], dt_bias[l], a_log[l], d_skip[l], ssm_norm_w[l], w_out[l], norm2_w[l],
                    peer_wq[l], peer_keys[l], expert_down[l], expert_up[l])
    return h2.reshape(batch, seq, d)
```

```python
import functools
import math

import jax
import jax.numpy as jnp
from jax import lax
from jax.experimental import pallas as pl
from jax.experimental.pallas import tpu as pltpu

F32 = jnp.float32
BF16 = jnp.bfloat16
I32 = jnp.int32

LANES = 128
VMEM_LIMIT = 56 * 1024 * 1024

NORM_EPS = 1e-6
HEAD_DIM = 64
ATT_HEADS = 8
ATT_WIDTH = 1024
SSM_WIDTH = 1024
SSM_HEADS = 16
SSM_GROUPS = 2
SSM_STATE = 128
SSM_CONV = 4
SSM_CHUNK = 128
SSM_BC = 2 * SSM_GROUPS * SSM_STATE
REL_BUCKETS = 32
REL_MAX_DIST = 128
PEER_HEADS = 8
PEER_NKEYS = 128
PEER_TOPK = 16
PEER_SLOTS = PEER_HEADS * PEER_TOPK
MAIN_COLS = 3 * ATT_WIDTH + SSM_WIDTH + SSM_WIDTH + SSM_BC
NEG = -1e30
LOG2E = math.log2(math.e)

NT_DIMS = (((1,), (1,)), ((), ()))


def _cparams(*sem):
    return pltpu.CompilerParams(dimension_semantics=sem, vmem_limit_bytes=VMEM_LIMIT)


def _sigmoid(x):
    return 1.0 / (1.0 + jnp.exp(-x))


def _ada_kernel(c_ref, w_ref, b_ref, o_ref):
    c = c_ref[...]
    sc = (c * _sigmoid(c)).astype(BF16)
    o_ref[...] = jnp.dot(sc, w_ref[...].astype(BF16), preferred_element_type=F32) + b_ref[...]


def _ada(c_pad, ada_w, ada_b):
    rows, d = c_pad.shape
    n = ada_w.shape[1]
    tn = 1536
    return pl.pallas_call(
        _ada_kernel,
        grid=(n // tn,),
        in_specs=[pl.BlockSpec((rows, d), lambda j: (0, 0)),
                  pl.BlockSpec((d, tn), lambda j: (0, j)),
                  pl.BlockSpec((1, tn), lambda j: (0, j))],
        out_specs=pl.BlockSpec((rows, tn), lambda j: (0, j)),
        out_shape=jax.ShapeDtypeStruct((rows, n), F32),
        compiler_params=_cparams("arbitrary"),
    )(c_pad, ada_w, ada_b)


def _group_rms(blk, w_row, lo):
    sq = blk * blk
    s_all = jnp.sum(sq, axis=1, keepdims=True)
    s_lo = jnp.sum(jnp.where(lo, sq, 0.0), axis=1, keepdims=True)
    s = jnp.where(lo, s_lo, s_all - s_lo)
    return blk * lax.rsqrt(s * (1.0 / HEAD_DIM) + NORM_EPS) * w_row


def _in_proj_kernel(x_ref, nw_ref, mod_ref, w_ref, wdt_ref, qn_ref, kn_ref,
                    o_ref, dt_ref, hn_scr, raw_scr, *, tn, ntiles):
    j = pl.program_id(1)
    qk_tiles = 2 * ATT_WIDTH // tn

    def finish(k):
        raw = raw_scr[k % 2]
        if k < qk_tiles:
            lo = lax.broadcasted_iota(I32, (1, LANES), 1) < HEAD_DIM
            w_row = qn_ref[...] if k < qk_tiles // 2 else kn_ref[...]
            for cb in range(tn // LANES):
                blk = raw[:, cb * LANES:(cb + 1) * LANES]
                o_ref[:, cb * LANES:(cb + 1) * LANES] = _group_rms(blk, w_row, lo).astype(BF16)
        else:
            o_ref[...] = raw.astype(BF16)

    for step in range(ntiles + 1):
        @pl.when(j == step)
        def _(step=step):
            if step == 0:
                mod = mod_ref[0]
                rows = x_ref.shape[0] // 2
                for r0 in (0, rows):
                    x = x_ref[r0:r0 + rows, :]
                    y = x * lax.rsqrt(jnp.mean(x * x, axis=1, keepdims=True) + NORM_EPS) * nw_ref[...]
                    hn = (y * (1.0 + mod[1:2, :]) + mod[0:1, :]).astype(BF16)
                    hn_scr[r0:r0 + rows, :] = hn
                    dt_ref[r0:r0 + rows, :] = jnp.dot(hn, wdt_ref[...], preferred_element_type=F32)
                    raw_scr[0, r0:r0 + rows, :] = jnp.dot(hn, w_ref[...], preferred_element_type=F32)
            elif step < ntiles:
                raw_scr[step % 2] = jnp.dot(hn_scr[...], w_ref[...], preferred_element_type=F32)
            if step > 0:
                finish(step - 1)


def _in_proj(x2, norm_w, mod3, w_main, w_dt, qn, kn, seq):
    n, d = x2.shape
    tm = min(1024, seq)
    tn = 512
    ntiles = MAIN_COLS // tn
    return pl.pallas_call(
        functools.partial(_in_proj_kernel, tn=tn, ntiles=ntiles),
        grid=(n // tm, ntiles + 1),
        in_specs=[pl.BlockSpec((tm, d), lambda i, j: (i, 0)),
                  pl.BlockSpec((1, d), lambda i, j: (0, 0)),
                  pl.BlockSpec((1, 6, d), lambda i, j: (i * tm // seq, 0, 0)),
                  pl.BlockSpec((d, tn), lambda i, j: (0, jnp.minimum(j, ntiles - 1))),
                  pl.BlockSpec((d, LANES), lambda i, j: (0, 0)),
                  pl.BlockSpec((1, LANES), lambda i, j: (0, 0)),
                  pl.BlockSpec((1, LANES), lambda i, j: (0, 0))],
        out_specs=[pl.BlockSpec((tm, tn), lambda i, j: (i, jnp.maximum(j - 1, 0))),
                   pl.BlockSpec((tm, LANES), lambda i, j: (i, 0))],
        out_shape=[jax.ShapeDtypeStruct((n, MAIN_COLS), BF16),
                   jax.ShapeDtypeStruct((n, LANES), F32)],
        scratch_shapes=[pltpu.VMEM((tm, d), BF16),
                        pltpu.VMEM((2, tm, tn), F32)],
        compiler_params=_cparams("arbitrary", "arbitrary"),
    )(x2, norm_w, mod3, w_main, w_dt, qn, kn)


ATT_GROUP = 2
ATT_UNROLL = 4
ATT_CHUNK = 256
VT_ROWS = LANES + 16


def _attn_kernel(relb_ref, q_ref, k_ref, vt_ref, lamv_ref, subw_ref, o_ref,
                 q2t_scr, acc_scr, bias_scr, s_scr, smax_scr, *, t, lam_init):
    hp = pl.program_id(1)
    qi = pl.program_id(2)
    tq = 2 * t
    nchunk = vt_ref.shape[0] // ATT_GROUP

    @pl.when(qi == 0)
    def _():
        kk = lax.broadcasted_iota(I32, (t, t), 0)
        qq = lax.broadcasted_iota(I32, (t, t), 1)
        max_exact = REL_BUCKETS // 2
        buckets = []
        for off in (0, t):
            nn = jnp.maximum(qq - kk + off, 0)
            nf = jnp.maximum(nn, 1).astype(F32)
            large = max_exact + (jnp.log(nf / max_exact) / math.log(REL_MAX_DIST / max_exact)
                                 * (REL_BUCKETS - max_exact)).astype(I32)
            buckets.append(jnp.where(nn < max_exact, nn, jnp.minimum(large, REL_BUCKETS - 1)))
        zeros = jnp.zeros((t, t), F32)
        masked = jnp.full((t, t), NEG, F32)
        for hh in range(ATT_GROUP):
            head = hp * ATT_GROUP + hh
            for m in range(2):
                far = relb_ref[(REL_BUCKETS - 1) * 2 * ATT_HEADS + head * 2 + m]
                diag, sub = zeros, zeros
                for b in range(REL_BUCKETS - 1):
                    delta = (relb_ref[b * 2 * ATT_HEADS + head * 2 + m] - far) * LOG2E
                    diag = jnp.where(buckets[0] == b, delta, diag)
                    sub = jnp.where(buckets[1] == b, delta, sub)
                diag = jnp.where(qq >= kk, diag, NEG)
                for tile, (first, last) in enumerate(((sub, zeros), (diag, sub), (masked, diag))):
                    bias_scr[hh, tile, :, m * tq:m * tq + t] = first
                    bias_scr[hh, tile, :, m * tq + t:(m + 1) * tq] = last

    d_lo = lax.broadcasted_iota(I32, (LANES, tq), 0) < HEAD_DIM
    for hh in range(ATT_GROUP):
        qt = q_ref[:, hh * LANES:(hh + 1) * LANES].astype(F32).T
        q2t_scr[hh, :, 0:tq] = jnp.where(d_lo, qt, 0.0).astype(BF16)
        q2t_scr[hh, :, tq:2 * tq] = jnp.where(d_lo, 0.0, qt).astype(BF16)
        acc_scr[hh] = jnp.zeros((VT_ROWS, 2 * tq), F32)

    def scores(c, hh):
        k_c = k_ref[pl.ds(pl.multiple_of(c * t, t), t), hh * LANES:(hh + 1) * LANES]
        return jnp.dot(k_c, q2t_scr[hh], preferred_element_type=F32)

    def issue(c, slot, tile):
        for hh in range(ATT_GROUP):
            s = scores(c, hh)
            if tile is not None:
                s = s + bias_scr[hh, tile]
            s_scr[slot, hh] = s
            row = slot * ATT_GROUP + hh
            smax_scr[row:row + 1, :] = jnp.max(s, axis=0, keepdims=True)

    ones_rows = jnp.where(lax.broadcasted_iota(I32, (VT_ROWS - LANES, t), 0) == 0, 1.0, 0.0).astype(BF16)

    def consume(c, carry, slot):
        out = []
        for hh in range(ATT_GROUP):
            m_prev = carry[hh]
            row = slot * ATT_GROUP + hh
            m_new = jnp.maximum(m_prev, smax_scr[row:row + 1, :])
            alpha = jnp.exp2(m_prev - m_new)
            p = jnp.exp2(s_scr[slot, hh] - m_new).astype(BF16)
            vt_c = jnp.concatenate([vt_ref[hh * nchunk + c], ones_rows], axis=0)
            pv = jnp.dot(vt_c, p, preferred_element_type=F32)
            acc_scr[hh] = alpha * acc_scr[hh] + pv
            out.append(m_new)
        return tuple(out)

    def run(first, tiles, more, reissue, carry):
        if reissue:
            issue(first, 0, tiles[0])
        for k in range(len(tiles)):
            if k + 1 < len(tiles):
                issue(first + k + 1, (k + 1) % 2, tiles[k + 1])
            elif more:
                issue(first + k + 1, (k + 1) % 2, None)
            carry = consume(first + k, carry, k % 2)
        return carry

    nfar = jnp.maximum(2 * qi - 1, 0)
    ntrip = nfar // ATT_UNROLL
    issue(0, 0, None)

    def trip(i, carry):
        return run(i * ATT_UNROLL, [None] * ATT_UNROLL, True, False, carry)

    init = tuple(jnp.full((1, 2 * tq), NEG, F32) for _ in range(ATT_GROUP))
    carry = lax.fori_loop(0, ntrip, trip, init)
    rest = ntrip * ATT_UNROLL
    tails = [functools.partial(run, rest, [None] * r + [0, 1, 2], False, False)
             for r in range(1, ATT_UNROLL, 2)]
    tails.append(functools.partial(run, rest, [1, 2], False, True))
    carry = lax.switch(jnp.where(qi == 0, len(tails) - 1, (nfar - rest) // 2), tails, carry)

    lv = lamv_ref[...]
    lam = (jnp.exp(jnp.sum(lv[0:1, :] * lv[1:2, :], axis=1, keepdims=True))
           - jnp.exp(jnp.sum(lv[2:3, :] * lv[3:4, :], axis=1, keepdims=True)) + lam_init)
    for hh in range(ATT_GROUP):
        acc = acc_scr[hh, 0:LANES, :] * (1.0 / acc_scr[hh, LANES:LANES + 1, :])
        o = (acc[:, 0:tq] - lam * acc[:, tq:2 * tq]).T
        o = o * lax.rsqrt(jnp.mean(o * o, axis=1, keepdims=True) + NORM_EPS) * subw_ref[...]
        o_ref[:, hh * LANES:(hh + 1) * LANES] = (o * (1.0 - lam_init)).astype(BF16)


def _attention(relb, proj, lamv, subw, batch, seq, lam_init):
    n = batch * seq
    t = ATT_CHUNK
    tq = 2 * t
    nq = seq // tq
    nk = seq // t
    gw = ATT_GROUP * LANES
    ngroups = ATT_HEADS // ATT_GROUP
    vt = proj[:, 2 * ATT_WIDTH:3 * ATT_WIDTH].reshape(batch, nk, t, ATT_HEADS, LANES)
    vt = vt.transpose(0, 3, 1, 4, 2).reshape(batch * ATT_HEADS * nk, LANES, t)
    return pl.pallas_call(
        functools.partial(_attn_kernel, t=t, lam_init=lam_init),
        grid=(batch, ngroups, nq),
        in_specs=[pl.BlockSpec(memory_space=pltpu.SMEM),
                  pl.BlockSpec((tq, gw), lambda b, g, i: (b * nq + i, g)),
                  pl.BlockSpec((seq, gw), lambda b, g, i: (b, ngroups + g)),
                  pl.BlockSpec((ATT_GROUP * nk, LANES, t), lambda b, g, i: (b * ngroups + g, 0, 0)),
                  pl.BlockSpec((8, LANES), lambda b, g, i: (0, 0)),
                  pl.BlockSpec((1, LANES), lambda b, g, i: (0, 0))],
        out_specs=pl.BlockSpec((tq, gw), lambda b, g, i: (b * nq + i, g)),
        out_shape=jax.ShapeDtypeStruct((n, ATT_WIDTH), BF16),
        scratch_shapes=[pltpu.VMEM((ATT_GROUP, LANES, 2 * tq), BF16),
                        pltpu.VMEM((ATT_GROUP, VT_ROWS, 2 * tq), F32),
                        pltpu.VMEM((ATT_GROUP, 3, t, 2 * tq), F32),
                        pltpu.VMEM((2, ATT_GROUP, t, 2 * tq), F32),
                        pltpu.VMEM((8, 2 * tq), F32)],
        compiler_params=_cparams("arbitrary", "arbitrary", "arbitrary"),
    )(relb, proj, proj, vt, lamv, subw)


def _split3(x):
    hi = x.astype(BF16)
    r1 = x - hi.astype(F32)
    mid = r1.astype(BF16)
    lo = (r1 - mid.astype(F32)).astype(BF16)
    return hi, mid, lo


def _ssd_kernel(z_ref, xs_ref, bc_ref, dt_ref, cw_ref, cb_ref, dtb_ref, alog_ref, dskip_ref, nw_ref,
                o_ref, xpad_scr, state_scr):
    L = SSM_CHUNK
    W = SSM_WIDTH
    P2 = LANES
    nblk = W // P2
    gw = W // SSM_GROUPS

    @pl.when(pl.program_id(0) == 0)
    def _():
        xpad_scr[0:8, :] = jnp.zeros((8, W + SSM_BC), F32)
        state_scr[...] = jnp.zeros((SSM_STATE, W), F32)

    xpad_scr[8:8 + L, 0:W] = xs_ref[...].astype(F32)
    xpad_scr[8:8 + L, W:W + SSM_BC] = bc_ref[...].astype(F32)
    conv = cb_ref[...] + cw_ref[0:1, :] * xpad_scr[5:5 + L, :]
    for kk in range(1, SSM_CONV):
        conv = conv + cw_ref[kk:kk + 1, :] * xpad_scr[5 + kk:5 + kk + L, :]
    xpad_scr[0:8, :] = xpad_scr[L:L + 8, :]
    u = conv * _sigmoid(conv)

    dtr = dt_ref[...] + dtb_ref[...]
    dt = jnp.maximum(dtr, 0.0) + jnp.log1p(jnp.exp(-jnp.abs(dtr)))
    a = -jnp.exp(alog_ref[...])
    da = dt * a

    ri = lax.broadcasted_iota(I32, (L, L), 0)
    ci = lax.broadcasted_iota(I32, (L, L), 1)
    tril = ri >= ci
    tri = jnp.where(tril, 1.0, 0.0).astype(BF16)
    hi, mid, lo3 = _split3(da)
    a_cs = (jnp.dot(tri, hi, preferred_element_type=F32) + jnp.dot(tri, mid, preferred_element_type=F32)
            + jnp.dot(tri, lo3, preferred_element_type=F32))
    a_cs_t = a_cs.T
    a_last = a_cs[L - 1:L, :]
    e_cs = jnp.exp(a_cs)
    dt_ds = dt * jnp.exp(a_last - a_cs)

    lane_lo = lax.broadcasted_iota(I32, (1, P2), 1) < HEAD_DIM

    def expand(mat, i):
        return jnp.where(lane_lo, mat[:, 2 * i:2 * i + 1], mat[:, 2 * i + 1:2 * i + 2])

    y_blocks = []
    for g in range(SSM_GROUPS):
        bm = u[:, W + g * SSM_STATE:W + (g + 1) * SSM_STATE]
        cm = u[:, W + (SSM_GROUPS + g) * SSM_STATE:W + (SSM_GROUPS + g + 1) * SSM_STATE]
        bm16 = bm.astype(BF16)
        cm16 = cm.astype(BF16)
        cb = lax.dot_general(cm16, bm16, NT_DIMS, preferred_element_type=F32)
        st_g = state_scr[:, g * gw:(g + 1) * gw]
        y_off = jnp.dot(cm16, st_g.astype(BF16), preferred_element_type=F32)
        xd_blocks = []
        for ib in range(nblk // SSM_GROUPS):
            i = g * (nblk // SSM_GROUPS) + ib
            xs_blk = u[:, i * P2:(i + 1) * P2]
            xc = xs_blk * expand(dt, i)
            yd = jnp.zeros((L, P2), F32)
            for hh in range(2):
                head = 2 * i + hh
                seg = a_cs[:, head:head + 1] - a_cs_t[head:head + 1, :]
                wmat = (cb * jnp.where(tril, jnp.exp(seg), 0.0)).astype(BF16)
                keep = lane_lo if hh == 0 else jnp.logical_not(lane_lo)
                yd = yd + jnp.dot(wmat, jnp.where(keep, xc, 0.0).astype(BF16), preferred_element_type=F32)
            y = yd + y_off[:, ib * P2:(ib + 1) * P2] * expand(e_cs, i) + expand(dskip_ref[...], i) * xs_blk
            zf = z_ref[:, i * P2:(i + 1) * P2].astype(F32)
            y_blocks.append(y * (zf * _sigmoid(zf)))
            xd_blocks.append((xs_blk * expand(dt_ds, i)).astype(BF16))
        xd = jnp.concatenate(xd_blocks, axis=1)
        st_new = jnp.dot(bm.T.astype(BF16), xd, preferred_element_type=F32)
        decay = jnp.concatenate([expand(jnp.exp(a_last), g * (nblk // SSM_GROUPS) + ib)
                                 for ib in range(nblk // SSM_GROUPS)], axis=1)
        state_scr[:, g * gw:(g + 1) * gw] = st_g * decay + st_new

    per_g = nblk // SSM_GROUPS
    for g in range(SSM_GROUPS):
        blks = y_blocks[g * per_g:(g + 1) * per_g]
        ss = sum(jnp.sum(b * b, axis=1, keepdims=True) for b in blks) * (1.0 / gw)
        inv = lax.rsqrt(ss + NORM_EPS)
        for ib, b in enumerate(blks):
            i = g * per_g + ib
            o_ref[:, i * P2:(i + 1) * P2] = (b * inv * nw_ref[:, i * P2:(i + 1) * P2]).astype(BF16)


def _ssd_batched_kernel(z_ref, xs_ref, bc_ref, dt_ref, cw_ref, cb_ref, dtb_ref, alog_ref, dskip_ref, nw_ref,
                        o_ref, xpad_scr, state_scr):
    for b in range(z_ref.shape[0]):
        _ssd_kernel(z_ref.at[b], xs_ref.at[b], bc_ref.at[b], dt_ref.at[b], cw_ref, cb_ref, dtb_ref,
                    alog_ref, dskip_ref, nw_ref, o_ref.at[b], xpad_scr.at[b], state_scr.at[b])


def _ssd(proj, dt_raw, conv_w, conv_b, dt_bias, a_log, d_skip, norm_w, batch, seq):
    L = SSM_CHUNK
    cd = SSM_WIDTH + SSM_BC
    proj3 = proj.reshape(batch, seq, MAIN_COLS)
    dt3 = dt_raw.reshape(batch, seq, LANES)
    out = pl.pallas_call(
        _ssd_batched_kernel,
        grid=(seq // L,),
        in_specs=[pl.BlockSpec((batch, L, SSM_WIDTH), lambda c: (0, c, 3)),
                  pl.BlockSpec((batch, L, SSM_WIDTH), lambda c: (0, c, 4)),
                  pl.BlockSpec((batch, L, SSM_BC), lambda c: (0, c, 10)),
                  pl.BlockSpec((batch, L, LANES), lambda c: (0, c, 0)),
                  pl.BlockSpec((SSM_CONV, cd), lambda c: (0, 0)),
                  pl.BlockSpec((1, cd), lambda c: (0, 0)),
                  pl.BlockSpec((1, LANES), lambda c: (0, 0)),
                  pl.BlockSpec((1, LANES), lambda c: (0, 0)),
                  pl.BlockSpec((1, LANES), lambda c: (0, 0)),
                  pl.BlockSpec((1, SSM_WIDTH), lambda c: (0, 0))],
        out_specs=pl.BlockSpec((batch, L, SSM_WIDTH), lambda c: (0, c, 0)),
        out_shape=jax.ShapeDtypeStruct((batch, seq, SSM_WIDTH), BF16),
        scratch_shapes=[pltpu.VMEM((batch, L + 8, cd), F32),
                        pltpu.VMEM((batch, SSM_STATE, SSM_WIDTH), F32)],
        compiler_params=_cparams("arbitrary"),
    )(proj3, proj3, proj3, dt3, conv_w, conv_b, dt_bias, a_log, d_skip, norm_w)
    return out.reshape(batch * seq, SSM_WIDTH)


def _out_proj_kernel(x_ref, att_ref, ssm_ref, w_ref, mod_ref, nw_ref, h_ref, hn_ref):
    mix = (jnp.dot(att_ref[...], w_ref[0:ATT_WIDTH, :], preferred_element_type=F32)
           + jnp.dot(ssm_ref[...], w_ref[ATT_WIDTH:, :], preferred_element_type=F32))
    mod = mod_ref[0]
    h1 = x_ref[...] + mod[2:3, :] * mix
    h_ref[...] = h1
    y = h1 * lax.rsqrt(jnp.mean(h1 * h1, axis=1, keepdims=True) + NORM_EPS) * nw_ref[...]
    hn_ref[...] = (y * (1.0 + mod[4:5, :]) + mod[3:4, :]).astype(BF16)


def _out_proj(x2, att, ssm, w_out, mod3, norm2_w, seq):
    n, d = x2.shape
    tm = min(256, seq)
    return pl.pallas_call(
        _out_proj_kernel,
        grid=(n // tm,),
        in_specs=[pl.BlockSpec((tm, d), lambda i: (i, 0)),
                  pl.BlockSpec((tm, ATT_WIDTH), lambda i: (i, 0)),
                  pl.BlockSpec((tm, SSM_WIDTH), lambda i: (i, 0)),
                  pl.BlockSpec((ATT_WIDTH + SSM_WIDTH, d), lambda i: (0, 0)),
                  pl.BlockSpec((1, 6, d), lambda i: (i * tm // seq, 0, 0)),
                  pl.BlockSpec((1, d), lambda i: (0, 0))],
        out_specs=[pl.BlockSpec((tm, d), lambda i: (i, 0)),
                   pl.BlockSpec((tm, d), lambda i: (i, 0))],
        out_shape=[jax.ShapeDtypeStruct((n, d), F32),
                   jax.ShapeDtypeStruct((n, d), BF16)],
        compiler_params=_cparams("arbitrary"),
    )(x2, att, ssm, w_out, mod3, norm2_w)


def _topk_rows(s, k, rows):
    row = lax.broadcasted_iota(I32, s.shape, 0).astype(F32)
    vals, idxs = [], []
    for _ in range(k):
        m, idx = _argmax_rows(s, rows)
        s = jnp.where(row == idx, -jnp.inf, s)
        vals.append(m)
        idxs.append(idx)
    return vals, idxs


def _argmax_rows(s, rows):
    vs = [s[8 * i:8 * i + 8] for i in range(rows // 8)]
    sub = lax.broadcasted_iota(I32, vs[0].shape, 0).astype(F32)
    rs = [sub + float(8 * i) for i in range(rows // 8)]
    while len(vs) > 1:
        nv, nr = [], []
        for i in range(0, len(vs) - 1, 2):
            take_b = vs[i + 1] > vs[i]
            nv.append(jnp.where(take_b, vs[i + 1], vs[i]))
            nr.append(jnp.where(take_b, rs[i + 1], rs[i]))
        if len(vs) % 2:
            nv.append(vs[-1])
            nr.append(rs[-1])
        vs, rs = nv, nr
    m = jnp.max(vs[0], axis=0, keepdims=True)
    idx = jnp.min(jnp.where(vs[0] == m, rs[0], float(rows)), axis=0, keepdims=True)
    return m, idx


_CAND = [(i, j) for i in range(PEER_TOPK) for j in range(PEER_TOPK) if (i + 1) * (j + 1) <= PEER_TOPK]


def _route_kernel(hn_ref, wq_ref, keys_ref, a_ref, b_ref, g_ref, top_scr, code_scr):
    t = hn_ref.shape[0]
    qp = jnp.dot(hn_ref[...], wq_ref[...], preferred_element_type=F32).astype(BF16)
    ncand = len(_CAND)
    pad = (-ncand) % 8
    for h in range(PEER_HEADS):
        sub = []
        for c in range(2):
            hc = 2 * h + c
            sc = lax.dot_general(keys_ref[hc], qp[:, hc * LANES:(hc + 1) * LANES], NT_DIMS,
                                 preferred_element_type=F32)
            sub.append(_topk_rows(sc, PEER_TOPK, PEER_NKEYS))
        (s1, i1), (s2, i2) = sub
        cand = jnp.concatenate([s1[i] + s2[j] for i, j in _CAND]
                               + [jnp.full((pad, t), -jnp.inf, F32)], axis=0)
        a_hi = [v * float(PEER_NKEYS) for v in i1]
        code = jnp.concatenate([a_hi[i] + i2[j] for i, j in _CAND] + [jnp.zeros((pad, t), F32)], axis=0)
        row = lax.broadcasted_iota(I32, cand.shape, 0).astype(F32)
        for kk in range(PEER_TOPK):
            m, idx = _argmax_rows(cand, ncand + pad)
            sel = row == idx
            slot = h * PEER_TOPK + kk
            top_scr[slot:slot + 1, :] = m
            code_scr[slot:slot + 1, :] = jnp.max(jnp.where(sel, code, -1.0), axis=0, keepdims=True)
            cand = jnp.where(sel, -jnp.inf, cand)
        top = top_scr[h * PEER_TOPK:(h + 1) * PEER_TOPK, :]
        e = jnp.exp(top - jnp.max(top, axis=0, keepdims=True))
        top_scr[h * PEER_TOPK:(h + 1) * PEER_TOPK, :] = e / jnp.sum(e, axis=0, keepdims=True)
    code_t = code_scr[...].T
    first = jnp.floor(code_t * (1.0 / PEER_NKEYS))
    a_ref[...] = first.astype(I32)
    b_ref[...] = (code_t - first * float(PEER_NKEYS)).astype(I32)
    g_ref[...] = top_scr[...].T


def _route(hn2, wq, keys):
    n, d = hn2.shape
    t = min(256, n)
    qd = wq.shape[1]
    out = jax.ShapeDtypeStruct((n, PEER_SLOTS), I32)
    return pl.pallas_call(
        _route_kernel,
        grid=(n // t,),
        in_specs=[pl.BlockSpec((t, d), lambda i: (i, 0)),
                  pl.BlockSpec((d, qd), lambda i: (0, 0)),
                  pl.BlockSpec(keys.shape, lambda i: (0, 0, 0))],
        out_specs=[pl.BlockSpec((t, PEER_SLOTS), lambda i: (i, 0))] * 3,
        out_shape=[out, out, jax.ShapeDtypeStruct((n, PEER_SLOTS), F32)],
        scratch_shapes=[pltpu.VMEM((PEER_SLOTS, t), F32),
                        pltpu.VMEM((PEER_SLOTS, t), F32)],
        compiler_params=_cparams("arbitrary"),
    )(hn2, wq, keys)


PAIR = 2 * PEER_NKEYS
DOWN_PAIRS = 8
DOWN_TOKENS = 1024
UP_KEYS = 16
UP_TOKENS = 512
TOKEN_UNROLL = 32
DENSE_PITCH = PEER_NKEYS + 8


def _peer_down_kernel(x_ref, dn_ref, a_ref, b_ref, pre_ref):
    j = pl.program_id(1)

    @pl.when(j == 0)
    def _():
        pre_ref[...] = jnp.zeros(pre_ref.shape, F32)

    x = x_ref[...]
    a = a_ref[...]
    b = b_ref[...]
    pre = pre_ref[...]
    for q in range(DOWN_PAIRS):
        p = lax.dot_general(x, dn_ref[q * PAIR:(q + 1) * PAIR, :], NT_DIMS,
                            preferred_element_type=F32)
        for half in range(2):
            g = jnp.take_along_axis(p[:, half * LANES:(half + 1) * LANES], b, axis=1)
            pre = jnp.where(a == 2 * (DOWN_PAIRS * j + q) + half, g, pre)
    pre_ref[...] = pre


def _peer_down(hn2, down16, aidx, bidx):
    n, d = hn2.shape
    t = min(DOWN_TOKENS, n)
    slot_spec = pl.BlockSpec((t, PEER_SLOTS), lambda i, j: (i, 0))
    return pl.pallas_call(
        _peer_down_kernel,
        grid=(n // t, down16.shape[0] // (DOWN_PAIRS * PAIR)),
        in_specs=[pl.BlockSpec((t, d), lambda i, j: (i, 0)),
                  pl.BlockSpec((DOWN_PAIRS * PAIR, d), lambda i, j: (j, 0)),
                  slot_spec, slot_spec],
        out_specs=slot_spec,
        out_shape=jax.ShapeDtypeStruct((n, PEER_SLOTS), F32),
        compiler_params=_cparams("arbitrary", "arbitrary"),
    )(hn2, down16, aidx, bidx)


U32 = jnp.uint32
HI_HALF = 0xFFFF0000


def _bf16_bits(x):
    return lax.bitcast_convert_type(x.astype(BF16).astype(F32), U32)


def _peer_up_kernel(pre_ref, g_ref, a_ref, b_ref, up_ref, h_ref, mod_ref, o_ref,
                    act_scr, dense_scr):
    j = pl.program_id(1)
    t = pre_ref.shape[0]
    half = t // 2
    nk = PEER_NKEYS

    @pl.when(j == 0)
    def _():
        pre = pre_ref[...]
        act_scr[...] = 0.5 * pre * (1.0 + lax.erf(pre * (1.0 / math.sqrt(2.0)))) * g_ref[...]
        o_ref[...] = jnp.zeros(o_ref.shape, F32)
        row = lax.broadcasted_iota(I32, (nk, PEER_SLOTS), 0)

        def scatter(a_row, b_row, c_row):
            xa = jnp.where(row == a_row, c_row, 0.0).astype(BF16)
            yb = jnp.where(row == b_row, 1.0, 0.0).astype(BF16)
            return lax.dot_general(xa, yb, NT_DIMS, preferred_element_type=F32)

        def body(i, carry):
            for grp in range(TOKEN_UNROLL // 8):
                base = pl.multiple_of(i * TOKEN_UNROLL + grp * 8, 8)
                lo = [r[pl.ds(base, 8), :] for r in (a_ref, b_ref, act_scr)]
                hi = [r[pl.ds(base + half, 8), :] for r in (a_ref, b_ref, act_scr)]
                for u in range(8):
                    d_lo = scatter(*[v[u:u + 1, :] for v in lo])
                    d_hi = scatter(*[v[u:u + 1, :] for v in hi])
                    dense_scr[pl.ds(pl.multiple_of((base + u) * DENSE_PITCH, 8), nk), :] = (
                        lax.shift_right_logical(_bf16_bits(d_lo), U32(16)) | _bf16_bits(d_hi))
            return carry

        lax.fori_loop(0, half // TOKEN_UNROLL, body, 0)

    words = [dense_scr[pl.ds(UP_KEYS * j + u, half, stride=DENSE_PITCH), :] for u in range(UP_KEYS)]
    lo = jnp.concatenate([lax.bitcast_convert_type(lax.shift_left(w, U32(16)), F32) for w in words], axis=1)
    hi = jnp.concatenate([lax.bitcast_convert_type(w & U32(HI_HALF), F32) for w in words], axis=1)
    lhs = jnp.concatenate([lo, hi], axis=0).astype(BF16)
    o_ref[...] += jnp.dot(lhs, up_ref[...], preferred_element_type=F32)

    @pl.when(j == pl.num_programs(1) - 1)
    def _():
        o_ref[...] = h_ref[...] + mod_ref[0][5:6, :] * o_ref[...]


def _peer_up(pre, gate, aidx, bidx, up16, h1, mod3, seq):
    n, d = h1.shape
    t = min(UP_TOKENS, seq)
    slot_spec = pl.BlockSpec((t, PEER_SLOTS), lambda i, j: (i, 0))
    return pl.pallas_call(
        _peer_up_kernel,
        grid=(n // t, up16.shape[0] // (UP_KEYS * PEER_NKEYS)),
        in_specs=[slot_spec, slot_spec, slot_spec, slot_spec,
                  pl.BlockSpec((UP_KEYS * PEER_NKEYS, d), lambda i, j: (j, 0)),
                  pl.BlockSpec((t, d), lambda i, j: (i, 0)),
                  pl.BlockSpec((1, 6, d), lambda i, j: (i * t // seq, 0, 0))],
        out_specs=pl.BlockSpec((t, d), lambda i, j: (i, 0)),
        out_shape=jax.ShapeDtypeStruct((n, d), F32),
        scratch_shapes=[pltpu.VMEM((t, PEER_SLOTS), F32),
                        pltpu.VMEM((t // 2 * DENSE_PITCH, PEER_NKEYS), U32)],
        compiler_params=_cparams("arbitrary", "arbitrary"),
    )(pre, gate, aidx, bidx, up16, h1, mod3)


def _pad_lanes(v):
    return jnp.pad(v.astype(F32), (0, LANES - v.shape[0])).reshape(1, LANES)


def _layer(h2, mod3, l, batch, seq, norm1_w, w_in, q_norm_w, k_norm_w, rel_bias, lambda_q1, lambda_k1,
           lambda_q2, lambda_k2, subln_w, conv_w, conv_b, dt_bias, a_log, d_skip, ssm_norm_w, w_out,
           norm2_w, peer_wq, peer_keys, expert_down, expert_up):
    d = h2.shape[1]
    lam_init = 0.8 - 0.6 * math.exp(-0.3 * l)
    w16 = w_in.astype(BF16)
    w_dt = jnp.pad(w16[:, MAIN_COLS:], ((0, 0), (0, LANES - SSM_HEADS)))
    qn = jnp.tile(q_norm_w.astype(F32) * (HEAD_DIM ** -0.5 * LOG2E), 2).reshape(1, LANES)
    kn = jnp.tile(k_norm_w.astype(F32), 2).reshape(1, LANES)
    proj, dt_raw = _in_proj(h2, norm1_w.reshape(1, d), mod3, w16, w_dt, qn, kn, seq)

    lamv = jnp.pad(jnp.stack([lambda_q1, lambda_k1, lambda_q2, lambda_k2]).astype(F32),
                   ((0, 4), (0, LANES - HEAD_DIM)))
    att = _attention(rel_bias.astype(F32).reshape(-1), proj, lamv, subln_w.reshape(1, LANES),
                     batch, seq, lam_init)
    ssm = _ssd(proj, dt_raw, conv_w, conv_b.reshape(1, -1), _pad_lanes(dt_bias), _pad_lanes(a_log),
               _pad_lanes(d_skip), ssm_norm_w.reshape(1, -1), batch, seq)
    h1, hn2 = _out_proj(h2, att, ssm, w_out.astype(BF16), mod3, norm2_w.reshape(1, d), seq)

    keys = peer_keys.astype(BF16).reshape(2 * PEER_HEADS, PEER_NKEYS, -1)
    aidx, bidx, gate = _route(hn2, peer_wq.astype(BF16), keys)
    pre = _peer_down(hn2, expert_down.astype(BF16), aidx, bidx)
    return _peer_up(pre, gate, aidx, bidx, expert_up.astype(BF16), h1, mod3, seq)


def kernel(x, c, ada_w, ada_b, norm1_w, w_in, q_norm_w, k_norm_w, rel_bias, lambda_q1, lambda_k1, lambda_q2, lambda_k2, subln_w, conv_w, conv_b, dt_bias, a_log, d_skip, ssm_norm_w, w_out, norm2_w, peer_wq, peer_keys, expert_down, expert_up):
    batch, seq, d = x.shape
    depth = ada_w.shape[0]
    assert seq % (2 * ATT_CHUNK) == 0 and seq % min(1024, seq) == 0, "unsupported sequence length"
    assert batch <= 8 and d == ATT_WIDTH + SSM_WIDTH, "unsupported batch / model width"
    h2 = x.reshape(batch * seq, d)
    c_pad = jnp.pad(c, ((0, 8 - batch), (0, 0)))
    for l in range(depth):
        mod = _ada(c_pad, ada_w[l], ada_b[l].reshape(1, -1))
        mod3 = mod[:batch].reshape(batch, 6, d)
        h2 = _layer(h2, mod3, l, batch, seq, norm1_w[l], w_in[l], q_norm_w[l], k_norm_w[l], rel_bias,
                    lambda_q1[l], lambda_k1[l], lambda_q2[l], lambda_k2[l], subln_w[l], conv_w[l],
                    conv_b[l], dt_bias[l], a_log[l], d_skip[l], ssm_norm_w[l], w_out[l], norm2_w[l],
                    peer_wq[l], peer_keys[l], expert_down[l], expert_up[l])
    return h2.reshape(batch, seq, d)
```

```python
import functools
import math

import jax
import jax.numpy as jnp
from jax import lax
from jax.experimental import pallas as pl
from jax.experimental.pallas import tpu as pltpu

F32 = jnp.float32
BF16 = jnp.bfloat16
I32 = jnp.int32

LANES = 128
VMEM_LIMIT = 56 * 1024 * 1024

NORM_EPS = 1e-6
HEAD_DIM = 64
ATT_HEADS = 8
ATT_WIDTH = 1024
SSM_WIDTH = 1024
SSM_HEADS = 16
SSM_GROUPS = 2
SSM_STATE = 128
SSM_CONV = 4
SSM_CHUNK = 128
SSM_BC = 2 * SSM_GROUPS * SSM_STATE
REL_BUCKETS = 32
REL_MAX_DIST = 128
PEER_HEADS = 8
PEER_NKEYS = 128
PEER_TOPK = 16
PEER_SLOTS = PEER_HEADS * PEER_TOPK
MAIN_COLS = 3 * ATT_WIDTH + SSM_WIDTH + SSM_WIDTH + SSM_BC
NEG = -1e30
LOG2E = math.log2(math.e)

NT_DIMS = (((1,), (1,)), ((), ()))


def _cparams(*sem):
    return pltpu.CompilerParams(dimension_semantics=sem, vmem_limit_bytes=VMEM_LIMIT)


def _sigmoid(x):
    return 1.0 / (1.0 + jnp.exp(-x))


def _ada_kernel(c_ref, w_ref, b_ref, o_ref):
    c = c_ref[...]
    sc = (c * _sigmoid(c)).astype(BF16)
    o_ref[...] = jnp.dot(sc, w_ref[...].astype(BF16), preferred_element_type=F32) + b_ref[...]


def _ada(c_pad, ada_w, ada_b):
    rows, d = c_pad.shape
    n = ada_w.shape[1]
    tn = 1536
    return pl.pallas_call(
        _ada_kernel,
        grid=(n // tn,),
        in_specs=[pl.BlockSpec((rows, d), lambda j: (0, 0)),
                  pl.BlockSpec((d, tn), lambda j: (0, j)),
                  pl.BlockSpec((1, tn), lambda j: (0, j))],
        out_specs=pl.BlockSpec((rows, tn), lambda j: (0, j)),
        out_shape=jax.ShapeDtypeStruct((rows, n), F32),
        compiler_params=_cparams("arbitrary"),
    )(c_pad, ada_w, ada_b)


def _group_rms(blk, w_row, lo):
    sq = blk * blk
    s_all = jnp.sum(sq, axis=1, keepdims=True)
    s_lo = jnp.sum(jnp.where(lo, sq, 0.0), axis=1, keepdims=True)
    s = jnp.where(lo, s_lo, s_all - s_lo)
    return blk * lax.rsqrt(s * (1.0 / HEAD_DIM) + NORM_EPS) * w_row


def _in_proj_kernel(x_ref, nw_ref, mod_ref, w_ref, wdt_ref, qn_ref, kn_ref,
                    o_ref, dt_ref, hn_scr, raw_scr, *, tn, ntiles):
    j = pl.program_id(1)
    qk_tiles = 2 * ATT_WIDTH // tn

    def finish(k):
        raw = raw_scr[k % 2]
        if k < qk_tiles:
            lo = lax.broadcasted_iota(I32, (1, LANES), 1) < HEAD_DIM
            w_row = qn_ref[...] if k < qk_tiles // 2 else kn_ref[...]
            for cb in range(tn // LANES):
                blk = raw[:, cb * LANES:(cb + 1) * LANES]
                o_ref[:, cb * LANES:(cb + 1) * LANES] = _group_rms(blk, w_row, lo).astype(BF16)
        else:
            o_ref[...] = raw.astype(BF16)

    for step in range(ntiles + 1):
        @pl.when(j == step)
        def _(step=step):
            if step == 0:
                mod = mod_ref[0]
                rows = x_ref.shape[0] // 2
                for r0 in (0, rows):
                    x = x_ref[r0:r0 + rows, :]
                    y = x * lax.rsqrt(jnp.mean(x * x, axis=1, keepdims=True) + NORM_EPS) * nw_ref[...]
                    hn = (y * (1.0 + mod[1:2, :]) + mod[0:1, :]).astype(BF16)
                    hn_scr[r0:r0 + rows, :] = hn
                    dt_ref[r0:r0 + rows, :] = jnp.dot(hn, wdt_ref[...], preferred_element_type=F32)
                    raw_scr[0, r0:r0 + rows, :] = jnp.dot(hn, w_ref[...], preferred_element_type=F32)
            elif step < ntiles:
                raw_scr[step % 2] = jnp.dot(hn_scr[...], w_ref[...], preferred_element_type=F32)
            if step > 0:
                finish(step - 1)


def _in_proj(x2, norm_w, mod3, w_main, w_dt, qn, kn, seq):
    n, d = x2.shape
    tm = min(1024, seq)
    tn = 512
    ntiles = MAIN_COLS // tn
    return pl.pallas_call(
        functools.partial(_in_proj_kernel, tn=tn, ntiles=ntiles),
        grid=(n // tm, ntiles + 1),
        in_specs=[pl.BlockSpec((tm, d), lambda i, j: (i, 0)),
                  pl.BlockSpec((1, d), lambda i, j: (0, 0)),
                  pl.BlockSpec((1, 6, d), lambda i, j: (i * tm // seq, 0, 0)),
                  pl.BlockSpec((d, tn), lambda i, j: (0, jnp.minimum(j, ntiles - 1))),
                  pl.BlockSpec((d, LANES), lambda i, j: (0, 0)),
                  pl.BlockSpec((1, LANES), lambda i, j: (0, 0)),
                  pl.BlockSpec((1, LANES), lambda i, j: (0, 0))],
        out_specs=[pl.BlockSpec((tm, tn), lambda i, j: (i, jnp.maximum(j - 1, 0))),
                   pl.BlockSpec((tm, LANES), lambda i, j: (i, 0))],
        out_shape=[jax.ShapeDtypeStruct((n, MAIN_COLS), BF16),
                   jax.ShapeDtypeStruct((n, LANES), F32)],
        scratch_shapes=[pltpu.VMEM((tm, d), BF16),
                        pltpu.VMEM((2, tm, tn), F32)],
        compiler_params=_cparams("arbitrary", "arbitrary"),
    )(x2, norm_w, mod3, w_main, w_dt, qn, kn)


ATT_GROUP = 2
ATT_UNROLL = 4
ATT_CHUNK = 256
VT_ROWS = LANES + 16


def _attn_kernel(relb_ref, q_ref, k_ref, vt_ref, lamv_ref, subw_ref, o_ref,
                 q2t_scr, acc_scr, bias_scr, s_scr, smax_scr, *, t, lam_init):
    hp = pl.program_id(1)
    qi = pl.program_id(2)
    tq = 2 * t
    nchunk = vt_ref.shape[0] // ATT_GROUP

    @pl.when(qi == 0)
    def _():
        kk = lax.broadcasted_iota(I32, (t, t), 0)
        qq = lax.broadcasted_iota(I32, (t, t), 1)
        max_exact = REL_BUCKETS // 2
        buckets = []
        for off in (0, t):
            nn = jnp.maximum(qq - kk + off, 0)
            nf = jnp.maximum(nn, 1).astype(F32)
            large = max_exact + (jnp.log(nf / max_exact) / math.log(REL_MAX_DIST / max_exact)
                                 * (REL_BUCKETS - max_exact)).astype(I32)
            buckets.append(jnp.where(nn < max_exact, nn, jnp.minimum(large, REL_BUCKETS - 1)))
        zeros = jnp.zeros((t, t), F32)
        masked = jnp.full((t, t), NEG, F32)
        for hh in range(ATT_GROUP):
            head = hp * ATT_GROUP + hh
            for m in range(2):
                far = relb_ref[(REL_BUCKETS - 1) * 2 * ATT_HEADS + head * 2 + m]
                diag, sub = zeros, zeros
                for b in range(REL_BUCKETS - 1):
                    delta = (relb_ref[b * 2 * ATT_HEADS + head * 2 + m] - far) * LOG2E
                    diag = jnp.where(buckets[0] == b, delta, diag)
                    sub = jnp.where(buckets[1] == b, delta, sub)
                diag = jnp.where(qq >= kk, diag, NEG)
                for tile, (first, last) in enumerate(((sub, zeros), (diag, sub), (masked, diag))):
                    bias_scr[hh, tile, :, m * tq:m * tq + t] = first
                    bias_scr[hh, tile, :, m * tq + t:(m + 1) * tq] = last

    d_lo = lax.broadcasted_iota(I32, (LANES, tq), 0) < HEAD_DIM
    for hh in range(ATT_GROUP):
        qt = q_ref[:, hh * LANES:(hh + 1) * LANES].astype(F32).T
        q2t_scr[hh, :, 0:tq] = jnp.where(d_lo, qt, 0.0).astype(BF16)
        q2t_scr[hh, :, tq:2 * tq] = jnp.where(d_lo, 0.0, qt).astype(BF16)
        acc_scr[hh] = jnp.zeros((VT_ROWS, 2 * tq), F32)

    def scores(c, hh):
        k_c = k_ref[pl.ds(pl.multiple_of(c * t, t), t), hh * LANES:(hh + 1) * LANES]
        return jnp.dot(k_c, q2t_scr[hh], preferred_element_type=F32)

    def issue(c, slot, tile):
        for hh in range(ATT_GROUP):
            s = scores(c, hh)
            if tile is not None:
                s = s + bias_scr[hh, tile]
            s_scr[slot, hh] = s
            row = slot * ATT_GROUP + hh
            smax_scr[row:row + 1, :] = jnp.max(s, axis=0, keepdims=True)

    ones_rows = jnp.where(lax.broadcasted_iota(I32, (VT_ROWS - LANES, t), 0) == 0, 1.0, 0.0).astype(BF16)

    def consume(c, carry, slot):
        out = []
        for hh in range(ATT_GROUP):
            m_prev = carry[hh]
            row = slot * ATT_GROUP + hh
            m_new = jnp.maximum(m_prev, smax_scr[row:row + 1, :])
            alpha = jnp.exp2(m_prev - m_new)
            p = jnp.exp2(s_scr[slot, hh] - m_new).astype(BF16)
            vt_c = jnp.concatenate([vt_ref[hh * nchunk + c], ones_rows], axis=0)
            pv = jnp.dot(vt_c, p, preferred_element_type=F32)
            acc_scr[hh] = alpha * acc_scr[hh] + pv
            out.append(m_new)
        return tuple(out)

    def run(first, tiles, more, reissue, carry):
        if reissue:
            issue(first, 0, tiles[0])
        for k in range(len(tiles)):
            if k + 1 < len(tiles):
                issue(first + k + 1, (k + 1) % 2, tiles[k + 1])
            elif more:
                issue(first + k + 1, (k + 1) % 2, None)
            carry = consume(first + k, carry, k % 2)
        return carry

    nfar = jnp.maximum(2 * qi - 1, 0)
    ntrip = nfar // ATT_UNROLL
    issue(0, 0, None)

    def trip(i, carry):
        return run(i * ATT_UNROLL, [None] * ATT_UNROLL, True, False, carry)

    init = tuple(jnp.full((1, 2 * tq), NEG, F32) for _ in range(ATT_GROUP))
    carry = lax.fori_loop(0, ntrip, trip, init)
    rest = ntrip * ATT_UNROLL
    tails = [functools.partial(run, rest, [None] * r + [0, 1, 2], False, False)
             for r in range(1, ATT_UNROLL, 2)]
    tails.append(functools.partial(run, rest, [1, 2], False, True))
    carry = lax.switch(jnp.where(qi == 0, len(tails) - 1, (nfar - rest) // 2), tails, carry)

    lv = lamv_ref[...]
    lam = (jnp.exp(jnp.sum(lv[0:1, :] * lv[1:2, :], axis=1, keepdims=True))
           - jnp.exp(jnp.sum(lv[2:3, :] * lv[3:4, :], axis=1, keepdims=True)) + lam_init)
    for hh in range(ATT_GROUP):
        acc = acc_scr[hh, 0:LANES, :] * (1.0 / acc_scr[hh, LANES:LANES + 1, :])
        o = (acc[:, 0:tq] - lam * acc[:, tq:2 * tq]).T
        o = o * lax.rsqrt(jnp.mean(o * o, axis=1, keepdims=True) + NORM_EPS) * subw_ref[...]
        o_ref[:, hh * LANES:(hh + 1) * LANES] = (o * (1.0 - lam_init)).astype(BF16)


def _attention(relb, proj, lamv, subw, batch, seq, lam_init):
    n = batch * seq
    t = ATT_CHUNK
    tq = 2 * t
    nq = seq // tq
    nk = seq // t
    gw = ATT_GROUP * LANES
    ngroups = ATT_HEADS // ATT_GROUP
    vt = proj[:, 2 * ATT_WIDTH:3 * ATT_WIDTH].reshape(batch, nk, t, ATT_HEADS, LANES)
    vt = vt.transpose(0, 3, 1, 4, 2).reshape(batch * ATT_HEADS * nk, LANES, t)
    return pl.pallas_call(
        functools.partial(_attn_kernel, t=t, lam_init=lam_init),
        grid=(batch, ngroups, nq),
        in_specs=[pl.BlockSpec(memory_space=pltpu.SMEM),
                  pl.BlockSpec((tq, gw), lambda b, g, i: (b * nq + i, g)),
                  pl.BlockSpec((seq, gw), lambda b, g, i: (b, ngroups + g)),
                  pl.BlockSpec((ATT_GROUP * nk, LANES, t), lambda b, g, i: (b * ngroups + g, 0, 0)),
                  pl.BlockSpec((8, LANES), lambda b, g, i: (0, 0)),
                  pl.BlockSpec((1, LANES), lambda b, g, i: (0, 0))],
        out_specs=pl.BlockSpec((tq, gw), lambda b, g, i: (b * nq + i, g)),
        out_shape=jax.ShapeDtypeStruct((n, ATT_WIDTH), BF16),
        scratch_shapes=[pltpu.VMEM((ATT_GROUP, LANES, 2 * tq), BF16),
                        pltpu.VMEM((ATT_GROUP, VT_ROWS, 2 * tq), F32),
                        pltpu.VMEM((ATT_GROUP, 3, t, 2 * tq), F32),
                        pltpu.VMEM((2, ATT_GROUP, t, 2 * tq), F32),
                        pltpu.VMEM((8, 2 * tq), F32)],
        compiler_params=_cparams("arbitrary", "arbitrary", "arbitrary"),
    )(relb, proj, proj, vt, lamv, subw)


def _split3(x):
    hi = x.astype(BF16)
    r1 = x - hi.astype(F32)
    mid = r1.astype(BF16)
    lo = (r1 - mid.astype(F32)).astype(BF16)
    return hi, mid, lo


def _ssd_kernel(z_ref, xs_ref, bc_ref, dt_ref, cw_ref, cb_ref, dtb_ref, alog_ref, dskip_ref, nw_ref,
                o_ref, xpad_scr, state_scr):
    L = SSM_CHUNK
    W = SSM_WIDTH
    P2 = LANES
    nblk = W // P2
    gw = W // SSM_GROUPS

    @pl.when(pl.program_id(0) == 0)
    def _():
        xpad_scr[0:8, :] = jnp.zeros((8, W + SSM_BC), F32)
        state_scr[...] = jnp.zeros((SSM_STATE, W), F32)

    xpad_scr[8:8 + L, 0:W] = xs_ref[...].astype(F32)
    xpad_scr[8:8 + L, W:W + SSM_BC] = bc_ref[...].astype(F32)
    conv = cb_ref[...] + cw_ref[0:1, :] * xpad_scr[5:5 + L, :]
    for kk in range(1, SSM_CONV):
        conv = conv + cw_ref[kk:kk + 1, :] * xpad_scr[5 + kk:5 + kk + L, :]
    xpad_scr[0:8, :] = xpad_scr[L:L + 8, :]
    u = conv * _sigmoid(conv)

    dtr = dt_ref[...] + dtb_ref[...]
    dt = jnp.maximum(dtr, 0.0) + jnp.log1p(jnp.exp(-jnp.abs(dtr)))
    a = -jnp.exp(alog_ref[...])
    da = dt * a

    ri = lax.broadcasted_iota(I32, (L, L), 0)
    ci = lax.broadcasted_iota(I32, (L, L), 1)
    tril = ri >= ci
    tri = jnp.where(tril, 1.0, 0.0).astype(BF16)
    hi, mid, lo3 = _split3(da)
    a_cs = (jnp.dot(tri, hi, preferred_element_type=F32) + jnp.dot(tri, mid, preferred_element_type=F32)
            + jnp.dot(tri, lo3, preferred_element_type=F32))
    a_cs_t = a_cs.T
    a_last = a_cs[L - 1:L, :]
    e_cs = jnp.exp(a_cs)
    dt_ds = dt * jnp.exp(a_last - a_cs)

    lane_lo = lax.broadcasted_iota(I32, (1, P2), 1) < HEAD_DIM

    def expand(mat, i):
        return jnp.where(lane_lo, mat[:, 2 * i:2 * i + 1], mat[:, 2 * i + 1:2 * i + 2])

    y_blocks = []
    for g in range(SSM_GROUPS):
        bm = u[:, W + g * SSM_STATE:W + (g + 1) * SSM_STATE]
        cm = u[:, W + (SSM_GROUPS + g) * SSM_STATE:W + (SSM_GROUPS + g + 1) * SSM_STATE]
        bm16 = bm.astype(BF16)
        cm16 = cm.astype(BF16)
        cb = lax.dot_general(cm16, bm16, NT_DIMS, preferred_element_type=F32)
        st_g = state_scr[:, g * gw:(g + 1) * gw]
        y_off = jnp.dot(cm16, st_g.astype(BF16), preferred_element_type=F32)
        xd_blocks = []
        for ib in range(nblk // SSM_GROUPS):
            i = g * (nblk // SSM_GROUPS) + ib
            xs_blk = u[:, i * P2:(i + 1) * P2]
            xc = xs_blk * expand(dt, i)
            yd = jnp.zeros((L, P2), F32)
            for hh in range(2):
                head = 2 * i + hh
                seg = a_cs[:, head:head + 1] - a_cs_t[head:head + 1, :]
                wmat = (cb * jnp.where(tril, jnp.exp(seg), 0.0)).astype(BF16)
                keep = lane_lo if hh == 0 else jnp.logical_not(lane_lo)
                yd = yd + jnp.dot(wmat, jnp.where(keep, xc, 0.0).astype(BF16), preferred_element_type=F32)
            y = yd + y_off[:, ib * P2:(ib + 1) * P2] * expand(e_cs, i) + expand(dskip_ref[...], i) * xs_blk
            zf = z_ref[:, i * P2:(i + 1) * P2].astype(F32)
            y_blocks.append(y * (zf * _sigmoid(zf)))
            xd_blocks.append((xs_blk * expand(dt_ds, i)).astype(BF16))
        xd = jnp.concatenate(xd_blocks, axis=1)
        st_new = jnp.dot(bm.T.astype(BF16), xd, preferred_element_type=F32)
        decay = jnp.concatenate([expand(jnp.exp(a_last), g * (nblk // SSM_GROUPS) + ib)
                                 for ib in range(nblk // SSM_GROUPS)], axis=1)
        state_scr[:, g * gw:(g + 1) * gw] = st_g * decay + st_new

    per_g = nblk // SSM_GROUPS
    for g in range(SSM_GROUPS):
        blks = y_blocks[g * per_g:(g + 1) * per_g]
        ss = sum(jnp.sum(b * b, axis=1, keepdims=True) for b in blks) * (1.0 / gw)
        inv = lax.rsqrt(ss + NORM_EPS)
        for ib, b in enumerate(blks):
            i = g * per_g + ib
            o_ref[:, i * P2:(i + 1) * P2] = (b * inv * nw_ref[:, i * P2:(i + 1) * P2]).astype(BF16)


def _ssd_batched_kernel(z_ref, xs_ref, bc_ref, dt_ref, cw_ref, cb_ref, dtb_ref, alog_ref, dskip_ref, nw_ref,
                        o_ref, xpad_scr, state_scr):
    for b in range(z_ref.shape[0]):
        _ssd_kernel(z_ref.at[b], xs_ref.at[b], bc_ref.at[b], dt_ref.at[b], cw_ref, cb_ref, dtb_ref,
                    alog_ref, dskip_ref, nw_ref, o_ref.at[b], xpad_scr.at[b], state_scr.at[b])


def _ssd(proj, dt_raw, conv_w, conv_b, dt_bias, a_log, d_skip, norm_w, batch, seq):
    L = SSM_CHUNK
    cd = SSM_WIDTH + SSM_BC
    proj3 = proj.reshape(batch, seq, MAIN_COLS)
    dt3 = dt_raw.reshape(batch, seq, LANES)
    out = pl.pallas_call(
        _ssd_batched_kernel,
        grid=(seq // L,),
        in_specs=[pl.BlockSpec((batch, L, SSM_WIDTH), lambda c: (0, c, 3)),
                  pl.BlockSpec((batch, L, SSM_WIDTH), lambda c: (0, c, 4)),
                  pl.BlockSpec((batch, L, SSM_BC), lambda c: (0, c, 10)),
                  pl.BlockSpec((batch, L, LANES), lambda c: (0, c, 0)),
                  pl.BlockSpec((SSM_CONV, cd), lambda c: (0, 0)),
                  pl.BlockSpec((1, cd), lambda c: (0, 0)),
                  pl.BlockSpec((1, LANES), lambda c: (0, 0)),
                  pl.BlockSpec((1, LANES), lambda c: (0, 0)),
                  pl.BlockSpec((1, LANES), lambda c: (0, 0)),
                  pl.BlockSpec((1, SSM_WIDTH), lambda c: (0, 0))],
        out_specs=pl.BlockSpec((batch, L, SSM_WIDTH), lambda c: (0, c, 0)),
        out_shape=jax.ShapeDtypeStruct((batch, seq, SSM_WIDTH), BF16),
        scratch_shapes=[pltpu.VMEM((batch, L + 8, cd), F32),
                        pltpu.VMEM((batch, SSM_STATE, SSM_WIDTH), F32)],
        compiler_params=_cparams("arbitrary"),
    )(proj3, proj3, proj3, dt3, conv_w, conv_b, dt_bias, a_log, d_skip, norm_w)
    return out.reshape(batch * seq, SSM_WIDTH)


def _out_proj_kernel(x_ref, att_ref, ssm_ref, w_ref, mod_ref, nw_ref, h_ref, hn_ref, raw_scr, *, nb):
    j = pl.program_id(0)

    def matmul(slot):
        raw_scr[slot] = (jnp.dot(att_ref[...], w_ref[0:ATT_WIDTH, :], preferred_element_type=F32)
                         + jnp.dot(ssm_ref[...], w_ref[ATT_WIDTH:, :], preferred_element_type=F32))

    def finish(slot):
        mod = mod_ref[0]
        h1 = x_ref[...] + mod[2:3, :] * raw_scr[slot]
        h_ref[...] = h1
        y = h1 * lax.rsqrt(jnp.mean(h1 * h1, axis=1, keepdims=True) + NORM_EPS) * nw_ref[...]
        hn_ref[...] = (y * (1.0 + mod[4:5, :]) + mod[3:4, :]).astype(BF16)

    @pl.when(j == 0)
    def _():
        matmul(0)

    for parity in range(2):
        @pl.when((j > 0) & (j < nb) & (j % 2 == parity))
        def _(parity=parity):
            matmul(parity)
            finish(1 - parity)

    @pl.when(j == nb)
    def _():
        finish((nb - 1) % 2)


def _out_proj(x2, att, ssm, w_out, mod3, norm2_w, seq):
    n, d = x2.shape
    tm = min(256, seq)
    nb = n // tm
    lag = lambda i: jnp.maximum(i - 1, 0)
    cur = lambda i: jnp.minimum(i, nb - 1)
    return pl.pallas_call(
        functools.partial(_out_proj_kernel, nb=nb),
        grid=(nb + 1,),
        in_specs=[pl.BlockSpec((tm, d), lambda i: (lag(i), 0)),
                  pl.BlockSpec((tm, ATT_WIDTH), lambda i: (cur(i), 0)),
                  pl.BlockSpec((tm, SSM_WIDTH), lambda i: (cur(i), 0)),
                  pl.BlockSpec((ATT_WIDTH + SSM_WIDTH, d), lambda i: (0, 0)),
                  pl.BlockSpec((1, 6, d), lambda i: (lag(i) * tm // seq, 0, 0)),
                  pl.BlockSpec((1, d), lambda i: (0, 0))],
        out_specs=[pl.BlockSpec((tm, d), lambda i: (lag(i), 0)),
                   pl.BlockSpec((tm, d), lambda i: (lag(i), 0))],
        out_shape=[jax.ShapeDtypeStruct((n, d), F32),
                   jax.ShapeDtypeStruct((n, d), BF16)],
        scratch_shapes=[pltpu.VMEM((2, tm, d), F32)],
        compiler_params=_cparams("arbitrary"),
    )(x2, att, ssm, w_out, mod3, norm2_w)


def _topk_rows(s, k, rows):
    row = lax.broadcasted_iota(I32, s.shape, 0).astype(F32)
    vals, idxs = [], []
    for _ in range(k):
        m, idx = _argmax_rows(s, rows)
        s = jnp.where(row == idx, -jnp.inf, s)
        vals.append(m)
        idxs.append(idx)
    return vals, idxs


def _argmax_rows(s, rows, payload=None):
    groups = range(rows // 8)
    vs = [s[8 * i:8 * i + 8] for i in groups]
    sub = lax.broadcasted_iota(I32, vs[0].shape, 0).astype(F32)
    rs = [sub + float(8 * i) for i in groups]
    ps = [payload[8 * i:8 * i + 8] for i in groups] if payload is not None else None
    while len(vs) > 1:
        nv, nr, npay = [], [], []
        for i in range(0, len(vs) - 1, 2):
            take_b = vs[i + 1] > vs[i]
            nv.append(jnp.where(take_b, vs[i + 1], vs[i]))
            nr.append(jnp.where(take_b, rs[i + 1], rs[i]))
            if ps is not None:
                npay.append(jnp.where(take_b, ps[i + 1], ps[i]))
        if len(vs) % 2:
            nv.append(vs[-1])
            nr.append(rs[-1])
            if ps is not None:
                npay.append(ps[-1])
        vs, rs, ps = nv, nr, (npay if ps is not None else None)
    m = jnp.max(vs[0], axis=0, keepdims=True)
    idx = jnp.min(jnp.where(vs[0] == m, rs[0], float(rows)), axis=0, keepdims=True)
    if ps is None:
        return m, idx
    return m, idx, jnp.max(jnp.where(rs[0] == idx, ps[0], -1.0), axis=0, keepdims=True)


_CAND = [(i, j) for i in range(PEER_TOPK) for j in range(PEER_TOPK) if (i + 1) * (j + 1) <= PEER_TOPK]


def _route_kernel(hn_ref, wq_ref, keys_ref, a_ref, b_ref, g_ref, top_scr, code_scr):
    t = hn_ref.shape[0]
    qp = jnp.dot(hn_ref[...], wq_ref[...], preferred_element_type=F32).astype(BF16)
    ncand = len(_CAND)
    pad = (-ncand) % 8
    for h in range(PEER_HEADS):
        sub = []
        for c in range(2):
            hc = 2 * h + c
            sc = lax.dot_general(keys_ref[hc], qp[:, hc * LANES:(hc + 1) * LANES], NT_DIMS,
                                 preferred_element_type=F32)
            sub.append(_topk_rows(sc, PEER_TOPK, PEER_NKEYS))
        (s1, i1), (s2, i2) = sub
        cand = jnp.concatenate([s1[i] + s2[j] for i, j in _CAND]
                               + [jnp.full((pad, t), -jnp.inf, F32)], axis=0)
        a_hi = [v * float(PEER_NKEYS) for v in i1]
        code = jnp.concatenate([a_hi[i] + i2[j] for i, j in _CAND] + [jnp.zeros((pad, t), F32)], axis=0)
        row = lax.broadcasted_iota(I32, cand.shape, 0).astype(F32)
        for kk in range(PEER_TOPK):
            m, idx, picked = _argmax_rows(cand, ncand + pad, code)
            slot = h * PEER_TOPK + kk
            top_scr[slot:slot + 1, :] = m
            code_scr[slot:slot + 1, :] = picked
            cand = jnp.where(row == idx, -jnp.inf, cand)
        top = top_scr[h * PEER_TOPK:(h + 1) * PEER_TOPK, :]
        e = jnp.exp(top - jnp.max(top, axis=0, keepdims=True))
        top_scr[h * PEER_TOPK:(h + 1) * PEER_TOPK, :] = e / jnp.sum(e, axis=0, keepdims=True)
    code_t = code_scr[...].T
    first = jnp.floor(code_t * (1.0 / PEER_NKEYS))
    a_ref[...] = first.astype(I32)
    b_ref[...] = (code_t - first * float(PEER_NKEYS)).astype(I32)
    g_ref[...] = top_scr[...].T


def _route(hn2, wq, keys):
    n, d = hn2.shape
    t = min(256, n)
    qd = wq.shape[1]
    out = jax.ShapeDtypeStruct((n, PEER_SLOTS), I32)
    return pl.pallas_call(
        _route_kernel,
        grid=(n // t,),
        in_specs=[pl.BlockSpec((t, d), lambda i: (i, 0)),
                  pl.BlockSpec((d, qd), lambda i: (0, 0)),
                  pl.BlockSpec(keys.shape, lambda i: (0, 0, 0))],
        out_specs=[pl.BlockSpec((t, PEER_SLOTS), lambda i: (i, 0))] * 3,
        out_shape=[out, out, jax.ShapeDtypeStruct((n, PEER_SLOTS), F32)],
        scratch_shapes=[pltpu.VMEM((PEER_SLOTS, t), F32),
                        pltpu.VMEM((PEER_SLOTS, t), F32)],
        compiler_params=_cparams("arbitrary"),
    )(hn2, wq, keys)


PAIR = 2 * PEER_NKEYS
DOWN_PAIRS = 8
DOWN_TOKENS = 1024
UP_KEYS = 16
UP_TOKENS = 512
TOKEN_UNROLL = 32
DENSE_PITCH = PEER_NKEYS + 8


def _peer_down_kernel(x_ref, dn_ref, a_ref, b_ref, pre_ref):
    j = pl.program_id(1)

    @pl.when(j == 0)
    def _():
        pre_ref[...] = jnp.zeros(pre_ref.shape, F32)

    x = x_ref[...]
    a = a_ref[...]
    b = b_ref[...]
    pre = pre_ref[...]
    for q in range(DOWN_PAIRS):
        p = lax.dot_general(x, dn_ref[q * PAIR:(q + 1) * PAIR, :], NT_DIMS,
                            preferred_element_type=F32)
        for half in range(2):
            g = jnp.take_along_axis(p[:, half * LANES:(half + 1) * LANES], b, axis=1)
            pre = jnp.where(a == 2 * (DOWN_PAIRS * j + q) + half, g, pre)
    pre_ref[...] = pre


def _peer_down(hn2, down16, aidx, bidx):
    n, d = hn2.shape
    t = min(DOWN_TOKENS, n)
    slot_spec = pl.BlockSpec((t, PEER_SLOTS), lambda i, j: (i, 0))
    return pl.pallas_call(
        _peer_down_kernel,
        grid=(n // t, down16.shape[0] // (DOWN_PAIRS * PAIR)),
        in_specs=[pl.BlockSpec((t, d), lambda i, j: (i, 0)),
                  pl.BlockSpec((DOWN_PAIRS * PAIR, d), lambda i, j: (j, 0)),
                  slot_spec, slot_spec],
        out_specs=slot_spec,
        out_shape=jax.ShapeDtypeStruct((n, PEER_SLOTS), F32),
        compiler_params=_cparams("arbitrary", "arbitrary"),
    )(hn2, down16, aidx, bidx)


U32 = jnp.uint32
HI_HALF = 0xFFFF0000


def _bf16_bits(x):
    return lax.bitcast_convert_type(x.astype(BF16).astype(F32), U32)


def _peer_up_kernel(pre_ref, g_ref, a_ref, b_ref, up_ref, h_ref, mod_ref, o_ref,
                    act_scr, dense_scr):
    j = pl.program_id(1)
    t = pre_ref.shape[0]
    half = t // 2
    nk = PEER_NKEYS

    @pl.when(j == 0)
    def _():
        pre = pre_ref[...]
        act_scr[...] = 0.5 * pre * (1.0 + lax.erf(pre * (1.0 / math.sqrt(2.0)))) * g_ref[...]
        o_ref[...] = jnp.zeros(o_ref.shape, F32)
        row = lax.broadcasted_iota(I32, (nk, PEER_SLOTS), 0)

        def scatter(a_row, b_row, c_row):
            xa = jnp.where(row == a_row, c_row, 0.0).astype(BF16)
            yb = jnp.where(row == b_row, 1.0, 0.0).astype(BF16)
            return lax.dot_general(xa, yb, NT_DIMS, preferred_element_type=F32)

        def body(i, carry):
            for grp in range(TOKEN_UNROLL // 8):
                base = pl.multiple_of(i * TOKEN_UNROLL + grp * 8, 8)
                lo = [r[pl.ds(base, 8), :] for r in (a_ref, b_ref, act_scr)]
                hi = [r[pl.ds(base + half, 8), :] for r in (a_ref, b_ref, act_scr)]
                for u in range(8):
                    d_lo = scatter(*[v[u:u + 1, :] for v in lo])
                    d_hi = scatter(*[v[u:u + 1, :] for v in hi])
                    dense_scr[pl.ds(pl.multiple_of((base + u) * DENSE_PITCH, 8), nk), :] = (
                        lax.shift_right_logical(_bf16_bits(d_lo), U32(16)) | _bf16_bits(d_hi))
            return carry

        lax.fori_loop(0, half // TOKEN_UNROLL, body, 0)

    words = [dense_scr[pl.ds(UP_KEYS * j + u, half, stride=DENSE_PITCH), :] for u in range(UP_KEYS)]
    lo = jnp.concatenate([lax.bitcast_convert_type(lax.shift_left(w, U32(16)), F32) for w in words], axis=1)
    hi = jnp.concatenate([lax.bitcast_convert_type(w & U32(HI_HALF), F32) for w in words], axis=1)
    lhs = jnp.concatenate([lo, hi], axis=0).astype(BF16)
    o_ref[...] += jnp.dot(lhs, up_ref[...], preferred_element_type=F32)

    @pl.when(j == pl.num_programs(1) - 1)
    def _():
        o_ref[...] = h_ref[...] + mod_ref[0][5:6, :] * o_ref[...]


def _peer_up(pre, gate, aidx, bidx, up16, h1, mod3, seq):
    n, d = h1.shape
    t = min(UP_TOKENS, seq)
    slot_spec = pl.BlockSpec((t, PEER_SLOTS), lambda i, j: (i, 0))
    return pl.pallas_call(
        _peer_up_kernel,
        grid=(n // t, up16.shape[0] // (UP_KEYS * PEER_NKEYS)),
        in_specs=[slot_spec, slot_spec, slot_spec, slot_spec,
                  pl.BlockSpec((UP_KEYS * PEER_NKEYS, d), lambda i, j: (j, 0)),
                  pl.BlockSpec((t, d), lambda i, j: (i, 0)),
                  pl.BlockSpec((1, 6, d), lambda i, j: (i * t // seq, 0, 0))],
        out_specs=pl.BlockSpec((t, d), lambda i, j: (i, 0)),
        out_shape=jax.ShapeDtypeStruct((n, d), F32),
        scratch_shapes=[pltpu.VMEM((t, PEER_SLOTS), F32),
                        pltpu.VMEM((t // 2 * DENSE_PITCH, PEER_NKEYS), U32)],
        compiler_params=_cparams("arbitrary", "arbitrary"),
    )(pre, gate, aidx, bidx, up16, h1, mod3)


def _pad_lanes(v):
    return jnp.pad(v.astype(F32), (0, LANES - v.shape[0])).reshape(1, LANES)


def _layer(h2, mod3, l, batch, seq, norm1_w, w_in, q_norm_w, k_norm_w, rel_bias, lambda_q1, lambda_k1,
           lambda_q2, lambda_k2, subln_w, conv_w, conv_b, dt_bias, a_log, d_skip, ssm_norm_w, w_out,
           norm2_w, peer_wq, peer_keys, expert_down, expert_up):
    d = h2.shape[1]
    lam_init = 0.8 - 0.6 * math.exp(-0.3 * l)
    w16 = w_in.astype(BF16)
    w_dt = jnp.pad(w16[:, MAIN_COLS:], ((0, 0), (0, LANES - SSM_HEADS)))
    qn = jnp.tile(q_norm_w.astype(F32) * (HEAD_DIM ** -0.5 * LOG2E), 2).reshape(1, LANES)
    kn = jnp.tile(k_norm_w.astype(F32), 2).reshape(1, LANES)
    proj, dt_raw = _in_proj(h2, norm1_w.reshape(1, d), mod3, w16, w_dt, qn, kn, seq)

    lamv = jnp.pad(jnp.stack([lambda_q1, lambda_k1, lambda_q2, lambda_k2]).astype(F32),
                   ((0, 4), (0, LANES - HEAD_DIM)))
    att = _attention(rel_bias.astype(F32).reshape(-1), proj, lamv, subln_w.reshape(1, LANES),
                     batch, seq, lam_init)
    ssm = _ssd(proj, dt_raw, conv_w, conv_b.reshape(1, -1), _pad_lanes(dt_bias), _pad_lanes(a_log),
               _pad_lanes(d_skip), ssm_norm_w.reshape(1, -1), batch, seq)
    h1, hn2 = _out_proj(h2, att, ssm, w_out.astype(BF16), mod3, norm2_w.reshape(1, d), seq)

    keys = peer_keys.astype(BF16).reshape(2 * PEER_HEADS, PEER_NKEYS, -1)
    aidx, bidx, gate = _route(hn2, peer_wq.astype(BF16), keys)
    pre = _peer_down(hn2, expert_down.astype(BF16), aidx, bidx)
    return _peer_up(pre, gate, aidx, bidx, expert_up.astype(BF16), h1, mod3, seq)


def kernel(x, c, ada_w, ada_b, norm1_w, w_in, q_norm_w, k_norm_w, rel_bias, lambda_q1, lambda_k1, lambda_q2, lambda_k2, subln_w, conv_w, conv_b, dt_bias, a_log, d_skip, ssm_norm_w, w_out, norm2_w, peer_wq, peer_keys, expert_down, expert_up):
    batch, seq, d = x.shape
    depth = ada_w.shape[0]
    assert seq % (2 * ATT_CHUNK) == 0 and seq % min(1024, seq) == 0, "unsupported sequence length"
    assert batch <= 8 and d == ATT_WIDTH + SSM_WIDTH, "unsupported batch / model width"
    h2 = x.reshape(batch * seq, d)
    c_pad = jnp.pad(c, ((0, 8 - batch), (0, 0)))
    for l in range(depth):
        mod = _ada(c_pad, ada_w[l], ada_b[l].reshape(1, -1))
        mod3 = mod[:batch].reshape(batch, 6, d)
        h2 = _layer(h2, mod3, l, batch, seq, norm1_w[l], w_in[l], q_norm_w[l], k_norm_w[l], rel_bias,
                    lambda_q1[l], lambda_k1[l], lambda_q2[l], lambda_k2[l], subln_w[l], conv_w[l],
                    conv_b[l], dt_bias[l], a_log[l], d_skip[l], ssm_norm_w[l], w_out[l], norm2_w[l],
                    peer_wq[l], peer_keys[l], expert_down[l], expert_up[l])
    return h2.reshape(batch, seq, d)
```

```python
import functools
import math

import jax
import jax.numpy as jnp
from jax import lax
from jax.experimental import pallas as pl
from jax.experimental.pallas import tpu as pltpu

F32 = jnp.float32
BF16 = jnp.bfloat16
I32 = jnp.int32

LANES = 128
VMEM_LIMIT = 56 * 1024 * 1024

NORM_EPS = 1e-6
HEAD_DIM = 64
ATT_HEADS = 8
ATT_WIDTH = 1024
SSM_WIDTH = 1024
SSM_HEADS = 16
SSM_GROUPS = 2
SSM_STATE = 128
SSM_CONV = 4
SSM_CHUNK = 128
SSM_BC = 2 * SSM_GROUPS * SSM_STATE
REL_BUCKETS = 32
REL_MAX_DIST = 128
PEER_HEADS = 8
PEER_NKEYS = 128
PEER_TOPK = 16
PEER_SLOTS = PEER_HEADS * PEER_TOPK
MAIN_COLS = 3 * ATT_WIDTH + SSM_WIDTH + SSM_WIDTH + SSM_BC
NEG = -1e30
LOG2E = math.log2(math.e)

NT_DIMS = (((1,), (1,)), ((), ()))


def _cparams(*sem):
    return pltpu.CompilerParams(dimension_semantics=sem, vmem_limit_bytes=VMEM_LIMIT)


def _sigmoid(x):
    return 1.0 / (1.0 + jnp.exp(-x))


def _ada_kernel(c_ref, w_ref, b_ref, o_ref):
    c = c_ref[...]
    sc = (c * _sigmoid(c)).astype(BF16)
    o_ref[...] = jnp.dot(sc, w_ref[...].astype(BF16), preferred_element_type=F32) + b_ref[...]


def _ada(c_pad, ada_w, ada_b):
    rows, d = c_pad.shape
    n = ada_w.shape[1]
    tn = 1536
    return pl.pallas_call(
        _ada_kernel,
        grid=(n // tn,),
        in_specs=[pl.BlockSpec((rows, d), lambda j: (0, 0)),
                  pl.BlockSpec((d, tn), lambda j: (0, j)),
                  pl.BlockSpec((1, tn), lambda j: (0, j))],
        out_specs=pl.BlockSpec((rows, tn), lambda j: (0, j)),
        out_shape=jax.ShapeDtypeStruct((rows, n), F32),
        compiler_params=_cparams("arbitrary"),
    )(c_pad, ada_w, ada_b)


def _group_rms(blk, w_row, lo):
    sq = blk * blk
    s_all = jnp.sum(sq, axis=1, keepdims=True)
    s_lo = jnp.sum(jnp.where(lo, sq, 0.0), axis=1, keepdims=True)
    s = jnp.where(lo, s_lo, s_all - s_lo)
    return blk * lax.rsqrt(s * (1.0 / HEAD_DIM) + NORM_EPS) * w_row


def _in_proj_kernel(x_ref, nw_ref, mod_ref, w_ref, wdt_ref, qn_ref, kn_ref,
                    o_ref, dt_ref, hn_scr, raw_scr, *, tn, ntiles):
    j = pl.program_id(1)
    qk_tiles = 2 * ATT_WIDTH // tn

    def finish(k):
        raw = raw_scr[k % 2]
        if k < qk_tiles:
            lo = lax.broadcasted_iota(I32, (1, LANES), 1) < HEAD_DIM
            w_row = qn_ref[...] if k < qk_tiles // 2 else kn_ref[...]
            for cb in range(tn // LANES):
                blk = raw[:, cb * LANES:(cb + 1) * LANES]
                o_ref[:, cb * LANES:(cb + 1) * LANES] = _group_rms(blk, w_row, lo).astype(BF16)
        else:
            o_ref[...] = raw.astype(BF16)

    for step in range(ntiles + 1):
        @pl.when(j == step)
        def _(step=step):
            if step == 0:
                mod = mod_ref[0]
                rows = x_ref.shape[0] // 2
                for r0 in (0, rows):
                    x = x_ref[r0:r0 + rows, :]
                    y = x * lax.rsqrt(jnp.mean(x * x, axis=1, keepdims=True) + NORM_EPS) * nw_ref[...]
                    hn = (y * (1.0 + mod[1:2, :]) + mod[0:1, :]).astype(BF16)
                    hn_scr[r0:r0 + rows, :] = hn
                    dt_ref[r0:r0 + rows, :] = jnp.dot(hn, wdt_ref[...], preferred_element_type=F32)
                    raw_scr[0, r0:r0 + rows, :] = jnp.dot(hn, w_ref[...], preferred_element_type=F32)
            elif step < ntiles:
                raw_scr[step % 2] = jnp.dot(hn_scr[...], w_ref[...], preferred_element_type=F32)
            if step > 0:
                finish(step - 1)


def _in_proj(x2, norm_w, mod3, w_main, w_dt, qn, kn, seq):
    n, d = x2.shape
    tm = min(1024, seq)
    tn = 512
    ntiles = MAIN_COLS // tn
    return pl.pallas_call(
        functools.partial(_in_proj_kernel, tn=tn, ntiles=ntiles),
        grid=(n // tm, ntiles + 1),
        in_specs=[pl.BlockSpec((tm, d), lambda i, j: (i, 0)),
                  pl.BlockSpec((1, d), lambda i, j: (0, 0)),
                  pl.BlockSpec((1, 6, d), lambda i, j: (i * tm // seq, 0, 0)),
                  pl.BlockSpec((d, tn), lambda i, j: (0, jnp.minimum(j, ntiles - 1))),
                  pl.BlockSpec((d, LANES), lambda i, j: (0, 0)),
                  pl.BlockSpec((1, LANES), lambda i, j: (0, 0)),
                  pl.BlockSpec((1, LANES), lambda i, j: (0, 0))],
        out_specs=[pl.BlockSpec((tm, tn), lambda i, j: (i, jnp.maximum(j - 1, 0))),
                   pl.BlockSpec((tm, LANES), lambda i, j: (i, 0))],
        out_shape=[jax.ShapeDtypeStruct((n, MAIN_COLS), BF16),
                   jax.ShapeDtypeStruct((n, LANES), F32)],
        scratch_shapes=[pltpu.VMEM((tm, d), BF16),
                        pltpu.VMEM((2, tm, tn), F32)],
        compiler_params=_cparams("arbitrary", "arbitrary"),
    )(x2, norm_w, mod3, w_main, w_dt, qn, kn)


ATT_GROUP = 2
ATT_UNROLL = 4
ATT_CHUNK = 256
VT_ROWS = LANES + 16


def _attn_kernel(relb_ref, q_ref, k_ref, vt_ref, lamv_ref, subw_ref, o_ref,
                 q2t_scr, acc_scr, bias_scr, s_scr, smax_scr, *, t, lam_init):
    hp = pl.program_id(1)
    qi = pl.program_id(2)
    tq = 2 * t
    nchunk = vt_ref.shape[0] // ATT_GROUP

    @pl.when(qi == 0)
    def _():
        kk = lax.broadcasted_iota(I32, (t, t), 0)
        qq = lax.broadcasted_iota(I32, (t, t), 1)
        max_exact = REL_BUCKETS // 2
        buckets = []
        for off in (0, t):
            nn = jnp.maximum(qq - kk + off, 0)
            nf = jnp.maximum(nn, 1).astype(F32)
            large = max_exact + (jnp.log(nf / max_exact) / math.log(REL_MAX_DIST / max_exact)
                                 * (REL_BUCKETS - max_exact)).astype(I32)
            buckets.append(jnp.where(nn < max_exact, nn, jnp.minimum(large, REL_BUCKETS - 1)))
        zeros = jnp.zeros((t, t), F32)
        masked = jnp.full((t, t), NEG, F32)
        for hh in range(ATT_GROUP):
            head = hp * ATT_GROUP + hh
            for m in range(2):
                far = relb_ref[(REL_BUCKETS - 1) * 2 * ATT_HEADS + head * 2 + m]
                diag, sub = zeros, zeros
                for b in range(REL_BUCKETS - 1):
                    delta = (relb_ref[b * 2 * ATT_HEADS + head * 2 + m] - far) * LOG2E
                    diag = jnp.where(buckets[0] == b, delta, diag)
                    sub = jnp.where(buckets[1] == b, delta, sub)
                diag = jnp.where(qq >= kk, diag, NEG)
                for tile, (first, last) in enumerate(((sub, zeros), (diag, sub), (masked, diag))):
                    bias_scr[hh, tile, :, m * tq:m * tq + t] = first
                    bias_scr[hh, tile, :, m * tq + t:(m + 1) * tq] = last

    d_lo = lax.broadcasted_iota(I32, (LANES, tq), 0) < HEAD_DIM
    for hh in range(ATT_GROUP):
        qt = q_ref[:, hh * LANES:(hh + 1) * LANES].astype(F32).T
        q2t_scr[hh, :, 0:tq] = jnp.where(d_lo, qt, 0.0).astype(BF16)
        q2t_scr[hh, :, tq:2 * tq] = jnp.where(d_lo, 0.0, qt).astype(BF16)
        acc_scr[hh] = jnp.zeros((VT_ROWS, 2 * tq), F32)

    def scores(c, hh):
        k_c = k_ref[pl.ds(pl.multiple_of(c * t, t), t), hh * LANES:(hh + 1) * LANES]
        return jnp.dot(k_c, q2t_scr[hh], preferred_element_type=F32)

    def issue(c, slot, tile):
        for hh in range(ATT_GROUP):
            s = scores(c, hh)
            if tile is not None:
                s = s + bias_scr[hh, tile]
            s_scr[slot, hh] = s
            row = slot * ATT_GROUP + hh
            smax_scr[row:row + 1, :] = jnp.max(s, axis=0, keepdims=True)

    ones_rows = jnp.where(lax.broadcasted_iota(I32, (VT_ROWS - LANES, t), 0) == 0, 1.0, 0.0).astype(BF16)

    def consume(c, carry, slot):
        out = []
        for hh in range(ATT_GROUP):
            m_prev = carry[hh]
            row = slot * ATT_GROUP + hh
            m_new = jnp.maximum(m_prev, smax_scr[row:row + 1, :])
            alpha = jnp.exp2(m_prev - m_new)
            p = jnp.exp2(s_scr[slot, hh] - m_new).astype(BF16)
            vt_c = jnp.concatenate([vt_ref[hh * nchunk + c], ones_rows], axis=0)
            pv = jnp.dot(vt_c, p, preferred_element_type=F32)
            acc_scr[hh] = alpha * acc_scr[hh] + pv
            out.append(m_new)
        return tuple(out)

    def run(first, tiles, more, reissue, carry):
        if reissue:
            issue(first, 0, tiles[0])
        for k in range(len(tiles)):
            if k + 1 < len(tiles):
                issue(first + k + 1, (k + 1) % 2, tiles[k + 1])
            elif more:
                issue(first + k + 1, (k + 1) % 2, None)
            carry = consume(first + k, carry, k % 2)
        return carry

    nfar = jnp.maximum(2 * qi - 1, 0)
    ntrip = nfar // ATT_UNROLL
    issue(0, 0, None)

    def trip(i, carry):
        return run(i * ATT_UNROLL, [None] * ATT_UNROLL, True, False, carry)

    init = tuple(jnp.full((1, 2 * tq), NEG, F32) for _ in range(ATT_GROUP))
    carry = lax.fori_loop(0, ntrip, trip, init)
    rest = ntrip * ATT_UNROLL
    tails = [functools.partial(run, rest, [None] * r + [0, 1, 2], False, False)
             for r in range(1, ATT_UNROLL, 2)]
    tails.append(functools.partial(run, rest, [1, 2], False, True))
    carry = lax.switch(jnp.where(qi == 0, len(tails) - 1, (nfar - rest) // 2), tails, carry)

    lv = lamv_ref[...]
    lam = (jnp.exp(jnp.sum(lv[0:1, :] * lv[1:2, :], axis=1, keepdims=True))
           - jnp.exp(jnp.sum(lv[2:3, :] * lv[3:4, :], axis=1, keepdims=True)) + lam_init)
    for hh in range(ATT_GROUP):
        acc = acc_scr[hh, 0:LANES, :] * (1.0 / acc_scr[hh, LANES:LANES + 1, :])
        o = (acc[:, 0:tq] - lam * acc[:, tq:2 * tq]).T
        o = o * lax.rsqrt(jnp.mean(o * o, axis=1, keepdims=True) + NORM_EPS) * subw_ref[...]
        o_ref[:, hh * LANES:(hh + 1) * LANES] = (o * (1.0 - lam_init)).astype(BF16)


def _attention(relb, proj, lamv, subw, batch, seq, lam_init):
    n = batch * seq
    t = ATT_CHUNK
    tq = 2 * t
    nq = seq // tq
    nk = seq // t
    gw = ATT_GROUP * LANES
    ngroups = ATT_HEADS // ATT_GROUP
    vt = proj[:, 2 * ATT_WIDTH:3 * ATT_WIDTH].reshape(batch, nk, t, ATT_HEADS, LANES)
    vt = vt.transpose(0, 3, 1, 4, 2).reshape(batch * ATT_HEADS * nk, LANES, t)
    return pl.pallas_call(
        functools.partial(_attn_kernel, t=t, lam_init=lam_init),
        grid=(batch, ngroups, nq),
        in_specs=[pl.BlockSpec(memory_space=pltpu.SMEM),
                  pl.BlockSpec((tq, gw), lambda b, g, i: (b * nq + i, g)),
                  pl.BlockSpec((seq, gw), lambda b, g, i: (b, ngroups + g)),
                  pl.BlockSpec((ATT_GROUP * nk, LANES, t), lambda b, g, i: (b * ngroups + g, 0, 0)),
                  pl.BlockSpec((8, LANES), lambda b, g, i: (0, 0)),
                  pl.BlockSpec((1, LANES), lambda b, g, i: (0, 0))],
        out_specs=pl.BlockSpec((tq, gw), lambda b, g, i: (b * nq + i, g)),
        out_shape=jax.ShapeDtypeStruct((n, ATT_WIDTH), BF16),
        scratch_shapes=[pltpu.VMEM((ATT_GROUP, LANES, 2 * tq), BF16),
                        pltpu.VMEM((ATT_GROUP, VT_ROWS, 2 * tq), F32),
                        pltpu.VMEM((ATT_GROUP, 3, t, 2 * tq), F32),
                        pltpu.VMEM((2, ATT_GROUP, t, 2 * tq), F32),
                        pltpu.VMEM((8, 2 * tq), F32)],
        compiler_params=_cparams("arbitrary", "arbitrary", "arbitrary"),
    )(relb, proj, proj, vt, lamv, subw)


def _split3(x):
    hi = x.astype(BF16)
    r1 = x - hi.astype(F32)
    mid = r1.astype(BF16)
    lo = (r1 - mid.astype(F32)).astype(BF16)
    return hi, mid, lo


def _ssd_kernel(z_ref, xs_ref, bc_ref, dt_ref, cw_ref, cb_ref, dtb_ref, alog_ref, dskip_ref, nw_ref,
                o_ref, xpad_scr, state_scr):
    L = SSM_CHUNK
    W = SSM_WIDTH
    P2 = LANES
    nblk = W // P2
    gw = W // SSM_GROUPS

    @pl.when(pl.program_id(0) == 0)
    def _():
        xpad_scr[0:8, :] = jnp.zeros((8, W + SSM_BC), F32)
        state_scr[...] = jnp.zeros((SSM_STATE, W), F32)

    xpad_scr[8:8 + L, 0:W] = xs_ref[...].astype(F32)
    xpad_scr[8:8 + L, W:W + SSM_BC] = bc_ref[...].astype(F32)
    conv = cb_ref[...] + cw_ref[0:1, :] * xpad_scr[5:5 + L, :]
    for kk in range(1, SSM_CONV):
        conv = conv + cw_ref[kk:kk + 1, :] * xpad_scr[5 + kk:5 + kk + L, :]
    xpad_scr[0:8, :] = xpad_scr[L:L + 8, :]
    u = conv * _sigmoid(conv)

    dtr = dt_ref[...] + dtb_ref[...]
    dt = jnp.maximum(dtr, 0.0) + jnp.log1p(jnp.exp(-jnp.abs(dtr)))
    a = -jnp.exp(alog_ref[...])
    da = dt * a

    ri = lax.broadcasted_iota(I32, (L, L), 0)
    ci = lax.broadcasted_iota(I32, (L, L), 1)
    tril = ri >= ci
    tri = jnp.where(tril, 1.0, 0.0).astype(BF16)
    hi, mid, lo3 = _split3(da)
    a_cs = (jnp.dot(tri, hi, preferred_element_type=F32) + jnp.dot(tri, mid, preferred_element_type=F32)
            + jnp.dot(tri, lo3, preferred_element_type=F32))
    a_cs_t = a_cs.T
    a_last = a_cs[L - 1:L, :]
    e_cs = jnp.exp(a_cs)
    dt_ds = dt * jnp.exp(a_last - a_cs)

    lane_lo = lax.broadcasted_iota(I32, (1, P2), 1) < HEAD_DIM

    def expand(mat, i):
        return jnp.where(lane_lo, mat[:, 2 * i:2 * i + 1], mat[:, 2 * i + 1:2 * i + 2])

    y_blocks = []
    for g in range(SSM_GROUPS):
        bm = u[:, W + g * SSM_STATE:W + (g + 1) * SSM_STATE]
        cm = u[:, W + (SSM_GROUPS + g) * SSM_STATE:W + (SSM_GROUPS + g + 1) * SSM_STATE]
        bm16 = bm.astype(BF16)
        cm16 = cm.astype(BF16)
        cb = lax.dot_general(cm16, bm16, NT_DIMS, preferred_element_type=F32)
        st_g = state_scr[:, g * gw:(g + 1) * gw]
        y_off = jnp.dot(cm16, st_g.astype(BF16), preferred_element_type=F32)
        xd_blocks = []
        for ib in range(nblk // SSM_GROUPS):
            i = g * (nblk // SSM_GROUPS) + ib
            xs_blk = u[:, i * P2:(i + 1) * P2]
            xc = xs_blk * expand(dt, i)
            yd = jnp.zeros((L, P2), F32)
            for hh in range(2):
                head = 2 * i + hh
                seg = a_cs[:, head:head + 1] - a_cs_t[head:head + 1, :]
                wmat = (cb * jnp.where(tril, jnp.exp(seg), 0.0)).astype(BF16)
                keep = lane_lo if hh == 0 else jnp.logical_not(lane_lo)
                yd = yd + jnp.dot(wmat, jnp.where(keep, xc, 0.0).astype(BF16), preferred_element_type=F32)
            y = yd + y_off[:, ib * P2:(ib + 1) * P2] * expand(e_cs, i) + expand(dskip_ref[...], i) * xs_blk
            zf = z_ref[:, i * P2:(i + 1) * P2].astype(F32)
            y_blocks.append(y * (zf * _sigmoid(zf)))
            xd_blocks.append((xs_blk * expand(dt_ds, i)).astype(BF16))
        xd = jnp.concatenate(xd_blocks, axis=1)
        st_new = jnp.dot(bm.T.astype(BF16), xd, preferred_element_type=F32)
        decay = jnp.concatenate([expand(jnp.exp(a_last), g * (nblk // SSM_GROUPS) + ib)
                                 for ib in range(nblk // SSM_GROUPS)], axis=1)
        state_scr[:, g * gw:(g + 1) * gw] = st_g * decay + st_new

    per_g = nblk // SSM_GROUPS
    for g in range(SSM_GROUPS):
        blks = y_blocks[g * per_g:(g + 1) * per_g]
        ss = sum(jnp.sum(b * b, axis=1, keepdims=True) for b in blks) * (1.0 / gw)
        inv = lax.rsqrt(ss + NORM_EPS)
        for ib, b in enumerate(blks):
            i = g * per_g + ib
            o_ref[:, i * P2:(i + 1) * P2] = (b * inv * nw_ref[:, i * P2:(i + 1) * P2]).astype(BF16)


def _ssd_batched_kernel(z_ref, xs_ref, bc_ref, dt_ref, cw_ref, cb_ref, dtb_ref, alog_ref, dskip_ref, nw_ref,
                        o_ref, xpad_scr, state_scr):
    for b in range(z_ref.shape[0]):
        _ssd_kernel(z_ref.at[b], xs_ref.at[b], bc_ref.at[b], dt_ref.at[b], cw_ref, cb_ref, dtb_ref,
                    alog_ref, dskip_ref, nw_ref, o_ref.at[b], xpad_scr.at[b], state_scr.at[b])


def _ssd(proj, dt_raw, conv_w, conv_b, dt_bias, a_log, d_skip, norm_w, batch, seq):
    L = SSM_CHUNK
    cd = SSM_WIDTH + SSM_BC
    proj3 = proj.reshape(batch, seq, MAIN_COLS)
    dt3 = dt_raw.reshape(batch, seq, LANES)
    out = pl.pallas_call(
        _ssd_batched_kernel,
        grid=(seq // L,),
        in_specs=[pl.BlockSpec((batch, L, SSM_WIDTH), lambda c: (0, c, 3)),
                  pl.BlockSpec((batch, L, SSM_WIDTH), lambda c: (0, c, 4)),
                  pl.BlockSpec((batch, L, SSM_BC), lambda c: (0, c, 10)),
                  pl.BlockSpec((batch, L, LANES), lambda c: (0, c, 0)),
                  pl.BlockSpec((SSM_CONV, cd), lambda c: (0, 0)),
                  pl.BlockSpec((1, cd), lambda c: (0, 0)),
                  pl.BlockSpec((1, LANES), lambda c: (0, 0)),
                  pl.BlockSpec((1, LANES), lambda c: (0, 0)),
                  pl.BlockSpec((1, LANES), lambda c: (0, 0)),
                  pl.BlockSpec((1, SSM_WIDTH), lambda c: (0, 0))],
        out_specs=pl.BlockSpec((batch, L, SSM_WIDTH), lambda c: (0, c, 0)),
        out_shape=jax.ShapeDtypeStruct((batch, seq, SSM_WIDTH), BF16),
        scratch_shapes=[pltpu.VMEM((batch, L + 8, cd), F32),
                        pltpu.VMEM((batch, SSM_STATE, SSM_WIDTH), F32)],
        compiler_params=_cparams("arbitrary"),
    )(proj3, proj3, proj3, dt3, conv_w, conv_b, dt_bias, a_log, d_skip, norm_w)
    return out.reshape(batch * seq, SSM_WIDTH)


def _out_proj_kernel(x_ref, att_ref, ssm_ref, w_ref, mod_ref, nw_ref, h_ref, hn_ref):
    mix = (jnp.dot(att_ref[...], w_ref[0:ATT_WIDTH, :], preferred_element_type=F32)
           + jnp.dot(ssm_ref[...], w_ref[ATT_WIDTH:, :], preferred_element_type=F32))
    mod = mod_ref[0]
    h1 = x_ref[...] + mod[2:3, :] * mix
    h_ref[...] = h1
    y = h1 * lax.rsqrt(jnp.mean(h1 * h1, axis=1, keepdims=True) + NORM_EPS) * nw_ref[...]
    hn_ref[...] = (y * (1.0 + mod[4:5, :]) + mod[3:4, :]).astype(BF16)


def _out_proj(x2, att, ssm, w_out, mod3, norm2_w, seq):
    n, d = x2.shape
    tm = min(256, seq)
    return pl.pallas_call(
        _out_proj_kernel,
        grid=(n // tm,),
        in_specs=[pl.BlockSpec((tm, d), lambda i: (i, 0)),
                  pl.BlockSpec((tm, ATT_WIDTH), lambda i: (i, 0)),
                  pl.BlockSpec((tm, SSM_WIDTH), lambda i: (i, 0)),
                  pl.BlockSpec((ATT_WIDTH + SSM_WIDTH, d), lambda i: (0, 0)),
                  pl.BlockSpec((1, 6, d), lambda i: (i * tm // seq, 0, 0)),
                  pl.BlockSpec((1, d), lambda i: (0, 0))],
        out_specs=[pl.BlockSpec((tm, d), lambda i: (i, 0)),
                   pl.BlockSpec((tm, d), lambda i: (i, 0))],
        out_shape=[jax.ShapeDtypeStruct((n, d), F32),
                   jax.ShapeDtypeStruct((n, d), BF16)],
        compiler_params=_cparams("arbitrary"),
    )(x2, att, ssm, w_out, mod3, norm2_w)


def _topk_rows(s, k, rows):
    row = lax.broadcasted_iota(I32, s.shape, 0).astype(F32)
    vals, idxs = [], []
    for _ in range(k):
        m, idx = _argmax_rows(s, rows)
        s = jnp.where(row == idx, -jnp.inf, s)
        vals.append(m)
        idxs.append(idx)
    return vals, idxs


def _argmax_rows(s, rows):
    vs = [s[8 * i:8 * i + 8] for i in range(rows // 8)]
    sub = lax.broadcasted_iota(I32, vs[0].shape, 0).astype(F32)
    rs = [sub + float(8 * i) for i in range(rows // 8)]
    while len(vs) > 1:
        nv, nr = [], []
        for i in range(0, len(vs) - 1, 2):
            take_b = vs[i + 1] > vs[i]
            nv.append(jnp.where(take_b, vs[i + 1], vs[i]))
            nr.append(jnp.where(take_b, rs[i + 1], rs[i]))
        if len(vs) % 2:
            nv.append(vs[-1])
            nr.append(rs[-1])
        vs, rs = nv, nr
    m = jnp.max(vs[0], axis=0, keepdims=True)
    idx = jnp.min(jnp.where(vs[0] == m, rs[0], float(rows)), axis=0, keepdims=True)
    return m, idx


_CAND = [(i, j) for i in range(PEER_TOPK) for j in range(PEER_TOPK) if (i + 1) * (j + 1) <= PEER_TOPK]


def _route_kernel(hn_ref, wq_ref, keys_ref, a_ref, b_ref, g_ref, top_scr, code_scr):
    t = hn_ref.shape[0]
    qp = jnp.dot(hn_ref[...], wq_ref[...], preferred_element_type=F32).astype(BF16)
    ncand = len(_CAND)
    pad = (-ncand) % 8
    for h in range(PEER_HEADS):
        sub = []
        for c in range(2):
            hc = 2 * h + c
            sc = lax.dot_general(keys_ref[hc], qp[:, hc * LANES:(hc + 1) * LANES], NT_DIMS,
                                 preferred_element_type=F32)
            sub.append(_topk_rows(sc, PEER_TOPK, PEER_NKEYS))
        (s1, i1), (s2, i2) = sub
        cand = jnp.concatenate([s1[i] + s2[j] for i, j in _CAND]
                               + [jnp.full((pad, t), -jnp.inf, F32)], axis=0)
        a_hi = [v * float(PEER_NKEYS) for v in i1]
        code = jnp.concatenate([a_hi[i] + i2[j] for i, j in _CAND] + [jnp.zeros((pad, t), F32)], axis=0)
        row = lax.broadcasted_iota(I32, cand.shape, 0).astype(F32)
        for kk in range(PEER_TOPK):
            m, idx = _argmax_rows(cand, ncand + pad)
            sel = row == idx
            slot = h * PEER_TOPK + kk
            top_scr[slot:slot + 1, :] = m
            code_scr[slot:slot + 1, :] = jnp.max(jnp.where(sel, code, -1.0), axis=0, keepdims=True)
            cand = jnp.where(sel, -jnp.inf, cand)
        top = top_scr[h * PEER_TOPK:(h + 1) * PEER_TOPK, :]
        e = jnp.exp(top - jnp.max(top, axis=0, keepdims=True))
        top_scr[h * PEER_TOPK:(h + 1) * PEER_TOPK, :] = e / jnp.sum(e, axis=0, keepdims=True)
    code_t = code_scr[...].T
    first = jnp.floor(code_t * (1.0 / PEER_NKEYS))
    a_ref[...] = first.astype(I32)
    b_ref[...] = (code_t - first * float(PEER_NKEYS)).astype(I32)
    g_ref[...] = top_scr[...].T


def _route(hn2, wq, keys):
    n, d = hn2.shape
    t = min(256, n)
    qd = wq.shape[1]
    out = jax.ShapeDtypeStruct((n, PEER_SLOTS), I32)
    return pl.pallas_call(
        _route_kernel,
        grid=(n // t,),
        in_specs=[pl.BlockSpec((t, d), lambda i: (i, 0)),
                  pl.BlockSpec((d, qd), lambda i: (0, 0)),
                  pl.BlockSpec(keys.shape, lambda i: (0, 0, 0))],
        out_specs=[pl.BlockSpec((t, PEER_SLOTS), lambda i: (i, 0))] * 3,
        out_shape=[out, out, jax.ShapeDtypeStruct((n, PEER_SLOTS), F32)],
        scratch_shapes=[pltpu.VMEM((PEER_SLOTS, t), F32),
                        pltpu.VMEM((PEER_SLOTS, t), F32)],
        compiler_params=_cparams("arbitrary"),
    )(hn2, wq, keys)


PAIR = 2 * PEER_NKEYS
DOWN_PAIRS = 16
DOWN_TOKENS = 512
UP_KEYS = 16
UP_TOKENS = 512
TOKEN_UNROLL = 32
DENSE_PITCH = PEER_NKEYS + 8


def _peer_down_kernel(x_ref, dn_ref, a_ref, b_ref, pre_ref):
    j = pl.program_id(1)

    @pl.when(j == 0)
    def _():
        pre_ref[...] = jnp.zeros(pre_ref.shape, F32)

    x = x_ref[...]
    a = a_ref[...]
    b = b_ref[...]
    pre = pre_ref[...]
    for q in range(DOWN_PAIRS):
        p = lax.dot_general(x, dn_ref[q * PAIR:(q + 1) * PAIR, :], NT_DIMS,
                            preferred_element_type=F32)
        for half in range(2):
            g = jnp.take_along_axis(p[:, half * LANES:(half + 1) * LANES], b, axis=1)
            pre = jnp.where(a == 2 * (DOWN_PAIRS * j + q) + half, g, pre)
    pre_ref[...] = pre


def _peer_down(hn2, down16, aidx, bidx):
    n, d = hn2.shape
    t = min(DOWN_TOKENS, n)
    slot_spec = pl.BlockSpec((t, PEER_SLOTS), lambda i, j: (i, 0))
    return pl.pallas_call(
        _peer_down_kernel,
        grid=(n // t, down16.shape[0] // (DOWN_PAIRS * PAIR)),
        in_specs=[pl.BlockSpec((t, d), lambda i, j: (i, 0)),
                  pl.BlockSpec((DOWN_PAIRS * PAIR, d), lambda i, j: (j, 0)),
                  slot_spec, slot_spec],
        out_specs=slot_spec,
        out_shape=jax.ShapeDtypeStruct((n, PEER_SLOTS), F32),
        compiler_params=_cparams("arbitrary", "arbitrary"),
    )(hn2, down16, aidx, bidx)


U32 = jnp.uint32
HI_HALF = 0xFFFF0000


def _bf16_bits(x):
    return lax.bitcast_convert_type(x.astype(BF16).astype(F32), U32)


def _peer_up_kernel(pre_ref, g_ref, a_ref, b_ref, up_ref, h_ref, mod_ref, o_ref,
                    act_scr, dense_scr):
    j = pl.program_id(1)
    t = pre_ref.shape[0]
    half = t // 2
    nk = PEER_NKEYS

    @pl.when(j == 0)
    def _():
        pre = pre_ref[...]
        act_scr[...] = 0.5 * pre * (1.0 + lax.erf(pre * (1.0 / math.sqrt(2.0)))) * g_ref[...]
        o_ref[...] = jnp.zeros(o_ref.shape, F32)
        row = lax.broadcasted_iota(I32, (nk, PEER_SLOTS), 0)

        def scatter(a_row, b_row, c_row):
            xa = jnp.where(row == a_row, c_row, 0.0).astype(BF16)
            yb = jnp.where(row == b_row, 1.0, 0.0).astype(BF16)
            return lax.dot_general(xa, yb, NT_DIMS, preferred_element_type=F32)

        def body(i, carry):
            for grp in range(TOKEN_UNROLL // 8):
                base = pl.multiple_of(i * TOKEN_UNROLL + grp * 8, 8)
                lo = [r[pl.ds(base, 8), :] for r in (a_ref, b_ref, act_scr)]
                hi = [r[pl.ds(base + half, 8), :] for r in (a_ref, b_ref, act_scr)]
                for u in range(8):
                    d_lo = scatter(*[v[u:u + 1, :] for v in lo])
                    d_hi = scatter(*[v[u:u + 1, :] for v in hi])
                    dense_scr[pl.ds(pl.multiple_of((base + u) * DENSE_PITCH, 8), nk), :] = (
                        lax.shift_right_logical(_bf16_bits(d_lo), U32(16)) | _bf16_bits(d_hi))
            return carry

        lax.fori_loop(0, half // TOKEN_UNROLL, body, 0)

    words = [dense_scr[pl.ds(UP_KEYS * j + u, half, stride=DENSE_PITCH), :] for u in range(UP_KEYS)]
    lo = jnp.concatenate([lax.bitcast_convert_type(lax.shift_left(w, U32(16)), F32) for w in words], axis=1)
    hi = jnp.concatenate([lax.bitcast_convert_type(w & U32(HI_HALF), F32) for w in words], axis=1)
    lhs = jnp.concatenate([lo, hi], axis=0).astype(BF16)
    o_ref[...] += jnp.dot(lhs, up_ref[...], preferred_element_type=F32)

    @pl.when(j == pl.num_programs(1) - 1)
    def _():
        o_ref[...] = h_ref[...] + mod_ref[0][5:6, :] * o_ref[...]


def _peer_up(pre, gate, aidx, bidx, up16, h1, mod3, seq):
    n, d = h1.shape
    t = min(UP_TOKENS, seq)
    slot_spec = pl.BlockSpec((t, PEER_SLOTS), lambda i, j: (i, 0))
    return pl.pallas_call(
        _peer_up_kernel,
        grid=(n // t, up16.shape[0] // (UP_KEYS * PEER_NKEYS)),
        in_specs=[slot_spec, slot_spec, slot_spec, slot_spec,
                  pl.BlockSpec((UP_KEYS * PEER_NKEYS, d), lambda i, j: (j, 0)),
                  pl.BlockSpec((t, d), lambda i, j: (i, 0)),
                  pl.BlockSpec((1, 6, d), lambda i, j: (i * t // seq, 0, 0))],
        out_specs=pl.BlockSpec((t, d), lambda i, j: (i, 0)),
        out_shape=jax.ShapeDtypeStruct((n, d), F32),
        scratch_shapes=[pltpu.VMEM((t, PEER_SLOTS), F32),
                        pltpu.VMEM((t // 2 * DENSE_PITCH, PEER_NKEYS), U32)],
        compiler_params=_cparams("arbitrary", "arbitrary"),
    )(pre, gate, aidx, bidx, up16, h1, mod3)


def _pad_lanes(v):
    return jnp.pad(v.astype(F32), (0, LANES - v.shape[0])).reshape(1, LANES)


def _layer(h2, mod3, l, batch, seq, norm1_w, w_in, q_norm_w, k_norm_w, rel_bias, lambda_q1, lambda_k1,
           lambda_q2, lambda_k2, subln_w, conv_w, conv_b, dt_bias, a_log, d_skip, ssm_norm_w, w_out,
           norm2_w, peer_wq, peer_keys, expert_down, expert_up):
    d = h2.shape[1]
    lam_init = 0.8 - 0.6 * math.exp(-0.3 * l)
    w16 = w_in.astype(BF16)
    w_dt = jnp.pad(w16[:, MAIN_COLS:], ((0, 0), (0, LANES - SSM_HEADS)))
    qn = jnp.tile(q_norm_w.astype(F32) * (HEAD_DIM ** -0.5 * LOG2E), 2).reshape(1, LANES)
    kn = jnp.tile(k_norm_w.astype(F32), 2).reshape(1, LANES)
    proj, dt_raw = _in_proj(h2, norm1_w.reshape(1, d), mod3, w16, w_dt, qn, kn, seq)

    lamv = jnp.pad(jnp.stack([lambda_q1, lambda_k1, lambda_q2, lambda_k2]).astype(F32),
                   ((0, 4), (0, LANES - HEAD_DIM)))
    att = _attention(rel_bias.astype(F32).reshape(-1), proj, lamv, subln_w.reshape(1, LANES),
                     batch, seq, lam_init)
    ssm = _ssd(proj, dt_raw, conv_w, conv_b.reshape(1, -1), _pad_lanes(dt_bias), _pad_lanes(a_log),
               _pad_lanes(d_skip), ssm_norm_w.reshape(1, -1), batch, seq)
    h1, hn2 = _out_proj(h2, att, ssm, w_out.astype(BF16), mod3, norm2_w.reshape(1, d), seq)

    keys = peer_keys.astype(BF16).reshape(2 * PEER_HEADS, PEER_NKEYS, -1)
    aidx, bidx, gate = _route(hn2, peer_wq.astype(BF16), keys)
    pre = _peer_down(hn2, expert_down.astype(BF16), aidx, bidx)
    return _peer_up(pre, gate, aidx, bidx, expert_up.astype(BF16), h1, mod3, seq)


def kernel(x, c, ada_w, ada_b, norm1_w, w_in, q_norm_w, k_norm_w, rel_bias, lambda_q1, lambda_k1, lambda_q2, lambda_k2, subln_w, conv_w, conv_b, dt_bias, a_log, d_skip, ssm_norm_w, w_out, norm2_w, peer_wq, peer_keys, expert_down, expert_up):
    batch, seq, d = x.shape
    depth = ada_w.shape[0]
    assert seq % (2 * ATT_CHUNK) == 0 and seq % min(1024, seq) == 0, "unsupported sequence length"
    assert batch <= 8 and d == ATT_WIDTH + SSM_WIDTH, "unsupported batch / model width"
    h2 = x.reshape(batch * seq, d)
    c_pad = jnp.pad(c, ((0, 8 - batch), (0, 0)))
    for l in range(depth):
        mod = _ada(c_pad, ada_w[l], ada_b[l].reshape(1, -1))
        mod3 = mod[:batch].reshape(batch, 6, d)
        h2 = _layer(h2, mod3, l, batch, seq, norm1_w[l], w_in[l], q_norm_w[l], k_norm_w[l], rel_bias,
                    lambda_q1[l], lambda_k1[l], lambda_q2[l], lambda_k2[l], subln_w[l], conv_w[l],
                    conv_b[l], dt_bias[l], a_log[l], d_skip[l], ssm_norm_w[l], w_out[l], norm2_w[l],
                    peer_wq[l], peer_keys[l], expert_down[l], expert_up[l])
    return h2.reshape(batch, seq, d)
```

```python
import functools
import math

import jax
import jax.numpy as jnp
from jax import lax
from jax.experimental import pallas as pl
from jax.experimental.pallas import tpu as pltpu

F32 = jnp.float32
BF16 = jnp.bfloat16
I32 = jnp.int32

LANES = 128
VMEM_LIMIT = 56 * 1024 * 1024

NORM_EPS = 1e-6
HEAD_DIM = 64
ATT_HEADS = 8
ATT_WIDTH = 1024
SSM_WIDTH = 1024
SSM_HEADS = 16
SSM_GROUPS = 2
SSM_STATE = 128
SSM_CONV = 4
SSM_CHUNK = 128
SSM_BC = 2 * SSM_GROUPS * SSM_STATE
REL_BUCKETS = 32
REL_MAX_DIST = 128
PEER_HEADS = 8
PEER_NKEYS = 128
PEER_TOPK = 16
PEER_SLOTS = PEER_HEADS * PEER_TOPK
MAIN_COLS = 3 * ATT_WIDTH + SSM_WIDTH + SSM_WIDTH + SSM_BC
NEG = -1e30
LOG2E = math.log2(math.e)

NT_DIMS = (((1,), (1,)), ((), ()))


def _cparams(*sem):
    return pltpu.CompilerParams(dimension_semantics=sem, vmem_limit_bytes=VMEM_LIMIT)


def _sigmoid(x):
    return 1.0 / (1.0 + jnp.exp(-x))


def _ada_kernel(c_ref, w_ref, b_ref, o_ref):
    c = c_ref[...]
    sc = (c * _sigmoid(c)).astype(BF16)
    o_ref[...] = jnp.dot(sc, w_ref[...].astype(BF16), preferred_element_type=F32) + b_ref[...]


def _ada(c_pad, ada_w, ada_b):
    rows, d = c_pad.shape
    n = ada_w.shape[1]
    tn = 1536
    return pl.pallas_call(
        _ada_kernel,
        grid=(n // tn,),
        in_specs=[pl.BlockSpec((rows, d), lambda j: (0, 0)),
                  pl.BlockSpec((d, tn), lambda j: (0, j)),
                  pl.BlockSpec((1, tn), lambda j: (0, j))],
        out_specs=pl.BlockSpec((rows, tn), lambda j: (0, j)),
        out_shape=jax.ShapeDtypeStruct((rows, n), F32),
        compiler_params=_cparams("arbitrary"),
    )(c_pad, ada_w, ada_b)


def _group_rms(blk, w_row, lo):
    sq = blk * blk
    s_all = jnp.sum(sq, axis=1, keepdims=True)
    s_lo = jnp.sum(jnp.where(lo, sq, 0.0), axis=1, keepdims=True)
    s = jnp.where(lo, s_lo, s_all - s_lo)
    return blk * lax.rsqrt(s * (1.0 / HEAD_DIM) + NORM_EPS) * w_row


def _in_proj_kernel(x_ref, nw_ref, mod_ref, w_ref, wdt_ref, qn_ref, kn_ref,
                    o_ref, dt_ref, hn_scr, raw_scr, *, tn, ntiles):
    j = pl.program_id(1)
    qk_tiles = 2 * ATT_WIDTH // tn

    def finish(k):
        raw = raw_scr[k % 2]
        if k < qk_tiles:
            lo = lax.broadcasted_iota(I32, (1, LANES), 1) < HEAD_DIM
            w_row = qn_ref[...] if k < qk_tiles // 2 else kn_ref[...]
            for cb in range(tn // LANES):
                blk = raw[:, cb * LANES:(cb + 1) * LANES]
                o_ref[:, cb * LANES:(cb + 1) * LANES] = _group_rms(blk, w_row, lo).astype(BF16)
        else:
            o_ref[...] = raw.astype(BF16)

    for step in range(ntiles + 1):
        @pl.when(j == step)
        def _(step=step):
            if step == 0:
                mod = mod_ref[0]
                rows = x_ref.shape[0] // 2
                for r0 in (0, rows):
                    x = x_ref[r0:r0 + rows, :]
                    y = x * lax.rsqrt(jnp.mean(x * x, axis=1, keepdims=True) + NORM_EPS) * nw_ref[...]
                    hn = (y * (1.0 + mod[1:2, :]) + mod[0:1, :]).astype(BF16)
                    hn_scr[r0:r0 + rows, :] = hn
                    dt_ref[r0:r0 + rows, :] = jnp.dot(hn, wdt_ref[...], preferred_element_type=F32)
                    raw_scr[0, r0:r0 + rows, :] = jnp.dot(hn, w_ref[...], preferred_element_type=F32)
            elif step < ntiles:
                raw_scr[step % 2] = jnp.dot(hn_scr[...], w_ref[...], preferred_element_type=F32)
            if step > 0:
                finish(step - 1)


def _in_proj(x2, norm_w, mod3, w_main, w_dt, qn, kn, seq):
    n, d = x2.shape
    tm = min(1024, seq)
    tn = 512
    ntiles = MAIN_COLS // tn
    return pl.pallas_call(
        functools.partial(_in_proj_kernel, tn=tn, ntiles=ntiles),
        grid=(n // tm, ntiles + 1),
        in_specs=[pl.BlockSpec((tm, d), lambda i, j: (i, 0)),
                  pl.BlockSpec((1, d), lambda i, j: (0, 0)),
                  pl.BlockSpec((1, 6, d), lambda i, j: (i * tm // seq, 0, 0)),
                  pl.BlockSpec((d, tn), lambda i, j: (0, jnp.minimum(j, ntiles - 1))),
                  pl.BlockSpec((d, LANES), lambda i, j: (0, 0)),
                  pl.BlockSpec((1, LANES), lambda i, j: (0, 0)),
                  pl.BlockSpec((1, LANES), lambda i, j: (0, 0))],
        out_specs=[pl.BlockSpec((tm, tn), lambda i, j: (i, jnp.maximum(j - 1, 0))),
                   pl.BlockSpec((tm, LANES), lambda i, j: (i, 0))],
        out_shape=[jax.ShapeDtypeStruct((n, MAIN_COLS), BF16),
                   jax.ShapeDtypeStruct((n, LANES), F32)],
        scratch_shapes=[pltpu.VMEM((tm, d), BF16),
                        pltpu.VMEM((2, tm, tn), F32)],
        compiler_params=_cparams("arbitrary", "arbitrary"),
    )(x2, norm_w, mod3, w_main, w_dt, qn, kn)


ATT_GROUP = 2
ATT_UNROLL = 4
ATT_CHUNK = 256
VT_ROWS = LANES + 16


def _attn_kernel(relb_ref, q_ref, k_ref, vt_ref, lamv_ref, subw_ref, o_ref,
                 q2t_scr, acc_scr, bias_scr, s_scr, smax_scr, *, t, lam_init):
    hp = pl.program_id(1)
    qi = pl.program_id(2)
    tq = 2 * t
    nchunk = vt_ref.shape[0] // ATT_GROUP

    @pl.when(qi == 0)
    def _():
        kk = lax.broadcasted_iota(I32, (t, t), 0)
        qq = lax.broadcasted_iota(I32, (t, t), 1)
        max_exact = REL_BUCKETS // 2
        buckets = []
        for off in (0, t):
            nn = jnp.maximum(qq - kk + off, 0)
            nf = jnp.maximum(nn, 1).astype(F32)
            large = max_exact + (jnp.log(nf / max_exact) / math.log(REL_MAX_DIST / max_exact)
                                 * (REL_BUCKETS - max_exact)).astype(I32)
            buckets.append(jnp.where(nn < max_exact, nn, jnp.minimum(large, REL_BUCKETS - 1)))
        zeros = jnp.zeros((t, t), F32)
        masked = jnp.full((t, t), NEG, F32)
        for hh in range(ATT_GROUP):
            head = hp * ATT_GROUP + hh
            for m in range(2):
                far = relb_ref[(REL_BUCKETS - 1) * 2 * ATT_HEADS + head * 2 + m]
                diag, sub = zeros, zeros
                for b in range(REL_BUCKETS - 1):
                    delta = (relb_ref[b * 2 * ATT_HEADS + head * 2 + m] - far) * LOG2E
                    diag = jnp.where(buckets[0] == b, delta, diag)
                    sub = jnp.where(buckets[1] == b, delta, sub)
                diag = jnp.where(qq >= kk, diag, NEG)
                for tile, (first, last) in enumerate(((sub, zeros), (diag, sub), (masked, diag))):
                    bias_scr[hh, tile, :, m * tq:m * tq + t] = first
                    bias_scr[hh, tile, :, m * tq + t:(m + 1) * tq] = last

    d_lo = lax.broadcasted_iota(I32, (LANES, tq), 0) < HEAD_DIM
    for hh in range(ATT_GROUP):
        qt = q_ref[:, hh * LANES:(hh + 1) * LANES].astype(F32).T
        q2t_scr[hh, :, 0:tq] = jnp.where(d_lo, qt, 0.0).astype(BF16)
        q2t_scr[hh, :, tq:2 * tq] = jnp.where(d_lo, 0.0, qt).astype(BF16)
        acc_scr[hh] = jnp.zeros((VT_ROWS, 2 * tq), F32)

    def scores(c, hh):
        k_c = k_ref[pl.ds(pl.multiple_of(c * t, t), t), hh * LANES:(hh + 1) * LANES]
        return jnp.dot(k_c, q2t_scr[hh], preferred_element_type=F32)

    def issue(c, slot, tile):
        for hh in range(ATT_GROUP):
            s = scores(c, hh)
            if tile is not None:
                s = s + bias_scr[hh, tile]
            s_scr[slot, hh] = s
            row = slot * ATT_GROUP + hh
            smax_scr[row:row + 1, :] = jnp.max(s, axis=0, keepdims=True)

    ones_rows = jnp.where(lax.broadcasted_iota(I32, (VT_ROWS - LANES, t), 0) == 0, 1.0, 0.0).astype(BF16)

    def consume(c, carry, slot):
        out = []
        for hh in range(ATT_GROUP):
            m_prev = carry[hh]
            row = slot * ATT_GROUP + hh
            m_new = jnp.maximum(m_prev, smax_scr[row:row + 1, :])
            alpha = jnp.exp2(m_prev - m_new)
            p = jnp.exp2(s_scr[slot, hh] - m_new).astype(BF16)
            vt_c = jnp.concatenate([vt_ref[hh * nchunk + c], ones_rows], axis=0)
            pv = jnp.dot(vt_c, p, preferred_element_type=F32)
            acc_scr[hh] = alpha * acc_scr[hh] + pv
            out.append(m_new)
        return tuple(out)

    def run(first, tiles, more, reissue, carry):
        if reissue:
            issue(first, 0, tiles[0])
        for k in range(len(tiles)):
            if k + 1 < len(tiles):
                issue(first + k + 1, (k + 1) % 2, tiles[k + 1])
            elif more:
                issue(first + k + 1, (k + 1) % 2, None)
            carry = consume(first + k, carry, k % 2)
        return carry

    nfar = jnp.maximum(2 * qi - 1, 0)
    ntrip = nfar // ATT_UNROLL
    issue(0, 0, None)

    def trip(i, carry):
        return run(i * ATT_UNROLL, [None] * ATT_UNROLL, True, False, carry)

    init = tuple(jnp.full((1, 2 * tq), NEG, F32) for _ in range(ATT_GROUP))
    carry = lax.fori_loop(0, ntrip, trip, init)
    rest = ntrip * ATT_UNROLL
    tails = [functools.partial(run, rest, [None] * r + [0, 1, 2], False, False)
             for r in range(1, ATT_UNROLL, 2)]
    tails.append(functools.partial(run, rest, [1, 2], False, True))
    carry = lax.switch(jnp.where(qi == 0, len(tails) - 1, (nfar - rest) // 2), tails, carry)

    lv = lamv_ref[...]
    lam = (jnp.exp(jnp.sum(lv[0:1, :] * lv[1:2, :], axis=1, keepdims=True))
           - jnp.exp(jnp.sum(lv[2:3, :] * lv[3:4, :], axis=1, keepdims=True)) + lam_init)
    for hh in range(ATT_GROUP):
        acc = acc_scr[hh, 0:LANES, :] * (1.0 / acc_scr[hh, LANES:LANES + 1, :])
        o = (acc[:, 0:tq] - lam * acc[:, tq:2 * tq]).T
        o = o * lax.rsqrt(jnp.mean(o * o, axis=1, keepdims=True) + NORM_EPS) * subw_ref[...]
        o_ref[:, hh * LANES:(hh + 1) * LANES] = (o * (1.0 - lam_init)).astype(BF16)


def _attention(relb, proj, lamv, subw, batch, seq, lam_init):
    n = batch * seq
    t = ATT_CHUNK
    tq = 2 * t
    nq = seq // tq
    nk = seq // t
    gw = ATT_GROUP * LANES
    ngroups = ATT_HEADS // ATT_GROUP
    vt = proj[:, 2 * ATT_WIDTH:3 * ATT_WIDTH].reshape(batch, nk, t, ATT_HEADS, LANES)
    vt = vt.transpose(0, 3, 1, 4, 2).reshape(batch * ATT_HEADS * nk, LANES, t)
    return pl.pallas_call(
        functools.partial(_attn_kernel, t=t, lam_init=lam_init),
        grid=(batch, ngroups, nq),
        in_specs=[pl.BlockSpec(memory_space=pltpu.SMEM),
                  pl.BlockSpec((tq, gw), lambda b, g, i: (b * nq + i, g)),
                  pl.BlockSpec((seq, gw), lambda b, g, i: (b, ngroups + g)),
                  pl.BlockSpec((ATT_GROUP * nk, LANES, t), lambda b, g, i: (b * ngroups + g, 0, 0)),
                  pl.BlockSpec((8, LANES), lambda b, g, i: (0, 0)),
                  pl.BlockSpec((1, LANES), lambda b, g, i: (0, 0))],
        out_specs=pl.BlockSpec((tq, gw), lambda b, g, i: (b * nq + i, g)),
        out_shape=jax.ShapeDtypeStruct((n, ATT_WIDTH), BF16),
        scratch_shapes=[pltpu.VMEM((ATT_GROUP, LANES, 2 * tq), BF16),
                        pltpu.VMEM((ATT_GROUP, VT_ROWS, 2 * tq), F32),
                        pltpu.VMEM((ATT_GROUP, 3, t, 2 * tq), F32),
                        pltpu.VMEM((2, ATT_GROUP, t, 2 * tq), F32),
                        pltpu.VMEM((8, 2 * tq), F32)],
        compiler_params=_cparams("arbitrary", "arbitrary", "arbitrary"),
    )(relb, proj, proj, vt, lamv, subw)


def _split3(x):
    hi = x.astype(BF16)
    r1 = x - hi.astype(F32)
    mid = r1.astype(BF16)
    lo = (r1 - mid.astype(F32)).astype(BF16)
    return hi, mid, lo


def _ssd_kernel(z_ref, xs_ref, bc_ref, dt_ref, cw_ref, cb_ref, dtb_ref, alog_ref, dskip_ref, nw_ref,
                o_ref, xpad_scr, state_scr):
    L = SSM_CHUNK
    W = SSM_WIDTH
    P2 = LANES
    nblk = W // P2
    gw = W // SSM_GROUPS

    @pl.when(pl.program_id(0) == 0)
    def _():
        xpad_scr[0:8, :] = jnp.zeros((8, W + SSM_BC), F32)
        state_scr[...] = jnp.zeros((SSM_STATE, W), F32)

    xpad_scr[8:8 + L, 0:W] = xs_ref[...].astype(F32)
    xpad_scr[8:8 + L, W:W + SSM_BC] = bc_ref[...].astype(F32)
    conv = cb_ref[...] + cw_ref[0:1, :] * xpad_scr[5:5 + L, :]
    for kk in range(1, SSM_CONV):
        conv = conv + cw_ref[kk:kk + 1, :] * xpad_scr[5 + kk:5 + kk + L, :]
    xpad_scr[0:8, :] = xpad_scr[L:L + 8, :]
    u = conv * _sigmoid(conv)

    dtr = dt_ref[...] + dtb_ref[...]
    dt = jnp.maximum(dtr, 0.0) + jnp.log1p(jnp.exp(-jnp.abs(dtr)))
    a = -jnp.exp(alog_ref[...])
    da = dt * a

    ri = lax.broadcasted_iota(I32, (L, L), 0)
    ci = lax.broadcasted_iota(I32, (L, L), 1)
    tril = ri >= ci
    tri = jnp.where(tril, 1.0, 0.0).astype(BF16)
    hi, mid, lo3 = _split3(da)
    a_cs = (jnp.dot(tri, hi, preferred_element_type=F32) + jnp.dot(tri, mid, preferred_element_type=F32)
            + jnp.dot(tri, lo3, preferred_element_type=F32))
    a_cs_t = a_cs.T
    a_last = a_cs[L - 1:L, :]
    e_cs = jnp.exp(a_cs)
    dt_ds = dt * jnp.exp(a_last - a_cs)

    lane_lo = lax.broadcasted_iota(I32, (1, P2), 1) < HEAD_DIM

    def expand(mat, i):
        return jnp.where(lane_lo, mat[:, 2 * i:2 * i + 1], mat[:, 2 * i + 1:2 * i + 2])

    y_blocks = []
    for g in range(SSM_GROUPS):
        bm = u[:, W + g * SSM_STATE:W + (g + 1) * SSM_STATE]
        cm = u[:, W + (SSM_GROUPS + g) * SSM_STATE:W + (SSM_GROUPS + g + 1) * SSM_STATE]
        bm16 = bm.astype(BF16)
        cm16 = cm.astype(BF16)
        cb = lax.dot_general(cm16, bm16, NT_DIMS, preferred_element_type=F32)
        st_g = state_scr[:, g * gw:(g + 1) * gw]
        y_off = jnp.dot(cm16, st_g.astype(BF16), preferred_element_type=F32)
        xd_blocks = []
        for ib in range(nblk // SSM_GROUPS):
            i = g * (nblk // SSM_GROUPS) + ib
            xs_blk = u[:, i * P2:(i + 1) * P2]
            xc = xs_blk * expand(dt, i)
            yd = jnp.zeros((L, P2), F32)
            for hh in range(2):
                head = 2 * i + hh
                seg = a_cs[:, head:head + 1] - a_cs_t[head:head + 1, :]
                wmat = (cb * jnp.where(tril, jnp.exp(seg), 0.0)).astype(BF16)
                keep = lane_lo if hh == 0 else jnp.logical_not(lane_lo)
                yd = yd + jnp.dot(wmat, jnp.where(keep, xc, 0.0).astype(BF16), preferred_element_type=F32)
            y = yd + y_off[:, ib * P2:(ib + 1) * P2] * expand(e_cs, i) + expand(dskip_ref[...], i) * xs_blk
            zf = z_ref[:, i * P2:(i + 1) * P2].astype(F32)
            y_blocks.append(y * (zf * _sigmoid(zf)))
            xd_blocks.append((xs_blk * expand(dt_ds, i)).astype(BF16))
        xd = jnp.concatenate(xd_blocks, axis=1)
        st_new = jnp.dot(bm.T.astype(BF16), xd, preferred_element_type=F32)
        decay = jnp.concatenate([expand(jnp.exp(a_last), g * (nblk // SSM_GROUPS) + ib)
                                 for ib in range(nblk // SSM_GROUPS)], axis=1)
        state_scr[:, g * gw:(g + 1) * gw] = st_g * decay + st_new

    per_g = nblk // SSM_GROUPS
    for g in range(SSM_GROUPS):
        blks = y_blocks[g * per_g:(g + 1) * per_g]
        ss = sum(jnp.sum(b * b, axis=1, keepdims=True) for b in blks) * (1.0 / gw)
        inv = lax.rsqrt(ss + NORM_EPS)
        for ib, b in enumerate(blks):
            i = g * per_g + ib
            o_ref[:, i * P2:(i + 1) * P2] = (b * inv * nw_ref[:, i * P2:(i + 1) * P2]).astype(BF16)


def _ssd_batched_kernel(z_ref, xs_ref, bc_ref, dt_ref, cw_ref, cb_ref, dtb_ref, alog_ref, dskip_ref, nw_ref,
                        o_ref, xpad_scr, state_scr):
    for b in range(z_ref.shape[0]):
        _ssd_kernel(z_ref.at[b], xs_ref.at[b], bc_ref.at[b], dt_ref.at[b], cw_ref, cb_ref, dtb_ref,
                    alog_ref, dskip_ref, nw_ref, o_ref.at[b], xpad_scr.at[b], state_scr.at[b])


def _ssd(proj, dt_raw, conv_w, conv_b, dt_bias, a_log, d_skip, norm_w, batch, seq):
    L = SSM_CHUNK
    cd = SSM_WIDTH + SSM_BC
    proj3 = proj.reshape(batch, seq, MAIN_COLS)
    dt3 = dt_raw.reshape(batch, seq, LANES)
    out = pl.pallas_call(
        _ssd_batched_kernel,
        grid=(seq // L,),
        in_specs=[pl.BlockSpec((batch, L, SSM_WIDTH), lambda c: (0, c, 3)),
                  pl.BlockSpec((batch, L, SSM_WIDTH), lambda c: (0, c, 4)),
                  pl.BlockSpec((batch, L, SSM_BC), lambda c: (0, c, 10)),
                  pl.BlockSpec((batch, L, LANES), lambda c: (0, c, 0)),
                  pl.BlockSpec((SSM_CONV, cd), lambda c: (0, 0)),
                  pl.BlockSpec((1, cd), lambda c: (0, 0)),
                  pl.BlockSpec((1, LANES), lambda c: (0, 0)),
                  pl.BlockSpec((1, LANES), lambda c: (0, 0)),
                  pl.BlockSpec((1, LANES), lambda c: (0, 0)),
                  pl.BlockSpec((1, SSM_WIDTH), lambda c: (0, 0))],
        out_specs=pl.BlockSpec((batch, L, SSM_WIDTH), lambda c: (0, c, 0)),
        out_shape=jax.ShapeDtypeStruct((batch, seq, SSM_WIDTH), BF16),
        scratch_shapes=[pltpu.VMEM((batch, L + 8, cd), F32),
                        pltpu.VMEM((batch, SSM_STATE, SSM_WIDTH), F32)],
        compiler_params=_cparams("arbitrary"),
    )(proj3, proj3, proj3, dt3, conv_w, conv_b, dt_bias, a_log, d_skip, norm_w)
    return out.reshape(batch * seq, SSM_WIDTH)


def _out_proj_kernel(x_ref, att_ref, ssm_ref, w_ref, mod_ref, nw_ref, h_ref, hn_ref):
    mix = (jnp.dot(att_ref[...], w_ref[0:ATT_WIDTH, :], preferred_element_type=F32)
           + jnp.dot(ssm_ref[...], w_ref[ATT_WIDTH:, :], preferred_element_type=F32))
    mod = mod_ref[0]
    h1 = x_ref[...] + mod[2:3, :] * mix
    h_ref[...] = h1
    y = h1 * lax.rsqrt(jnp.mean(h1 * h1, axis=1, keepdims=True) + NORM_EPS) * nw_ref[...]
    hn_ref[...] = (y * (1.0 + mod[4:5, :]) + mod[3:4, :]).astype(BF16)


def _out_proj(x2, att, ssm, w_out, mod3, norm2_w, seq):
    n, d = x2.shape
    tm = min(256, seq)
    return pl.pallas_call(
        _out_proj_kernel,
        grid=(n // tm,),
        in_specs=[pl.BlockSpec((tm, d), lambda i: (i, 0)),
                  pl.BlockSpec((tm, ATT_WIDTH), lambda i: (i, 0)),
                  pl.BlockSpec((tm, SSM_WIDTH), lambda i: (i, 0)),
                  pl.BlockSpec((ATT_WIDTH + SSM_WIDTH, d), lambda i: (0, 0)),
                  pl.BlockSpec((1, 6, d), lambda i: (i * tm // seq, 0, 0)),
                  pl.BlockSpec((1, d), lambda i: (0, 0))],
        out_specs=[pl.BlockSpec((tm, d), lambda i: (i, 0)),
                   pl.BlockSpec((tm, d), lambda i: (i, 0))],
        out_shape=[jax.ShapeDtypeStruct((n, d), F32),
                   jax.ShapeDtypeStruct((n, d), BF16)],
        compiler_params=_cparams("arbitrary"),
    )(x2, att, ssm, w_out, mod3, norm2_w)


def _topk_rows(s, k, rows):
    row = lax.broadcasted_iota(I32, s.shape, 0).astype(F32)
    vals, idxs = [], []
    for _ in range(k):
        m, idx = _argmax_rows(s, rows)
        s = jnp.where(row == idx, -jnp.inf, s)
        vals.append(m)
        idxs.append(idx)
    return vals, idxs


def _argmax_rows(s, rows, payload=None):
    groups = range(rows // 8)
    vs = [s[8 * i:8 * i + 8] for i in groups]
    sub = lax.broadcasted_iota(I32, vs[0].shape, 0).astype(F32)
    rs = [sub + float(8 * i) for i in groups]
    ps = [payload[8 * i:8 * i + 8] for i in groups] if payload is not None else None
    while len(vs) > 1:
        nv, nr, npay = [], [], []
        for i in range(0, len(vs) - 1, 2):
            take_b = vs[i + 1] > vs[i]
            nv.append(jnp.where(take_b, vs[i + 1], vs[i]))
            nr.append(jnp.where(take_b, rs[i + 1], rs[i]))
            if ps is not None:
                npay.append(jnp.where(take_b, ps[i + 1], ps[i]))
        if len(vs) % 2:
            nv.append(vs[-1])
            nr.append(rs[-1])
            if ps is not None:
                npay.append(ps[-1])
        vs, rs, ps = nv, nr, (npay if ps is not None else None)
    m = jnp.max(vs[0], axis=0, keepdims=True)
    idx = jnp.min(jnp.where(vs[0] == m, rs[0], float(rows)), axis=0, keepdims=True)
    if ps is None:
        return m, idx
    return m, idx, jnp.max(jnp.where(rs[0] == idx, ps[0], -1.0), axis=0, keepdims=True)


_CAND = [(i, j) for i in range(PEER_TOPK) for j in range(PEER_TOPK) if (i + 1) * (j + 1) <= PEER_TOPK]


def _route_kernel(hn_ref, wq_ref, keys_ref, a_ref, b_ref, g_ref, top_scr, code_scr):
    t = hn_ref.shape[0]
    qp = jnp.dot(hn_ref[...], wq_ref[...], preferred_element_type=F32).astype(BF16)
    ncand = len(_CAND)
    pad = (-ncand) % 8
    for h in range(PEER_HEADS):
        sub = []
        for c in range(2):
            hc = 2 * h + c
            sc = lax.dot_general(keys_ref[hc], qp[:, hc * LANES:(hc + 1) * LANES], NT_DIMS,
                                 preferred_element_type=F32)
            sub.append(_topk_rows(sc, PEER_TOPK, PEER_NKEYS))
        (s1, i1), (s2, i2) = sub
        cand = jnp.concatenate([s1[i] + s2[j] for i, j in _CAND]
                               + [jnp.full((pad, t), -jnp.inf, F32)], axis=0)
        a_hi = [v * float(PEER_NKEYS) for v in i1]
        code = jnp.concatenate([a_hi[i] + i2[j] for i, j in _CAND] + [jnp.zeros((pad, t), F32)], axis=0)
        row = lax.broadcasted_iota(I32, cand.shape, 0).astype(F32)
        for kk in range(PEER_TOPK):
            m, idx, picked = _argmax_rows(cand, ncand + pad, code)
            slot = h * PEER_TOPK + kk
            top_scr[slot:slot + 1, :] = m
            code_scr[slot:slot + 1, :] = picked
            cand = jnp.where(row == idx, -jnp.inf, cand)
        top = top_scr[h * PEER_TOPK:(h + 1) * PEER_TOPK, :]
        e = jnp.exp(top - jnp.max(top, axis=0, keepdims=True))
        top_scr[h * PEER_TOPK:(h + 1) * PEER_TOPK, :] = e / jnp.sum(e, axis=0, keepdims=True)
    code_t = code_scr[...].T
    first = jnp.floor(code_t * (1.0 / PEER_NKEYS))
    a_ref[...] = first.astype(I32)
    b_ref[...] = (code_t - first * float(PEER_NKEYS)).astype(I32)
    g_ref[...] = top_scr[...].T


def _route(hn2, wq, keys):
    n, d = hn2.shape
    t = min(256, n)
    qd = wq.shape[1]
    out = jax.ShapeDtypeStruct((n, PEER_SLOTS), I32)
    return pl.pallas_call(
        _route_kernel,
        grid=(n // t,),
        in_specs=[pl.BlockSpec((t, d), lambda i: (i, 0)),
                  pl.BlockSpec((d, qd), lambda i: (0, 0)),
                  pl.BlockSpec(keys.shape, lambda i: (0, 0, 0))],
        out_specs=[pl.BlockSpec((t, PEER_SLOTS), lambda i: (i, 0))] * 3,
        out_shape=[out, out, jax.ShapeDtypeStruct((n, PEER_SLOTS), F32)],
        scratch_shapes=[pltpu.VMEM((PEER_SLOTS, t), F32),
                        pltpu.VMEM((PEER_SLOTS, t), F32)],
        compiler_params=_cparams("arbitrary"),
    )(hn2, wq, keys)


PAIR = 2 * PEER_NKEYS
DOWN_PAIRS = 16
DOWN_TOKENS = 512
UP_KEYS = 16
UP_TOKENS = 512
TOKEN_UNROLL = 32
DENSE_PITCH = PEER_NKEYS + 8


def _peer_down_kernel(x_ref, dn_ref, a_ref, b_ref, pre_ref):
    j = pl.program_id(1)

    @pl.when(j == 0)
    def _():
        pre_ref[...] = jnp.zeros(pre_ref.shape, F32)

    x = x_ref[...]
    a = a_ref[...]
    b = b_ref[...]
    pre = pre_ref[...]
    for q in range(DOWN_PAIRS):
        p = lax.dot_general(x, dn_ref[q * PAIR:(q + 1) * PAIR, :], NT_DIMS,
                            preferred_element_type=F32)
        for half in range(2):
            g = jnp.take_along_axis(p[:, half * LANES:(half + 1) * LANES], b, axis=1)
            pre = jnp.where(a == 2 * (DOWN_PAIRS * j + q) + half, g, pre)
    pre_ref[...] = pre


def _peer_down(hn2, down16, aidx, bidx):
    n, d = hn2.shape
    t = min(DOWN_TOKENS, n)
    slot_spec = pl.BlockSpec((t, PEER_SLOTS), lambda i, j: (i, 0))
    return pl.pallas_call(
        _peer_down_kernel,
        grid=(n // t, down16.shape[0] // (DOWN_PAIRS * PAIR)),
        in_specs=[pl.BlockSpec((t, d), lambda i, j: (i, 0)),
                  pl.BlockSpec((DOWN_PAIRS * PAIR, d), lambda i, j: (j, 0)),
                  slot_spec, slot_spec],
        out_specs=slot_spec,
        out_shape=jax.ShapeDtypeStruct((n, PEER_SLOTS), F32),
        compiler_params=_cparams("arbitrary", "arbitrary"),
    )(hn2, down16, aidx, bidx)


U32 = jnp.uint32
HI_HALF = 0xFFFF0000


def _bf16_bits(x):
    return lax.bitcast_convert_type(x.astype(BF16).astype(F32), U32)


def _peer_up_kernel(pre_ref, g_ref, a_ref, b_ref, up_ref, h_ref, mod_ref, o_ref,
                    act_scr, dense_scr):
    j = pl.program_id(1)
    t = pre_ref.shape[0]
    half = t // 2
    nk = PEER_NKEYS

    @pl.when(j == 0)
    def _():
        pre = pre_ref[...]
        act_scr[...] = 0.5 * pre * (1.0 + lax.erf(pre * (1.0 / math.sqrt(2.0)))) * g_ref[...]
        o_ref[...] = jnp.zeros(o_ref.shape, F32)
        row = lax.broadcasted_iota(I32, (nk, PEER_SLOTS), 0)

        def scatter(a_row, b_row, c_row):
            xa = jnp.where(row == a_row, c_row, 0.0).astype(BF16)
            yb = jnp.where(row == b_row, 1.0, 0.0).astype(BF16)
            return lax.dot_general(xa, yb, NT_DIMS, preferred_element_type=F32)

        def body(i, carry):
            for grp in range(TOKEN_UNROLL // 8):
                base = pl.multiple_of(i * TOKEN_UNROLL + grp * 8, 8)
                lo = [r[pl.ds(base, 8), :] for r in (a_ref, b_ref, act_scr)]
                hi = [r[pl.ds(base + half, 8), :] for r in (a_ref, b_ref, act_scr)]
                for u in range(8):
                    d_lo = scatter(*[v[u:u + 1, :] for v in lo])
                    d_hi = scatter(*[v[u:u + 1, :] for v in hi])
                    dense_scr[pl.ds(pl.multiple_of((base + u) * DENSE_PITCH, 8), nk), :] = (
                        lax.shift_right_logical(_bf16_bits(d_lo), U32(16)) | _bf16_bits(d_hi))
            return carry

        lax.fori_loop(0, half // TOKEN_UNROLL, body, 0)

    words = [dense_scr[pl.ds(UP_KEYS * j + u, half, stride=DENSE_PITCH), :] for u in range(UP_KEYS)]
    lo = jnp.concatenate([lax.bitcast_convert_type(lax.shift_left(w, U32(16)), F32) for w in words], axis=1)
    hi = jnp.concatenate([lax.bitcast_convert_type(w & U32(HI_HALF), F32) for w in words], axis=1)
    lhs = jnp.concatenate([lo, hi], axis=0).astype(BF16)
    o_ref[...] += jnp.dot(lhs, up_ref[...], preferred_element_type=F32)

    @pl.when(j == pl.num_programs(1) - 1)
    def _():
        o_ref[...] = h_ref[...] + mod_ref[0][5:6, :] * o_ref[...]


def _peer_up(pre, gate, aidx, bidx, up16, h1, mod3, seq):
    n, d = h1.shape
    t = min(UP_TOKENS, seq)
    slot_spec = pl.BlockSpec((t, PEER_SLOTS), lambda i, j: (i, 0))
    return pl.pallas_call(
        _peer_up_kernel,
        grid=(n // t, up16.shape[0] // (UP_KEYS * PEER_NKEYS)),
        in_specs=[slot_spec, slot_spec, slot_spec, slot_spec,
                  pl.BlockSpec((UP_KEYS * PEER_NKEYS, d), lambda i, j: (j, 0)),
                  pl.BlockSpec((t, d), lambda i, j: (i, 0)),
                  pl.BlockSpec((1, 6, d), lambda i, j: (i * t // seq, 0, 0))],
        out_specs=pl.BlockSpec((t, d), lambda i, j: (i, 0)),
        out_shape=jax.ShapeDtypeStruct((n, d), F32),
        scratch_shapes=[pltpu.VMEM((t, PEER_SLOTS), F32),
                        pltpu.VMEM((t // 2 * DENSE_PITCH, PEER_NKEYS), U32)],
        compiler_params=_cparams("arbitrary", "arbitrary"),
    )(pre, gate, aidx, bidx, up16, h1, mod3)


def _pad_lanes(v):
    return jnp.pad(v.astype(F32), (0, LANES - v.shape[0])).reshape(1, LANES)


def _layer(h2, mod3, l, batch, seq, norm1_w, w_in, q_norm_w, k_norm_w, rel_bias, lambda_q1, lambda_k1,
           lambda_q2, lambda_k2, subln_w, conv_w, conv_b, dt_bias, a_log, d_skip, ssm_norm_w, w_out,
           norm2_w, peer_wq, peer_keys, expert_down, expert_up):
    d = h2.shape[1]
    lam_init = 0.8 - 0.6 * math.exp(-0.3 * l)
    w16 = w_in.astype(BF16)
    w_dt = jnp.pad(w16[:, MAIN_COLS:], ((0, 0), (0, LANES - SSM_HEADS)))
    qn = jnp.tile(q_norm_w.astype(F32) * (HEAD_DIM ** -0.5 * LOG2E), 2).reshape(1, LANES)
    kn = jnp.tile(k_norm_w.astype(F32), 2).reshape(1, LANES)
    proj, dt_raw = _in_proj(h2, norm1_w.reshape(1, d), mod3, w16, w_dt, qn, kn, seq)

    lamv = jnp.pad(jnp.stack([lambda_q1, lambda_k1, lambda_q2, lambda_k2]).astype(F32),
                   ((0, 4), (0, LANES - HEAD_DIM)))
    att = _attention(rel_bias.astype(F32).reshape(-1), proj, lamv, subln_w.reshape(1, LANES),
                     batch, seq, lam_init)
    ssm = _ssd(proj, dt_raw, conv_w, conv_b.reshape(1, -1), _pad_lanes(dt_bias), _pad_lanes(a_log),
               _pad_lanes(d_skip), ssm_norm_w.reshape(1, -1), batch, seq)
    h1, hn2 = _out_proj(h2, att, ssm, w_out.astype(BF16), mod3, norm2_w.reshape(1, d), seq)

    keys = peer_keys.astype(BF16).reshape(2 * PEER_HEADS, PEER_NKEYS, -1)
    aidx, bidx, gate = _route(hn2, peer_wq.astype(BF16), keys)
    pre = _peer_down(hn2, expert_down.astype(BF16), aidx, bidx)
    return _peer_up(pre, gate, aidx, bidx, expert_up.astype(BF16), h1, mod3, seq)


def kernel(x, c, ada_w, ada_b, norm1_w, w_in, q_norm_w, k_norm_w, rel_bias, lambda_q1, lambda_k1, lambda_q2, lambda_k2, subln_w, conv_w, conv_b, dt_bias, a_log, d_skip, ssm_norm_w, w_out, norm2_w, peer_wq, peer_keys, expert_down, expert_up):
    batch, seq, d = x.shape
    depth = ada_w.shape[0]
    assert seq % (2 * ATT_CHUNK) == 0 and seq % min(1024, seq) == 0, "unsupported sequence length"
    assert batch <= 8 and d == ATT_WIDTH + SSM_WIDTH, "unsupported batch / model width"
    h2 = x.reshape(batch * seq, d)
    c_pad = jnp.pad(c, ((0, 8 - batch), (0, 0)))
    for l in range(depth):
        mod = _ada(c_pad, ada_w[l], ada_b[l].reshape(1, -1))
        mod3 = mod[:batch].reshape(batch, 6, d)
        h2 = _layer(h2, mod3, l, batch, seq, norm1_w[l], w_in[l], q_norm_w[l], k_norm_w[l], rel_bias,
                    lambda_q1[l], lambda_k1[l], lambda_q2[l], lambda_k2[l], subln_w[l], conv_w[l],
                    conv_b[l], dt_bias[l], a_log[l], d_skip[l], ssm_norm_w[l], w_out[l], norm2_w[l],
                    peer_wq[l], peer_keys[l], expert_down[l], expert_up[l])
    return h2.reshape(batch, seq, d)
```

```python
import functools
import math

import jax
import jax.numpy as jnp
from jax import lax
from jax.experimental import pallas as pl
from jax.experimental.pallas import tpu as pltpu

F32 = jnp.float32
BF16 = jnp.bfloat16
I32 = jnp.int32

LANES = 128
VMEM_LIMIT = 56 * 1024 * 1024

NORM_EPS = 1e-6
HEAD_DIM = 64
ATT_HEADS = 8
ATT_WIDTH = 1024
SSM_WIDTH = 1024
SSM_HEADS = 16
SSM_GROUPS = 2
SSM_STATE = 128
SSM_CONV = 4
SSM_CHUNK = 128
SSM_BC = 2 * SSM_GROUPS * SSM_STATE
REL_BUCKETS = 32
REL_MAX_DIST = 128
PEER_HEADS = 8
PEER_NKEYS = 128
PEER_TOPK = 16
PEER_SLOTS = PEER_HEADS * PEER_TOPK
MAIN_COLS = 3 * ATT_WIDTH + SSM_WIDTH + SSM_WIDTH + SSM_BC
NEG = -1e30
LOG2E = math.log2(math.e)

NT_DIMS = (((1,), (1,)), ((), ()))


def _cparams(*sem):
    return pltpu.CompilerParams(dimension_semantics=sem, vmem_limit_bytes=VMEM_LIMIT)


def _sigmoid(x):
    return 1.0 / (1.0 + jnp.exp(-x))


def _ada_kernel(c_ref, w_ref, b_ref, o_ref):
    c = c_ref[...]
    sc = (c * _sigmoid(c)).astype(BF16)
    o_ref[...] = jnp.dot(sc, w_ref[...].astype(BF16), preferred_element_type=F32) + b_ref[...]


def _ada(c_pad, ada_w, ada_b):
    rows, d = c_pad.shape
    n = ada_w.shape[1]
    tn = 1536
    return pl.pallas_call(
        _ada_kernel,
        grid=(n // tn,),
        in_specs=[pl.BlockSpec((rows, d), lambda j: (0, 0)),
                  pl.BlockSpec((d, tn), lambda j: (0, j)),
                  pl.BlockSpec((1, tn), lambda j: (0, j))],
        out_specs=pl.BlockSpec((rows, tn), lambda j: (0, j)),
        out_shape=jax.ShapeDtypeStruct((rows, n), F32),
        compiler_params=_cparams("arbitrary"),
    )(c_pad, ada_w, ada_b)


def _group_rms(blk, w_row, lo):
    sq = blk * blk
    s_all = jnp.sum(sq, axis=1, keepdims=True)
    s_lo = jnp.sum(jnp.where(lo, sq, 0.0), axis=1, keepdims=True)
    s = jnp.where(lo, s_lo, s_all - s_lo)
    return blk * lax.rsqrt(s * (1.0 / HEAD_DIM) + NORM_EPS) * w_row


def _in_proj_kernel(x_ref, nw_ref, mod_ref, w_ref, wdt_ref, qn_ref, kn_ref,
                    o_ref, dt_ref, hn_scr, raw_scr, *, tn, ntiles):
    j = pl.program_id(1)

    def finish(k):
        lo = lax.broadcasted_iota(I32, (1, LANES), 1) < HEAD_DIM
        for cb in range(tn // LANES):
            col = k * tn + cb * LANES
            blk = raw_scr[k % 2, :, cb * LANES:(cb + 1) * LANES]
            if col < 2 * ATT_WIDTH:
                w_row = qn_ref[...] if col < ATT_WIDTH else kn_ref[...]
                blk = _group_rms(blk, w_row, lo)
            o_ref[:, cb * LANES:(cb + 1) * LANES] = blk.astype(BF16)

    for step in range(ntiles + 1):
        @pl.when(j == step)
        def _(step=step):
            if step == 0:
                mod = mod_ref[0]
                rows = x_ref.shape[0] // 2
                for r0 in (0, rows):
                    x = x_ref[r0:r0 + rows, :]
                    y = x * lax.rsqrt(jnp.mean(x * x, axis=1, keepdims=True) + NORM_EPS) * nw_ref[...]
                    hn = (y * (1.0 + mod[1:2, :]) + mod[0:1, :]).astype(BF16)
                    hn_scr[r0:r0 + rows, :] = hn
                    dt_ref[r0:r0 + rows, :] = jnp.dot(hn, wdt_ref[...], preferred_element_type=F32)
                    raw_scr[0, r0:r0 + rows, :] = jnp.dot(hn, w_ref[...], preferred_element_type=F32)
            elif step < ntiles:
                raw_scr[step % 2] = jnp.dot(hn_scr[...], w_ref[...], preferred_element_type=F32)
            if step > 0:
                finish(step - 1)


def _in_proj(x2, norm_w, mod3, w_main, w_dt, qn, kn, seq):
    n, d = x2.shape
    tm = min(512, seq)
    tn = MAIN_COLS // 2
    ntiles = MAIN_COLS // tn
    return pl.pallas_call(
        functools.partial(_in_proj_kernel, tn=tn, ntiles=ntiles),
        grid=(n // tm, ntiles + 1),
        in_specs=[pl.BlockSpec((tm, d), lambda i, j: (i, 0)),
                  pl.BlockSpec((1, d), lambda i, j: (0, 0)),
                  pl.BlockSpec((1, 6, d), lambda i, j: (i * tm // seq, 0, 0)),
                  pl.BlockSpec((d, tn), lambda i, j: (0, jnp.minimum(j, ntiles - 1))),
                  pl.BlockSpec((d, LANES), lambda i, j: (0, 0)),
                  pl.BlockSpec((1, LANES), lambda i, j: (0, 0)),
                  pl.BlockSpec((1, LANES), lambda i, j: (0, 0))],
        out_specs=[pl.BlockSpec((tm, tn), lambda i, j: (i, jnp.maximum(j - 1, 0))),
                   pl.BlockSpec((tm, LANES), lambda i, j: (i, 0))],
        out_shape=[jax.ShapeDtypeStruct((n, MAIN_COLS), BF16),
                   jax.ShapeDtypeStruct((n, LANES), F32)],
        scratch_shapes=[pltpu.VMEM((tm, d), BF16),
                        pltpu.VMEM((2, tm, tn), F32)],
        compiler_params=_cparams("arbitrary", "arbitrary"),
    )(x2, norm_w, mod3, w_main, w_dt, qn, kn)


ATT_GROUP = 2
ATT_UNROLL = 4
ATT_CHUNK = 256
VT_ROWS = LANES + 16


def _attn_kernel(relb_ref, q_ref, k_ref, vt_ref, lamv_ref, subw_ref, o_ref,
                 q2t_scr, acc_scr, bias_scr, s_scr, smax_scr, *, t, lam_init):
    hp = pl.program_id(1)
    qi = pl.program_id(2)
    tq = 2 * t
    nchunk = vt_ref.shape[0] // ATT_GROUP

    @pl.when(qi == 0)
    def _():
        kk = lax.broadcasted_iota(I32, (t, t), 0)
        qq = lax.broadcasted_iota(I32, (t, t), 1)
        max_exact = REL_BUCKETS // 2
        buckets = []
        for off in (0, t):
            nn = jnp.maximum(qq - kk + off, 0)
            nf = jnp.maximum(nn, 1).astype(F32)
            large = max_exact + (jnp.log(nf / max_exact) / math.log(REL_MAX_DIST / max_exact)
                                 * (REL_BUCKETS - max_exact)).astype(I32)
            buckets.append(jnp.where(nn < max_exact, nn, jnp.minimum(large, REL_BUCKETS - 1)))
        zeros = jnp.zeros((t, t), F32)
        masked = jnp.full((t, t), NEG, F32)
        for hh in range(ATT_GROUP):
            head = hp * ATT_GROUP + hh
            for m in range(2):
                far = relb_ref[(REL_BUCKETS - 1) * 2 * ATT_HEADS + head * 2 + m]
                diag, sub = zeros, zeros
                for b in range(REL_BUCKETS - 1):
                    delta = (relb_ref[b * 2 * ATT_HEADS + head * 2 + m] - far) * LOG2E
                    diag = jnp.where(buckets[0] == b, delta, diag)
                    sub = jnp.where(buckets[1] == b, delta, sub)
                diag = jnp.where(qq >= kk, diag, NEG)
                for tile, (first, last) in enumerate(((sub, zeros), (diag, sub), (masked, diag))):
                    bias_scr[hh, tile, :, m * tq:m * tq + t] = first
                    bias_scr[hh, tile, :, m * tq + t:(m + 1) * tq] = last

    d_lo = lax.broadcasted_iota(I32, (LANES, tq), 0) < HEAD_DIM
    for hh in range(ATT_GROUP):
        qt = q_ref[:, hh * LANES:(hh + 1) * LANES].astype(F32).T
        q2t_scr[hh, :, 0:tq] = jnp.where(d_lo, qt, 0.0).astype(BF16)
        q2t_scr[hh, :, tq:2 * tq] = jnp.where(d_lo, 0.0, qt).astype(BF16)
        acc_scr[hh] = jnp.zeros((VT_ROWS, 2 * tq), F32)

    def scores(c, hh):
        k_c = k_ref[pl.ds(pl.multiple_of(c * t, t), t), hh * LANES:(hh + 1) * LANES]
        return jnp.dot(k_c, q2t_scr[hh], preferred_element_type=F32)

    def issue(c, slot, tile):
        for hh in range(ATT_GROUP):
            s = scores(c, hh)
            if tile is not None:
                s = s + bias_scr[hh, tile]
            s_scr[slot, hh] = s
            row = slot * ATT_GROUP + hh
            smax_scr[row:row + 1, :] = jnp.max(s, axis=0, keepdims=True)

    ones_rows = jnp.where(lax.broadcasted_iota(I32, (VT_ROWS - LANES, t), 0) == 0, 1.0, 0.0).astype(BF16)

    def consume(c, carry, slot):
        out = []
        for hh in range(ATT_GROUP):
            m_prev = carry[hh]
            row = slot * ATT_GROUP + hh
            m_new = jnp.maximum(m_prev, smax_scr[row:row + 1, :])
            alpha = jnp.exp2(m_prev - m_new)
            p = jnp.exp2(s_scr[slot, hh] - m_new).astype(BF16)
            vt_c = jnp.concatenate([vt_ref[hh * nchunk + c], ones_rows], axis=0)
            pv = jnp.dot(vt_c, p, preferred_element_type=F32)
            acc_scr[hh] = alpha * acc_scr[hh] + pv
            out.append(m_new)
        return tuple(out)

    def run(first, tiles, more, reissue, carry):
        if reissue:
            issue(first, 0, tiles[0])
        for k in range(len(tiles)):
            if k + 1 < len(tiles):
                issue(first + k + 1, (k + 1) % 2, tiles[k + 1])
            elif more:
                issue(first + k + 1, (k + 1) % 2, None)
            carry = consume(first + k, carry, k % 2)
        return carry

    nfar = jnp.maximum(2 * qi - 1, 0)
    ntrip = nfar // ATT_UNROLL
    issue(0, 0, None)

    def trip(i, carry):
        return run(i * ATT_UNROLL, [None] * ATT_UNROLL, True, False, carry)

    init = tuple(jnp.full((1, 2 * tq), NEG, F32) for _ in range(ATT_GROUP))
    carry = lax.fori_loop(0, ntrip, trip, init)
    rest = ntrip * ATT_UNROLL
    tails = [functools.partial(run, rest, [None] * r + [0, 1, 2], False, False)
             for r in range(1, ATT_UNROLL, 2)]
    tails.append(functools.partial(run, rest, [1, 2], False, True))
    carry = lax.switch(jnp.where(qi == 0, len(tails) - 1, (nfar - rest) // 2), tails, carry)

    lv = lamv_ref[...]
    lam = (jnp.exp(jnp.sum(lv[0:1, :] * lv[1:2, :], axis=1, keepdims=True))
           - jnp.exp(jnp.sum(lv[2:3, :] * lv[3:4, :], axis=1, keepdims=True)) + lam_init)
    for hh in range(ATT_GROUP):
        acc = acc_scr[hh, 0:LANES, :] * (1.0 / acc_scr[hh, LANES:LANES + 1, :])
        o = (acc[:, 0:tq] - lam * acc[:, tq:2 * tq]).T
        o = o * lax.rsqrt(jnp.mean(o * o, axis=1, keepdims=True) + NORM_EPS) * subw_ref[...]
        o_ref[:, hh * LANES:(hh + 1) * LANES] = (o * (1.0 - lam_init)).astype(BF16)


def _attention(relb, proj, lamv, subw, batch, seq, lam_init):
    n = batch * seq
    t = ATT_CHUNK
    tq = 2 * t
    nq = seq // tq
    nk = seq // t
    gw = ATT_GROUP * LANES
    ngroups = ATT_HEADS // ATT_GROUP
    vt = proj[:, 2 * ATT_WIDTH:3 * ATT_WIDTH].reshape(batch, nk, t, ATT_HEADS, LANES)
    vt = vt.transpose(0, 3, 1, 4, 2).reshape(batch * ATT_HEADS * nk, LANES, t)
    return pl.pallas_call(
        functools.partial(_attn_kernel, t=t, lam_init=lam_init),
        grid=(batch, ngroups, nq),
        in_specs=[pl.BlockSpec(memory_space=pltpu.SMEM),
                  pl.BlockSpec((tq, gw), lambda b, g, i: (b * nq + i, g)),
                  pl.BlockSpec((seq, gw), lambda b, g, i: (b, ngroups + g)),
                  pl.BlockSpec((ATT_GROUP * nk, LANES, t), lambda b, g, i: (b * ngroups + g, 0, 0)),
                  pl.BlockSpec((8, LANES), lambda b, g, i: (0, 0)),
                  pl.BlockSpec((1, LANES), lambda b, g, i: (0, 0))],
        out_specs=pl.BlockSpec((tq, gw), lambda b, g, i: (b * nq + i, g)),
        out_shape=jax.ShapeDtypeStruct((n, ATT_WIDTH), BF16),
        scratch_shapes=[pltpu.VMEM((ATT_GROUP, LANES, 2 * tq), BF16),
                        pltpu.VMEM((ATT_GROUP, VT_ROWS, 2 * tq), F32),
                        pltpu.VMEM((ATT_GROUP, 3, t, 2 * tq), F32),
                        pltpu.VMEM((2, ATT_GROUP, t, 2 * tq), F32),
                        pltpu.VMEM((8, 2 * tq), F32)],
        compiler_params=_cparams("arbitrary", "arbitrary", "arbitrary"),
    )(relb, proj, proj, vt, lamv, subw)


def _split3(x):
    hi = x.astype(BF16)
    r1 = x - hi.astype(F32)
    mid = r1.astype(BF16)
    lo = (r1 - mid.astype(F32)).astype(BF16)
    return hi, mid, lo


def _ssd_kernel(z_ref, xs_ref, bc_ref, dt_ref, cw_ref, cb_ref, dtb_ref, alog_ref, dskip_ref, nw_ref,
                o_ref, xpad_scr, state_scr):
    L = SSM_CHUNK
    W = SSM_WIDTH
    P2 = LANES
    nblk = W // P2
    gw = W // SSM_GROUPS

    @pl.when(pl.program_id(0) == 0)
    def _():
        xpad_scr[0:8, :] = jnp.zeros((8, W + SSM_BC), F32)
        state_scr[...] = jnp.zeros((SSM_STATE, W), F32)

    xpad_scr[8:8 + L, 0:W] = xs_ref[...].astype(F32)
    xpad_scr[8:8 + L, W:W + SSM_BC] = bc_ref[...].astype(F32)
    conv = cb_ref[...] + cw_ref[0:1, :] * xpad_scr[5:5 + L, :]
    for kk in range(1, SSM_CONV):
        conv = conv + cw_ref[kk:kk + 1, :] * xpad_scr[5 + kk:5 + kk + L, :]
    xpad_scr[0:8, :] = xpad_scr[L:L + 8, :]
    u = conv * _sigmoid(conv)

    dtr = dt_ref[...] + dtb_ref[...]
    dt = jnp.maximum(dtr, 0.0) + jnp.log1p(jnp.exp(-jnp.abs(dtr)))
    a = -jnp.exp(alog_ref[...])
    da = dt * a

    ri = lax.broadcasted_iota(I32, (L, L), 0)
    ci = lax.broadcasted_iota(I32, (L, L), 1)
    tril = ri >= ci
    tri = jnp.where(tril, 1.0, 0.0).astype(BF16)
    hi, mid, lo3 = _split3(da)
    a_cs = (jnp.dot(tri, hi, preferred_element_type=F32) + jnp.dot(tri, mid, preferred_element_type=F32)
            + jnp.dot(tri, lo3, preferred_element_type=F32))
    a_cs_t = a_cs.T
    a_last = a_cs[L - 1:L, :]
    e_cs = jnp.exp(a_cs)
    dt_ds = dt * jnp.exp(a_last - a_cs)

    lane_lo = lax.broadcasted_iota(I32, (1, P2), 1) < HEAD_DIM

    def expand(mat, i):
        return jnp.where(lane_lo, mat[:, 2 * i:2 * i + 1], mat[:, 2 * i + 1:2 * i + 2])

    y_blocks = []
    for g in range(SSM_GROUPS):
        bm = u[:, W + g * SSM_STATE:W + (g + 1) * SSM_STATE]
        cm = u[:, W + (SSM_GROUPS + g) * SSM_STATE:W + (SSM_GROUPS + g + 1) * SSM_STATE]
        bm16 = bm.astype(BF16)
        cm16 = cm.astype(BF16)
        cb = lax.dot_general(cm16, bm16, NT_DIMS, preferred_element_type=F32)
        st_g = state_scr[:, g * gw:(g + 1) * gw]
        y_off = jnp.dot(cm16, st_g.astype(BF16), preferred_element_type=F32)
        xd_blocks = []
        for ib in range(nblk // SSM_GROUPS):
            i = g * (nblk // SSM_GROUPS) + ib
            xs_blk = u[:, i * P2:(i + 1) * P2]
            xc = xs_blk * expand(dt, i)
            yd = jnp.zeros((L, P2), F32)
            for hh in range(2):
                head = 2 * i + hh
                seg = a_cs[:, head:head + 1] - a_cs_t[head:head + 1, :]
                wmat = (cb * jnp.where(tril, jnp.exp(seg), 0.0)).astype(BF16)
                keep = lane_lo if hh == 0 else jnp.logical_not(lane_lo)
                yd = yd + jnp.dot(wmat, jnp.where(keep, xc, 0.0).astype(BF16), preferred_element_type=F32)
            y = yd + y_off[:, ib * P2:(ib + 1) * P2] * expand(e_cs, i) + expand(dskip_ref[...], i) * xs_blk
            zf = z_ref[:, i * P2:(i + 1) * P2].astype(F32)
            y_blocks.append(y * (zf * _sigmoid(zf)))
            xd_blocks.append((xs_blk * expand(dt_ds, i)).astype(BF16))
        xd = jnp.concatenate(xd_blocks, axis=1)
        st_new = jnp.dot(bm.T.astype(BF16), xd, preferred_element_type=F32)
        decay = jnp.concatenate([expand(jnp.exp(a_last), g * (nblk // SSM_GROUPS) + ib)
                                 for ib in range(nblk // SSM_GROUPS)], axis=1)
        state_scr[:, g * gw:(g + 1) * gw] = st_g * decay + st_new

    per_g = nblk // SSM_GROUPS
    for g in range(SSM_GROUPS):
        blks = y_blocks[g * per_g:(g + 1) * per_g]
        ss = sum(jnp.sum(b * b, axis=1, keepdims=True) for b in blks) * (1.0 / gw)
        inv = lax.rsqrt(ss + NORM_EPS)
        for ib, b in enumerate(blks):
            i = g * per_g + ib
            o_ref[:, i * P2:(i + 1) * P2] = (b * inv * nw_ref[:, i * P2:(i + 1) * P2]).astype(BF16)


def _ssd_batched_kernel(z_ref, xs_ref, bc_ref, dt_ref, cw_ref, cb_ref, dtb_ref, alog_ref, dskip_ref, nw_ref,
                        o_ref, xpad_scr, state_scr):
    for b in range(z_ref.shape[0]):
        _ssd_kernel(z_ref.at[b], xs_ref.at[b], bc_ref.at[b], dt_ref.at[b], cw_ref, cb_ref, dtb_ref,
                    alog_ref, dskip_ref, nw_ref, o_ref.at[b], xpad_scr.at[b], state_scr.at[b])


def _ssd(proj, dt_raw, conv_w, conv_b, dt_bias, a_log, d_skip, norm_w, batch, seq):
    L = SSM_CHUNK
    cd = SSM_WIDTH + SSM_BC
    proj3 = proj.reshape(batch, seq, MAIN_COLS)
    dt3 = dt_raw.reshape(batch, seq, LANES)
    out = pl.pallas_call(
        _ssd_batched_kernel,
        grid=(seq // L,),
        in_specs=[pl.BlockSpec((batch, L, SSM_WIDTH), lambda c: (0, c, 3)),
                  pl.BlockSpec((batch, L, SSM_WIDTH), lambda c: (0, c, 4)),
                  pl.BlockSpec((batch, L, SSM_BC), lambda c: (0, c, 10)),
                  pl.BlockSpec((batch, L, LANES), lambda c: (0, c, 0)),
                  pl.BlockSpec((SSM_CONV, cd), lambda c: (0, 0)),
                  pl.BlockSpec((1, cd), lambda c: (0, 0)),
                  pl.BlockSpec((1, LANES), lambda c: (0, 0)),
                  pl.BlockSpec((1, LANES), lambda c: (0, 0)),
                  pl.BlockSpec((1, LANES), lambda c: (0, 0)),
                  pl.BlockSpec((1, SSM_WIDTH), lambda c: (0, 0))],
        out_specs=pl.BlockSpec((batch, L, SSM_WIDTH), lambda c: (0, c, 0)),
        out_shape=jax.ShapeDtypeStruct((batch, seq, SSM_WIDTH), BF16),
        scratch_shapes=[pltpu.VMEM((batch, L + 8, cd), F32),
                        pltpu.VMEM((batch, SSM_STATE, SSM_WIDTH), F32)],
        compiler_params=_cparams("arbitrary"),
    )(proj3, proj3, proj3, dt3, conv_w, conv_b, dt_bias, a_log, d_skip, norm_w)
    return out.reshape(batch * seq, SSM_WIDTH)


def _out_proj_kernel(x_ref, att_ref, ssm_ref, w_ref, mod_ref, nw_ref, h_ref, hn_ref):
    mix = (jnp.dot(att_ref[...], w_ref[0:ATT_WIDTH, :], preferred_element_type=F32)
           + jnp.dot(ssm_ref[...], w_ref[ATT_WIDTH:, :], preferred_element_type=F32))
    mod = mod_ref[0]
    h1 = x_ref[...] + mod[2:3, :] * mix
    h_ref[...] = h1
    y = h1 * lax.rsqrt(jnp.mean(h1 * h1, axis=1, keepdims=True) + NORM_EPS) * nw_ref[...]
    hn_ref[...] = (y * (1.0 + mod[4:5, :]) + mod[3:4, :]).astype(BF16)


def _out_proj(x2, att, ssm, w_out, mod3, norm2_w, seq):
    n, d = x2.shape
    tm = min(256, seq)
    return pl.pallas_call(
        _out_proj_kernel,
        grid=(n // tm,),
        in_specs=[pl.BlockSpec((tm, d), lambda i: (i, 0)),
                  pl.BlockSpec((tm, ATT_WIDTH), lambda i: (i, 0)),
                  pl.BlockSpec((tm, SSM_WIDTH), lambda i: (i, 0)),
                  pl.BlockSpec((ATT_WIDTH + SSM_WIDTH, d), lambda i: (0, 0)),
                  pl.BlockSpec((1, 6, d), lambda i: (i * tm // seq, 0, 0)),
                  pl.BlockSpec((1, d), lambda i: (0, 0))],
        out_specs=[pl.BlockSpec((tm, d), lambda i: (i, 0)),
                   pl.BlockSpec((tm, d), lambda i: (i, 0))],
        out_shape=[jax.ShapeDtypeStruct((n, d), F32),
                   jax.ShapeDtypeStruct((n, d), BF16)],
        compiler_params=_cparams("arbitrary"),
    )(x2, att, ssm, w_out, mod3, norm2_w)


def _topk_rows(s, k, rows):
    row = lax.broadcasted_iota(I32, s.shape, 0).astype(F32)
    vals, idxs = [], []
    for _ in range(k):
        m, idx = _argmax_rows(s, rows)
        s = jnp.where(row == idx, -jnp.inf, s)
        vals.append(m)
        idxs.append(idx)
    return vals, idxs


def _argmax_rows(s, rows, payload=None):
    groups = range(rows // 8)
    vs = [s[8 * i:8 * i + 8] for i in groups]
    sub = lax.broadcasted_iota(I32, vs[0].shape, 0).astype(F32)
    rs = [sub + float(8 * i) for i in groups]
    ps = [payload[8 * i:8 * i + 8] for i in groups] if payload is not None else None
    while len(vs) > 1:
        nv, nr, npay = [], [], []
        for i in range(0, len(vs) - 1, 2):
            take_b = vs[i + 1] > vs[i]
            nv.append(jnp.where(take_b, vs[i + 1], vs[i]))
            nr.append(jnp.where(take_b, rs[i + 1], rs[i]))
            if ps is not None:
                npay.append(jnp.where(take_b, ps[i + 1], ps[i]))
        if len(vs) % 2:
            nv.append(vs[-1])
            nr.append(rs[-1])
            if ps is not None:
                npay.append(ps[-1])
        vs, rs, ps = nv, nr, (npay if ps is not None else None)
    m = jnp.max(vs[0], axis=0, keepdims=True)
    idx = jnp.min(jnp.where(vs[0] == m, rs[0], float(rows)), axis=0, keepdims=True)
    if ps is None:
        return m, idx
    return m, idx, jnp.max(jnp.where(rs[0] == idx, ps[0], -1.0), axis=0, keepdims=True)


_CAND = [(i, j) for i in range(PEER_TOPK) for j in range(PEER_TOPK) if (i + 1) * (j + 1) <= PEER_TOPK]


def _route_kernel(hn_ref, wq_ref, keys_ref, a_ref, b_ref, g_ref, top_scr, code_scr):
    t = hn_ref.shape[0]
    qp = jnp.dot(hn_ref[...], wq_ref[...], preferred_element_type=F32).astype(BF16)
    ncand = len(_CAND)
    pad = (-ncand) % 8
    for h in range(PEER_HEADS):
        sub = []
        for c in range(2):
            hc = 2 * h + c
            sc = lax.dot_general(keys_ref[hc], qp[:, hc * LANES:(hc + 1) * LANES], NT_DIMS,
                                 preferred_element_type=F32)
            sub.append(_topk_rows(sc, PEER_TOPK, PEER_NKEYS))
        (s1, i1), (s2, i2) = sub
        cand = jnp.concatenate([s1[i] + s2[j] for i, j in _CAND]
                               + [jnp.full((pad, t), -jnp.inf, F32)], axis=0)
        a_hi = [v * float(PEER_NKEYS) for v in i1]
        code = jnp.concatenate([a_hi[i] + i2[j] for i, j in _CAND] + [jnp.zeros((pad, t), F32)], axis=0)
        row = lax.broadcasted_iota(I32, cand.shape, 0).astype(F32)
        for kk in range(PEER_TOPK):
            m, idx, picked = _argmax_rows(cand, ncand + pad, code)
            slot = h * PEER_TOPK + kk
            top_scr[slot:slot + 1, :] = m
            code_scr[slot:slot + 1, :] = picked
            cand = jnp.where(row == idx, -jnp.inf, cand)
        top = top_scr[h * PEER_TOPK:(h + 1) * PEER_TOPK, :]
        e = jnp.exp(top - jnp.max(top, axis=0, keepdims=True))
        top_scr[h * PEER_TOPK:(h + 1) * PEER_TOPK, :] = e / jnp.sum(e, axis=0, keepdims=True)
    code_t = code_scr[...].T
    first = jnp.floor(code_t * (1.0 / PEER_NKEYS))
    a_ref[...] = first.astype(I32)
    b_ref[...] = (code_t - first * float(PEER_NKEYS)).astype(I32)
    g_ref[...] = top_scr[...].T


def _route(hn2, wq, keys):
    n, d = hn2.shape
    t = min(256, n)
    qd = wq.shape[1]
    out = jax.ShapeDtypeStruct((n, PEER_SLOTS), I32)
    return pl.pallas_call(
        _route_kernel,
        grid=(n // t,),
        in_specs=[pl.BlockSpec((t, d), lambda i: (i, 0)),
                  pl.BlockSpec((d, qd), lambda i: (0, 0)),
                  pl.BlockSpec(keys.shape, lambda i: (0, 0, 0))],
        out_specs=[pl.BlockSpec((t, PEER_SLOTS), lambda i: (i, 0))] * 3,
        out_shape=[out, out, jax.ShapeDtypeStruct((n, PEER_SLOTS), F32)],
        scratch_shapes=[pltpu.VMEM((PEER_SLOTS, t), F32),
                        pltpu.VMEM((PEER_SLOTS, t), F32)],
        compiler_params=_cparams("arbitrary"),
    )(hn2, wq, keys)


PAIR = 2 * PEER_NKEYS
DOWN_PAIRS = 16
DOWN_TOKENS = 512
UP_KEYS = 16
UP_TOKENS = 512
TOKEN_UNROLL = 32
DENSE_PITCH = PEER_NKEYS + 8


def _peer_down_kernel(x_ref, dn_ref, a_ref, b_ref, pre_ref):
    j = pl.program_id(1)

    @pl.when(j == 0)
    def _():
        pre_ref[...] = jnp.zeros(pre_ref.shape, F32)

    x = x_ref[...]
    a = a_ref[...]
    b = b_ref[...]
    pre = pre_ref[...]
    for q in range(DOWN_PAIRS):
        p = lax.dot_general(x, dn_ref[q * PAIR:(q + 1) * PAIR, :], NT_DIMS,
                            preferred_element_type=F32)
        for half in range(2):
            g = jnp.take_along_axis(p[:, half * LANES:(half + 1) * LANES], b, axis=1)
            pre = jnp.where(a == 2 * (DOWN_PAIRS * j + q) + half, g, pre)
    pre_ref[...] = pre


def _peer_down(hn2, down16, aidx, bidx):
    n, d = hn2.shape
    t = min(DOWN_TOKENS, n)
    slot_spec = pl.BlockSpec((t, PEER_SLOTS), lambda i, j: (i, 0))
    return pl.pallas_call(
        _peer_down_kernel,
        grid=(n // t, down16.shape[0] // (DOWN_PAIRS * PAIR)),
        in_specs=[pl.BlockSpec((t, d), lambda i, j: (i, 0)),
                  pl.BlockSpec((DOWN_PAIRS * PAIR, d), lambda i, j: (j, 0)),
                  slot_spec, slot_spec],
        out_specs=slot_spec,
        out_shape=jax.ShapeDtypeStruct((n, PEER_SLOTS), F32),
        compiler_params=_cparams("arbitrary", "arbitrary"),
    )(hn2, down16, aidx, bidx)


U32 = jnp.uint32
HI_HALF = 0xFFFF0000


def _bf16_bits(x):
    return lax.bitcast_convert_type(x.astype(BF16).astype(F32), U32)


def _peer_up_kernel(pre_ref, g_ref, a_ref, b_ref, up_ref, h_ref, mod_ref, o_ref,
                    act_scr, dense_scr):
    j = pl.program_id(1)
    t = pre_ref.shape[0]
    half = t // 2
    nk = PEER_NKEYS

    @pl.when(j == 0)
    def _():
        pre = pre_ref[...]
        act_scr[...] = 0.5 * pre * (1.0 + lax.erf(pre * (1.0 / math.sqrt(2.0)))) * g_ref[...]
        o_ref[...] = jnp.zeros(o_ref.shape, F32)
        row = lax.broadcasted_iota(I32, (nk, PEER_SLOTS), 0)

        def scatter(a_row, b_row, c_row):
            xa = jnp.where(row == a_row, c_row, 0.0).astype(BF16)
            yb = jnp.where(row == b_row, 1.0, 0.0).astype(BF16)
            return lax.dot_general(xa, yb, NT_DIMS, preferred_element_type=F32)

        def body(i, carry):
            for grp in range(TOKEN_UNROLL // 8):
                base = pl.multiple_of(i * TOKEN_UNROLL + grp * 8, 8)
                lo = [r[pl.ds(base, 8), :] for r in (a_ref, b_ref, act_scr)]
                hi = [r[pl.ds(base + half, 8), :] for r in (a_ref, b_ref, act_scr)]
                for u in range(8):
                    d_lo = scatter(*[v[u:u + 1, :] for v in lo])
                    d_hi = scatter(*[v[u:u + 1, :] for v in hi])
                    dense_scr[pl.ds(pl.multiple_of((base + u) * DENSE_PITCH, 8), nk), :] = (
                        lax.shift_right_logical(_bf16_bits(d_lo), U32(16)) | _bf16_bits(d_hi))
            return carry

        lax.fori_loop(0, half // TOKEN_UNROLL, body, 0)

    words = [dense_scr[pl.ds(UP_KEYS * j + u, half, stride=DENSE_PITCH), :] for u in range(UP_KEYS)]
    lo = jnp.concatenate([lax.bitcast_convert_type(lax.shift_left(w, U32(16)), F32) for w in words], axis=1)
    hi = jnp.concatenate([lax.bitcast_convert_type(w & U32(HI_HALF), F32) for w in words], axis=1)
    lhs = jnp.concatenate([lo, hi], axis=0).astype(BF16)
    o_ref[...] += jnp.dot(lhs, up_ref[...], preferred_element_type=F32)

    @pl.when(j == pl.num_programs(1) - 1)
    def _():
        o_ref[...] = h_ref[...] + mod_ref[0][5:6, :] * o_ref[...]


def _peer_up(pre, gate, aidx, bidx, up16, h1, mod3, seq):
    n, d = h1.shape
    t = min(UP_TOKENS, seq)
    slot_spec = pl.BlockSpec((t, PEER_SLOTS), lambda i, j: (i, 0))
    return pl.pallas_call(
        _peer_up_kernel,
        grid=(n // t, up16.shape[0] // (UP_KEYS * PEER_NKEYS)),
        in_specs=[slot_spec, slot_spec, slot_spec, slot_spec,
                  pl.BlockSpec((UP_KEYS * PEER_NKEYS, d), lambda i, j: (j, 0)),
                  pl.BlockSpec((t, d), lambda i, j: (i, 0)),
                  pl.BlockSpec((1, 6, d), lambda i, j: (i * t // seq, 0, 0))],
        out_specs=pl.BlockSpec((t, d), lambda i, j: (i, 0)),
        out_shape=jax.ShapeDtypeStruct((n, d), F32),
        scratch_shapes=[pltpu.VMEM((t, PEER_SLOTS), F32),
                        pltpu.VMEM((t // 2 * DENSE_PITCH, PEER_NKEYS), U32)],
        compiler_params=_cparams("arbitrary", "arbitrary"),
    )(pre, gate, aidx, bidx, up16, h1, mod3)


def _pad_lanes(v):
    return jnp.pad(v.astype(F32), (0, LANES - v.shape[0])).reshape(1, LANES)


def _layer(h2, mod3, l, batch, seq, norm1_w, w_in, q_norm_w, k_norm_w, rel_bias, lambda_q1, lambda_k1,
           lambda_q2, lambda_k2, subln_w, conv_w, conv_b, dt_bias, a_log, d_skip, ssm_norm_w, w_out,
           norm2_w, peer_wq, peer_keys, expert_down, expert_up):
    d = h2.shape[1]
    lam_init = 0.8 - 0.6 * math.exp(-0.3 * l)
    w16 = w_in.astype(BF16)
    w_dt = jnp.pad(w16[:, MAIN_COLS:], ((0, 0), (0, LANES - SSM_HEADS)))
    qn = jnp.tile(q_norm_w.astype(F32) * (HEAD_DIM ** -0.5 * LOG2E), 2).reshape(1, LANES)
    kn = jnp.tile(k_norm_w.astype(F32), 2).reshape(1, LANES)
    proj, dt_raw = _in_proj(h2, norm1_w.reshape(1, d), mod3, w16, w_dt, qn, kn, seq)

    lamv = jnp.pad(jnp.stack([lambda_q1, lambda_k1, lambda_q2, lambda_k2]).astype(F32),
                   ((0, 4), (0, LANES - HEAD_DIM)))
    att = _attention(rel_bias.astype(F32).reshape(-1), proj, lamv, subln_w.reshape(1, LANES),
                     batch, seq, lam_init)
    ssm = _ssd(proj, dt_raw, conv_w, conv_b.reshape(1, -1), _pad_lanes(dt_bias), _pad_lanes(a_log),
               _pad_lanes(d_skip), ssm_norm_w.reshape(1, -1), batch, seq)
    h1, hn2 = _out_proj(h2, att, ssm, w_out.astype(BF16), mod3, norm2_w.reshape(1, d), seq)

    keys = peer_keys.astype(BF16).reshape(2 * PEER_HEADS, PEER_NKEYS, -1)
    aidx, bidx, gate = _route(hn2, peer_wq.astype(BF16), keys)
    pre = _peer_down(hn2, expert_down.astype(BF16), aidx, bidx)
    return _peer_up(pre, gate, aidx, bidx, expert_up.astype(BF16), h1, mod3, seq)


def kernel(x, c, ada_w, ada_b, norm1_w, w_in, q_norm_w, k_norm_w, rel_bias, lambda_q1, lambda_k1, lambda_q2, lambda_k2, subln_w, conv_w, conv_b, dt_bias, a_log, d_skip, ssm_norm_w, w_out, norm2_w, peer_wq, peer_keys, expert_down, expert_up):
    batch, seq, d = x.shape
    depth = ada_w.shape[0]
    assert seq % (2 * ATT_CHUNK) == 0 and seq % min(1024, seq) == 0, "unsupported sequence length"
    assert batch <= 8 and d == ATT_WIDTH + SSM_WIDTH, "unsupported batch / model width"
    h2 = x.reshape(batch * seq, d)
    c_pad = jnp.pad(c, ((0, 8 - batch), (0, 0)))
    for l in range(depth):
        mod = _ada(c_pad, ada_w[l], ada_b[l].reshape(1, -1))
        mod3 = mod[:batch].reshape(batch, 6, d)
        h2 = _layer(h2, mod3, l, batch, seq, norm1_w[l], w_in[l], q_norm_w[l], k_norm_w[l], rel_bias,
                    lambda_q1[l], lambda_k1[l], lambda_q2[l], lambda_k2[l], subln_w[l], conv_w[l],
                    conv_b[l], dt_bias[l], a_log[l], d_skip[l], ssm_norm_w[l], w_out[l], norm2_w[l],
                    peer_wq[l], peer_keys[l], expert_down[l], expert_up[l])
    return h2.reshape(batch, seq, d)
```
